```python
import jax, jax.numpy as jnp
from jax import lax
import numpy as np

D_MODEL = 1024
BATCH = 8
SEQ = 2048
DEPTH = 4
DEC_BATCH = 32
DEC_SEQ = 64
PAST_LEN = 1024

CHUNK = 64
N_A = DEPTH // 2
N_B = DEPTH - N_A
POOL_WINDOWS = (2, 4, 8, 16)
POOL_GROUPS = len(POOL_WINDOWS)
POOL_WIDTH = D_MODEL // 2
POOL_GROUP_DIM = POOL_WIDTH // POOL_GROUPS
POOL_HIST = max(POOL_WINDOWS) - 1
FOX_HEAD_DIM = 64
FOX_WIDTH = D_MODEL // 2
FOX_HEADS = FOX_WIDTH // FOX_HEAD_DIM
MEM_TOKENS = 256
MEM_HEADS = 4
MEM_WIDTH = D_MODEL // 2
MEM_HEAD_DIM = MEM_WIDTH // MEM_HEADS
MIX_WIDTH = POOL_WIDTH + MEM_WIDTH
D_FF = ((8 * D_MODEL // 3 + 255) // 256) * 256
Q_BLOCK = 128
EPS = 1e-6
FORGET_BIAS_INIT = 3.0
FOX_SCALE = FOX_HEAD_DIM ** -0.5
MEM_SCALE = MEM_HEAD_DIM ** -0.5

kernel_name = 'yoco_pool_fox_stream_step'


def rmsnorm(x, g):
    xf = x.astype(jnp.float32)
    y = xf * lax.rsqrt(jnp.mean(xf * xf, axis=-1, keepdims=True) + EPS)
    return (y * g.astype(jnp.float32)).astype(x.dtype)


def pool_mix(u, hist, pos0, w_pool, scale):
    B, L, P = u.shape
    up = jnp.concatenate([hist.astype(u.dtype), u], axis=1).astype(jnp.float32)
    c = jnp.concatenate([jnp.zeros((B, 1, P), jnp.float32), jnp.cumsum(up, axis=1)], axis=1)
    end = c[:, POOL_HIST + 1:]
    pos = pos0 + jnp.arange(L)
    means = []
    for g, w in enumerate(POOL_WINDOWS):
        sl = slice(g * POOL_GROUP_DIM, (g + 1) * POOL_GROUP_DIM)
        start = c[:, POOL_HIST + 1 - w:POOL_HIST + 1 - w + L, sl]
        cnt = jnp.minimum(pos + 1, w).astype(jnp.float32)[None, :, None]
        means.append((end[..., sl] - start) / cnt)
    d = (jnp.concatenate(means, axis=-1) - up[:, POOL_HIST:]).reshape(B, L, POOL_GROUPS, POOL_GROUP_DIM)
    y = jnp.einsum('blgc,gcd->blgd', d, w_pool.astype(jnp.float32)).reshape(B, L, P)
    return (y * scale.astype(jnp.float32)).astype(u.dtype)


def mem_kv(mem, g_mem, w_mem_kv, k_norm_mem):
    B, M, _ = mem.shape
    kv = rmsnorm(mem, g_mem) @ w_mem_kv
    k = rmsnorm(kv[..., :MEM_WIDTH].reshape(B, M, MEM_HEADS, MEM_HEAD_DIM), k_norm_mem)
    v = kv[..., MEM_WIDTH:].reshape(B, M, MEM_HEADS, MEM_HEAD_DIM)
    return k, v


def mem_attend(q, k, v):
    s = jnp.einsum('blhd,bmhd->bhlm', q, k).astype(jnp.float32) * MEM_SCALE
    p = jax.nn.softmax(s, axis=-1).astype(v.dtype)
    return jnp.einsum('bhlm,bmhd->blhd', p, v)


def shared_kv(x, g_kv, w_kv, k_norm_fox, b_f):
    B, L, _ = x.shape
    z = rmsnorm(x, g_kv) @ w_kv
    k = rmsnorm(z[..., :FOX_WIDTH].reshape(B, L, FOX_HEADS, FOX_HEAD_DIM), k_norm_fox)
    v = z[..., FOX_WIDTH:2 * FOX_WIDTH].reshape(B, L, FOX_HEADS, FOX_HEAD_DIM)
    logf = jax.nn.log_sigmoid(z[..., 2 * FOX_WIDTH:].astype(jnp.float32) + b_f.astype(jnp.float32))
    return k, v, logf


def fox_attention(q, k, v, Fq, Fk, q_off):
    B, Lq, H, Dh = q.shape
    kpos = jnp.arange(k.shape[1])
    Fk_t = jnp.swapaxes(Fk, 1, 2)[:, :, None, :]

    def block(args):
        qb, fqb, qpos = args
        s = jnp.einsum('bqhd,bkhd->bhqk', qb, k).astype(jnp.float32) * FOX_SCALE
        s = s + jnp.swapaxes(fqb, 1, 2)[..., None] - Fk_t
        s = jnp.where(kpos[None, :] <= qpos[:, None], s, -jnp.inf)
        p = jax.nn.softmax(s, axis=-1).astype(v.dtype)
        return jnp.einsum('bhqk,bkhd->bqhd', p, v)

    qpos = q_off + jnp.arange(Lq)
    if Lq <= Q_BLOCK:
        return block((q, Fq, qpos))
    nb = Lq // Q_BLOCK
    qs = jnp.swapaxes(q.reshape(B, nb, Q_BLOCK, H, Dh), 0, 1)
    fs = jnp.swapaxes(Fq.reshape(B, nb, Q_BLOCK, H), 0, 1)
    out = lax.map(block, (qs, fs, qpos.reshape(nb, Q_BLOCK)))
    return jnp.swapaxes(out, 0, 1).reshape(B, Lq, H, Dh)


def swiglu(x, g, w_gu, w_down):
    gu = rmsnorm(x, g) @ w_gu
    return (jax.nn.silu(gu[..., :D_FF]) * gu[..., D_FF:]) @ w_down


def trunk(x, pos0, pool_hist, fox_past, mem_k, mem_v, g_mix, w_in, w_out, q_norm_mem,
          w_pool, pool_scale, q_norm_fox, g_kv, w_kv, k_norm_fox, b_f, g_ffn, w_gu, w_down):
    B, L, _ = x.shape
    pool_states = []
    fox_new = None
    for i in range(DEPTH):
        z = rmsnorm(x, g_mix[i]) @ w_in[i]
        qm = rmsnorm(z[..., MIX_WIDTH - MEM_WIDTH:].reshape(B, L, MEM_HEADS, MEM_HEAD_DIM), q_norm_mem[i])
        om = mem_attend(qm, mem_k[i].astype(x.dtype), mem_v[i].astype(x.dtype)).reshape(B, L, MEM_WIDTH)
        if i < N_A:
            u = z[..., :POOL_WIDTH]
            op = pool_mix(u, pool_hist[i], pos0, w_pool[i], pool_scale[i])
            pool_states.append(jnp.concatenate([pool_hist[i].astype(u.dtype), u], axis=1)[:, -POOL_HIST:])
        else:
            if i == N_A:
                k_new, v_new, lf_new = shared_kv(x, g_kv, w_kv, k_norm_fox, b_f)
                fox_new = (k_new, v_new, lf_new)
                if fox_past is None:
                    k_all, v_all, lf_all = k_new, v_new, lf_new
                else:
                    k_all = jnp.concatenate([fox_past[0].astype(k_new.dtype), k_new], axis=1)
                    v_all = jnp.concatenate([fox_past[1].astype(v_new.dtype), v_new], axis=1)
                    lf_all = jnp.concatenate([fox_past[2].astype(jnp.float32), lf_new], axis=1)
                F_all = jnp.cumsum(lf_all, axis=1)
                Fq = F_all[:, -L:]
                q_off = k_all.shape[1] - L
            qf = rmsnorm(z[..., :FOX_WIDTH].reshape(B, L, FOX_HEADS, FOX_HEAD_DIM), q_norm_fox[i - N_A])
            op = fox_attention(qf, k_all, v_all, Fq, F_all, q_off).reshape(B, L, FOX_WIDTH)
        x = x + jnp.concatenate([op, om], axis=-1) @ w_out[i]
        x = x + swiglu(x, g_ffn[i], w_gu[i], w_down[i])
    return x, jnp.stack(pool_states), fox_new


def setup_inputs(seed: int = 0) -> dict:
    key = jax.random.key(seed)
    ks = iter(jax.random.split(key, 40))
    D = D_MODEL

    def nrm(shape, scale=1.0):
        return scale * jax.random.normal(next(ks), shape, jnp.float32)

    def gain(shape):
        return 1.0 + nrm(shape, 0.05)

    inp = {}
    inp['x_prompt'] = nrm((BATCH, SEQ, D))
    inp['x_sample'] = nrm((DEC_BATCH, DEC_SEQ, D))
    inp['state_pool'] = nrm((N_A, DEC_BATCH, POOL_HIST, POOL_WIDTH))
    inp['cache_fox_k'] = nrm((DEC_BATCH, PAST_LEN, FOX_HEADS, FOX_HEAD_DIM))
    inp['cache_fox_v'] = nrm((DEC_BATCH, PAST_LEN, FOX_HEADS, FOX_HEAD_DIM))
    inp['cache_fox_logf'] = jax.nn.log_sigmoid(FORGET_BIAS_INIT + nrm((DEC_BATCH, PAST_LEN, FOX_HEADS)))
    inp['cache_mem_k'] = nrm((DEPTH, DEC_BATCH, MEM_TOKENS, MEM_HEADS, MEM_HEAD_DIM))
    inp['cache_mem_v'] = nrm((DEPTH, DEC_BATCH, MEM_TOKENS, MEM_HEADS, MEM_HEAD_DIM))
    inp['mem_prompt'] = nrm((BATCH, MEM_TOKENS, D))
    inp['g_mix'] = gain((DEPTH, D))
    inp['w_in'] = nrm((DEPTH, D, MIX_WIDTH), D ** -0.5)
    inp['w_out'] = nrm((DEPTH, MIX_WIDTH, D), MIX_WIDTH ** -0.5)
    inp['q_norm_mem'] = gain((DEPTH, MEM_HEAD_DIM))
    inp['g_mem'] = gain((DEPTH, D))
    inp['w_mem_kv'] = nrm((DEPTH, D, 2 * MEM_WIDTH), D ** -0.5)
    inp['k_norm_mem'] = gain((DEPTH, MEM_HEAD_DIM))
    inp['w_pool'] = nrm((N_A, POOL_GROUPS, POOL_GROUP_DIM, POOL_GROUP_DIM), POOL_GROUP_DIM ** -0.5)
    inp['pool_scale'] = 1.0 + nrm((N_A, POOL_WIDTH), 0.1)
    inp['q_norm_fox'] = gain((N_B, FOX_HEAD_DIM))
    inp['g_kv'] = gain((D,))
    inp['w_kv'] = nrm((D, 2 * FOX_WIDTH + FOX_HEADS), D ** -0.5)
    inp['k_norm_fox'] = gain((FOX_HEAD_DIM,))
    inp['b_f'] = FORGET_BIAS_INIT + nrm((FOX_HEADS,), 0.1)
    inp['g_ffn'] = gain((DEPTH, D))
    inp['w_gu'] = nrm((DEPTH, D, 2 * D_FF), D ** -0.5)
    inp['w_down'] = nrm((DEPTH, D_FF, D), D_FF ** -0.5)
    return inp


def reference(x_prompt, x_sample, state_pool, cache_fox_k, cache_fox_v, cache_fox_logf,
              cache_mem_k, cache_mem_v, mem_prompt, g_mix, w_in, w_out, q_norm_mem, g_mem,
              w_mem_kv, k_norm_mem, w_pool, pool_scale, q_norm_fox, g_kv, w_kv, k_norm_fox,
              b_f, g_ffn, w_gu, w_down):
    mks, mvs = [], []
    for i in range(DEPTH):
        mk, mv = mem_kv(mem_prompt, g_mem[i], w_mem_kv[i], k_norm_mem[i])
        mks.append(mk)
        mvs.append(mv)
    mem_k_prompt = jnp.stack(mks)
    mem_v_prompt = jnp.stack(mvs)
    pool_hist0 = jnp.zeros((N_A, x_prompt.shape[0], POOL_HIST, POOL_WIDTH), x_prompt.dtype)
    y_prompt, pool_state_prompt, fox_p = trunk(
        x_prompt, 0, pool_hist0, None, mem_k_prompt, mem_v_prompt, g_mix, w_in, w_out,
        q_norm_mem, w_pool, pool_scale, q_norm_fox, g_kv, w_kv, k_norm_fox, b_f, g_ffn, w_gu, w_down)
    y_sample, pool_state_sample, fox_s = trunk(
        x_sample, PAST_LEN, state_pool, (cache_fox_k, cache_fox_v, cache_fox_logf),
        cache_mem_k, cache_mem_v, g_mix, w_in, w_out, q_norm_mem, w_pool, pool_scale,
        q_norm_fox, g_kv, w_kv, k_norm_fox, b_f, g_ffn, w_gu, w_down)
    fox_k_prompt, fox_v_prompt, fox_logf_prompt = fox_p
    fox_k_sample, fox_v_sample, fox_logf_sample = fox_s
    return (y_prompt, y_sample, pool_state_prompt, fox_k_prompt, fox_v_prompt, fox_logf_prompt,
            mem_k_prompt, mem_v_prompt, pool_state_sample, fox_k_sample, fox_v_sample, fox_logf_sample)
```

```python
import functools

import jax
import jax.numpy as jnp
from jax import lax
from jax.experimental import pallas as pl
from jax.experimental.pallas import tpu as pltpu

F32 = jnp.float32
BF16 = jnp.bfloat16

D_MODEL = 1024
DEPTH = 4
N_A = DEPTH // 2
PAST_LEN = 1024
POOL_WINDOWS = (2, 4, 8, 16)
POOL_GROUPS = len(POOL_WINDOWS)
POOL_WIDTH = D_MODEL // 2
POOL_GROUP_DIM = POOL_WIDTH // POOL_GROUPS
POOL_HIST = max(POOL_WINDOWS) - 1
HIST_ROWS = POOL_HIST + 1
FOX_HEAD_DIM = 64
FOX_WIDTH = D_MODEL // 2
FOX_HEADS = FOX_WIDTH // FOX_HEAD_DIM
MEM_TOKENS = 256
MEM_HEADS = 4
MEM_WIDTH = D_MODEL // 2
MEM_HEAD_DIM = MEM_WIDTH // MEM_HEADS
MIX_WIDTH = POOL_WIDTH + MEM_WIDTH
D_FF = ((8 * D_MODEL // 3 + 255) // 256) * 256
EPS = 1e-6
FOX_SCALE = FOX_HEAD_DIM ** -0.5
MEM_SCALE = MEM_HEAD_DIM ** -0.5

LANES = 128
ROW_BLOCK = 512
KV_COLS = 2 * FOX_WIDTH + LANES
FFN_CHUNKS = 2
VMEM_LIMIT = 56 * 1024 * 1024


def _dot(a, b):
    return jnp.dot(a, b, preferred_element_type=F32)


def _dot_nt(a, b):
    return lax.dot_general(a, b, (((1,), (1,)), ((), ())), preferred_element_type=F32)


def _dot_tn(a, b):
    return lax.dot_general(a, b, (((0,), (0,)), ((), ())), preferred_element_type=F32)


def _rms(x, g):
    ms = jnp.mean(x * x, axis=-1, keepdims=True)
    return (x * lax.rsqrt(ms + EPS)) * g


def _rms_head64(x, g_pair):
    lo = lax.broadcasted_iota(jnp.int32, (1, LANES), 1) < FOX_HEAD_DIM
    outs = []
    for c in range(x.shape[-1] // LANES):
        xc = x[:, c * LANES:(c + 1) * LANES]
        sq = xc * xc
        s_lo = jnp.sum(jnp.where(lo, sq, 0.0), axis=-1, keepdims=True)
        s_hi = jnp.sum(jnp.where(lo, 0.0, sq), axis=-1, keepdims=True)
        ms = jnp.where(lo, s_lo, s_hi) * (1.0 / FOX_HEAD_DIM)
        outs.append((xc * lax.rsqrt(ms + EPS)) * g_pair)
    return outs


def _const_spec(shape):
    return pl.BlockSpec(shape, lambda *_: (0,) * len(shape), pipeline_mode=pl.Buffered(1))


def _params(n_grid):
    return pltpu.CompilerParams(
        dimension_semantics=("arbitrary",) * n_grid, vmem_limit_bytes=VMEM_LIMIT)


def _mem_kv_kernel(mem_ref, g_ref, w_ref, kn_ref, k_ref, v_ref):
    nb = mem_ref.shape[0]
    x = mem_ref[...].reshape(nb * MEM_TOKENS, D_MODEL)
    kv = _dot(_rms(x, g_ref[0]).astype(BF16), w_ref[0])
    for h in range(MEM_HEADS):
        sl = slice(h * MEM_HEAD_DIM, (h + 1) * MEM_HEAD_DIM)
        k_ref[0, :, :, sl] = _rms(kv[:, sl], kn_ref[0]).reshape(nb, MEM_TOKENS, MEM_HEAD_DIM)
    v_ref[0] = kv[:, MEM_WIDTH:].reshape(nb, MEM_TOKENS, MEM_WIDTH)


def _mem_kv(mem, g_mem, w_mem_kv, k_norm_mem):
    B = mem.shape[0]
    nb = 4
    out = jax.ShapeDtypeStruct((DEPTH, B, MEM_TOKENS, MEM_WIDTH), F32)
    return pl.pallas_call(
        _mem_kv_kernel,
        grid=(DEPTH, B // nb),
        in_specs=[
            pl.BlockSpec((nb, MEM_TOKENS, D_MODEL), lambda i, b: (b, 0, 0)),
            pl.BlockSpec((1, 1, D_MODEL), lambda i, b: (i, 0, 0)),
            pl.BlockSpec((1, D_MODEL, 2 * MEM_WIDTH), lambda i, b: (i, 0, 0)),
            pl.BlockSpec((1, 1, MEM_HEAD_DIM), lambda i, b: (i, 0, 0)),
        ],
        out_specs=[
            pl.BlockSpec((1, nb, MEM_TOKENS, MEM_WIDTH), lambda i, b: (i, b, 0, 0)),
            pl.BlockSpec((1, nb, MEM_TOKENS, MEM_WIDTH), lambda i, b: (i, b, 0, 0)),
        ],
        out_shape=[out, out],
        compiler_params=_params(2),
        name="mem_kv",
    )(mem, g_mem.reshape(DEPTH, 1, D_MODEL), w_mem_kv, k_norm_mem.reshape(DEPTH, 1, MEM_HEAD_DIM))


def _in_proj(x_ref, g_ref, w_ref):
    nb, tl, _ = x_ref.shape
    x = x_ref[...].reshape(nb * tl, D_MODEL)
    return _dot(_rms(x, g_ref[...]).astype(BF16), w_ref[...])


def _mem_attend(zq, qn, mk, mv):
    outs = []
    for h in range(MEM_HEADS):
        sl = slice(h * MEM_HEAD_DIM, (h + 1) * MEM_HEAD_DIM)
        q = _rms(zq[:, sl], qn).astype(BF16)
        s = _dot_nt(q, mk[:, sl]) * MEM_SCALE
        p = jnp.exp(s - jnp.max(s, axis=-1, keepdims=True))
        p = p * (1.0 / jnp.sum(p, axis=-1, keepdims=True))
        outs.append(_dot(p.astype(BF16), mv[:, sl]))
    return outs


def _mixer_pool_kernel(x_ref, g_ref, w_in_ref, qn_ref, mk_ref, mv_ref, hist_ref, wp_ref, ps_ref,
                       cat_ref, state_ref, ubuf, *, pos0):
    nb, tl, _ = x_ref.shape
    j = pl.program_id(1)
    z = _in_proj(x_ref, g_ref, w_in_ref)

    @pl.when(j == 0)
    def _():
        ubuf[:, 0:HIST_ROWS, :] = hist_ref[...]

    pos = pos0 + j * tl + lax.broadcasted_iota(jnp.int32, (tl, 1), 0)
    for i in range(nb):
        zi = z[i * tl:(i + 1) * tl]
        u = zi[:, :POOL_WIDTH]
        ubuf[i, HIST_ROWS:HIST_ROWS + tl, :] = u
        for g, w in enumerate(POOL_WINDOWS):
            sl = slice(g * POOL_GROUP_DIM, (g + 1) * POOL_GROUP_DIM)
            ug = u[:, sl]
            acc = ug
            for k in range(1, w):
                acc = acc + ubuf[i, HIST_ROWS - k:HIST_ROWS - k + tl, sl]
            cnt = jnp.minimum(pos + 1, w).astype(F32)
            d = acc / cnt - ug
            y = _dot(d.astype(BF16), wp_ref[g]) * ps_ref[:, sl]
            cat_ref[i, :, sl] = y.astype(cat_ref.dtype)
        om = _mem_attend(zi[:, POOL_WIDTH:], qn_ref[...], mk_ref[i].astype(BF16), mv_ref[i].astype(BF16))
        for h in range(MEM_HEADS):
            lo = POOL_WIDTH + h * MEM_HEAD_DIM
            cat_ref[i, :, lo:lo + MEM_HEAD_DIM] = om[h].astype(cat_ref.dtype)
        tail = ubuf[i, tl:tl + HIST_ROWS, :]
        state_ref[i] = tail
        ubuf[i, 0:HIST_ROWS, :] = tail


def _mixer_pool(x, hist, mem_k, mem_v, g, w_in, qn, w_pool, pool_scale, *, pos0, nb, tl):
    B, L, _ = x.shape
    assert tl >= HIST_ROWS and L % tl == 0 and B % nb == 0
    return pl.pallas_call(
        functools.partial(_mixer_pool_kernel, pos0=pos0),
        grid=(B // nb, L // tl),
        in_specs=[
            pl.BlockSpec((nb, tl, D_MODEL), lambda b, j: (b, j, 0)),
            _const_spec((1, D_MODEL)),
            _const_spec((D_MODEL, MIX_WIDTH)),
            _const_spec((1, MEM_HEAD_DIM)),
            pl.BlockSpec((nb, MEM_TOKENS, MEM_WIDTH), lambda b, j: (b, 0, 0)),
            pl.BlockSpec((nb, MEM_TOKENS, MEM_WIDTH), lambda b, j: (b, 0, 0)),
            pl.BlockSpec((nb, HIST_ROWS, POOL_WIDTH), lambda b, j: (b, 0, 0)),
            _const_spec((POOL_GROUPS, POOL_GROUP_DIM, POOL_GROUP_DIM)),
            _const_spec((1, POOL_WIDTH)),
        ],
        out_specs=[
            pl.BlockSpec((nb, tl, MIX_WIDTH), lambda b, j: (b, j, 0)),
            pl.BlockSpec((nb, HIST_ROWS, POOL_WIDTH), lambda b, j: (b, 0, 0)),
        ],
        out_shape=[
            jax.ShapeDtypeStruct((B, L, MIX_WIDTH), BF16),
            jax.ShapeDtypeStruct((B, HIST_ROWS, POOL_WIDTH), F32),
        ],
        scratch_shapes=[pltpu.VMEM((nb, HIST_ROWS + tl, POOL_WIDTH), F32)],
        compiler_params=_params(2),
        name="mixer_pool",
    )(x, g.reshape(1, D_MODEL), w_in, qn.reshape(1, MEM_HEAD_DIM), mem_k, mem_v, hist, w_pool,
      pool_scale.reshape(1, POOL_WIDTH))


def _kv_proj_kernel(x_ref, g_ref, w_ref, kn_ref, bf_ref, k_ref, v_ref, lf_ref, kb_ref, vb_ref):
    z = _dot(_rms(x_ref[...], g_ref[...]).astype(BF16), w_ref[...])
    k = jnp.concatenate(_rms_head64(z[:, :FOX_WIDTH], kn_ref[...]), axis=-1)
    v = z[:, FOX_WIDTH:2 * FOX_WIDTH]
    k_ref[...] = k
    v_ref[...] = v
    kb_ref[...] = k.astype(BF16)
    vb_ref[...] = v.astype(BF16)
    t = -(z[:, 2 * FOX_WIDTH:2 * FOX_WIDTH + FOX_HEADS] + bf_ref[...])
    lf_ref[...] = -(jnp.maximum(t, 0.0) + jnp.log1p(jnp.exp(-jnp.abs(t))))


def _kv_proj(x2, g_kv, w_kv_pad, kn_pair, b_f):
    T = x2.shape[0]
    rows = pl.BlockSpec((ROW_BLOCK, FOX_WIDTH), lambda r: (r, 0))
    return pl.pallas_call(
        _kv_proj_kernel,
        grid=(T // ROW_BLOCK,),
        in_specs=[
            pl.BlockSpec((ROW_BLOCK, D_MODEL), lambda r: (r, 0)),
            _const_spec((1, D_MODEL)),
            _const_spec((D_MODEL, KV_COLS)),
            _const_spec((1, LANES)),
            _const_spec((1, FOX_HEADS)),
        ],
        out_specs=[rows, rows, pl.BlockSpec((ROW_BLOCK, FOX_HEADS), lambda r: (r, 0)), rows, rows],
        out_shape=[
            jax.ShapeDtypeStruct((T, FOX_WIDTH), F32),
            jax.ShapeDtypeStruct((T, FOX_WIDTH), F32),
            jax.ShapeDtypeStruct((T, FOX_HEADS), F32),
            jax.ShapeDtypeStruct((T, FOX_WIDTH), BF16),
            jax.ShapeDtypeStruct((T, FOX_WIDTH), BF16),
        ],
        compiler_params=_params(1),
        name="kv_proj",
    )(x2, g_kv.reshape(1, D_MODEL), w_kv_pad, kn_pair, b_f.reshape(1, FOX_HEADS))


def _cumsum_kernel(lf_ref, f_ref):
    rows, n = lf_ref.shape
    r = lax.broadcasted_iota(jnp.int32, (LANES, LANES), 0)
    c = lax.broadcasted_iota(jnp.int32, (LANES, LANES), 1)
    tri = jnp.where(r <= c, 1.0, 0.0).astype(BF16)
    carry = jnp.zeros((rows, 1), F32)
    for ch in range(n // LANES):
        x = lf_ref[:, ch * LANES:(ch + 1) * LANES]
        hi = x.astype(BF16)
        r1 = x - hi.astype(F32)
        mid = r1.astype(BF16)
        low = (r1 - mid.astype(F32)).astype(BF16)
        y = (_dot(hi, tri) + _dot(mid, tri)) + _dot(low, tri) + carry
        f_ref[:, ch * LANES:(ch + 1) * LANES] = y
        carry = y[:, LANES - 1:LANES]


def _cumsum_lanes(lf_t):
    return pl.pallas_call(
        _cumsum_kernel,
        out_shape=jax.ShapeDtypeStruct(lf_t.shape, F32),
        name="logf_cumsum",
    )(lf_t)


def _masked_pair(x):
    lo = lax.broadcasted_iota(jnp.int32, (1, LANES), 1) < FOX_HEAD_DIM
    return jnp.where(lo, x, 0.0).astype(BF16), jnp.where(lo, 0.0, x).astype(BF16), lo


def _mixer_fox_prompt_kernel(x_ref, g_ref, w_in_ref, qn_ref, mk_ref, mv_ref, qnf_ref,
                             k_ref, v_ref, ft_ref, fq_ref, cat_ref):
    _, tl, _ = x_ref.shape
    j = pl.program_id(1)
    z = _in_proj(x_ref, g_ref, w_in_ref)
    qs = _rms_head64(z[:, :FOX_WIDTH], qnf_ref[...])
    causal = (lax.broadcasted_iota(jnp.int32, (tl, 1), 0)
              >= lax.broadcasted_iota(jnp.int32, (1, tl), 1))
    neg = jnp.full((tl, 1), -jnp.inf, F32)
    zero = jnp.zeros((tl, 1), F32)

    for c in range(FOX_HEADS // 2):
        sl = slice(c * LANES, (c + 1) * LANES)
        q_a, q_b, lo = _masked_pair(qs[c] * FOX_SCALE)
        fq_a = fq_ref[0, :, 2 * c:2 * c + 1]
        fq_b = fq_ref[0, :, 2 * c + 1:2 * c + 2]

        def step(kb, carry, masked):
            m_a, l_a, m_b, l_b, acc = carry
            kp = k_ref[0, kb, :, sl]
            vp = v_ref[0, kb, :, sl]
            new = []
            pv = []
            for q, fq, h, m, l in ((q_a, fq_a, 2 * c, m_a, l_a), (q_b, fq_b, 2 * c + 1, m_b, l_b)):
                s = (_dot_nt(q, kp) + fq) - ft_ref[0, kb, h:h + 1, :]
                if masked:
                    s = jnp.where(causal, s, -jnp.inf)
                m_new = jnp.maximum(m, jnp.max(s, axis=-1, keepdims=True))
                p = jnp.exp(s - m_new)
                alpha = jnp.exp(m - m_new)
                new.append((m_new, alpha * l + jnp.sum(p, axis=-1, keepdims=True), alpha))
                pv.append(_dot(p.astype(BF16), vp))
            (m_a, l_a, al_a), (m_b, l_b, al_b) = new
            acc = jnp.where(lo, al_a, al_b) * acc + jnp.where(lo, pv[0], pv[1])
            return m_a, l_a, m_b, l_b, acc

        init = (neg, zero, neg, zero, jnp.zeros((tl, LANES), F32))
        carry = lax.fori_loop(0, j, functools.partial(step, masked=False), init)
        _, l_a, _, l_b, acc = step(j, carry, masked=True)
        cat_ref[0, :, sl] = (acc * (1.0 / jnp.where(lo, l_a, l_b))).astype(cat_ref.dtype)

    om = _mem_attend(z[:, FOX_WIDTH:], qn_ref[...], mk_ref[0].astype(BF16), mv_ref[0].astype(BF16))
    for h in range(MEM_HEADS):
        o = FOX_WIDTH + h * MEM_HEAD_DIM
        cat_ref[0, :, o:o + MEM_HEAD_DIM] = om[h].astype(cat_ref.dtype)


def _mixer_fox_prompt(x, k_b, v_b, f_t, f_q, mem_k, mem_v, g, w_in, qn, qnf_pair, *, tl):
    B, L, _ = x.shape
    nkb = L // tl
    k4 = k_b.reshape(B, nkb, tl, FOX_WIDTH)
    v4 = v_b.reshape(B, nkb, tl, FOX_WIDTH)
    ft4 = jnp.swapaxes(f_t.reshape(B, FOX_HEADS, nkb, tl), 1, 2)
    kv_spec = pl.BlockSpec((1, nkb, tl, FOX_WIDTH), lambda b, j: (b, 0, 0, 0))
    return pl.pallas_call(
        _mixer_fox_prompt_kernel,
        grid=(B, nkb),
        in_specs=[
            pl.BlockSpec((1, tl, D_MODEL), lambda b, j: (b, j, 0)),
            _const_spec((1, D_MODEL)),
            _const_spec((D_MODEL, MIX_WIDTH)),
            _const_spec((1, MEM_HEAD_DIM)),
            pl.BlockSpec((1, MEM_TOKENS, MEM_WIDTH), lambda b, j: (b, 0, 0)),
            pl.BlockSpec((1, MEM_TOKENS, MEM_WIDTH), lambda b, j: (b, 0, 0)),
            _const_spec((1, LANES)),
            kv_spec,
            kv_spec,
            pl.BlockSpec((1, nkb, FOX_HEADS, tl), lambda b, j: (b, 0, 0, 0)),
            pl.BlockSpec((1, tl, FOX_HEADS), lambda b, j: (b, j, 0)),
        ],
        out_specs=pl.BlockSpec((1, tl, MIX_WIDTH), lambda b, j: (b, j, 0)),
        out_shape=jax.ShapeDtypeStruct((B, L, MIX_WIDTH), BF16),
        compiler_params=_params(2),
        name="mixer_fox_prompt",
    )(x, g.reshape(1, D_MODEL), w_in, qn.reshape(1, MEM_HEAD_DIM), mem_k, mem_v, qnf_pair,
      k4, v4, ft4, f_q)


def _mixer_fox_sample_kernel(x_ref, g_ref, w_in_ref, qn_ref, mk_ref, mv_ref, qnf_ref,
                             kp_ref, vp_ref, kn_ref, vn_ref, ft_ref, fq_ref, cat_ref):
    nb, tl, _ = x_ref.shape
    past = kp_ref.shape[1]
    z = _in_proj(x_ref, g_ref, w_in_ref)
    causal = (lax.broadcasted_iota(jnp.int32, (tl, 1), 0)
              >= lax.broadcasted_iota(jnp.int32, (1, tl), 1))
    for i in range(nb):
        zi = z[i * tl:(i + 1) * tl]
        qs = _rms_head64(zi[:, :FOX_WIDTH], qnf_ref[...])
        for c in range(FOX_HEADS // 2):
            sl = slice(c * LANES, (c + 1) * LANES)
            q_a, q_b, lo = _masked_pair(qs[c] * FOX_SCALE)
            k_past = kp_ref[i, :, sl].astype(BF16)
            v_past = vp_ref[i, :, sl].astype(BF16)
            k_new = kn_ref[i, :, sl].astype(BF16)
            v_new = vn_ref[i, :, sl].astype(BF16)
            outs = []
            for q, h in ((q_a, 2 * c), (q_b, 2 * c + 1)):
                fq = fq_ref[i, :, h:h + 1]
                s_p = (_dot_nt(q, k_past) + fq) - ft_ref[i, h:h + 1, 0:past]
                s_n = (_dot_nt(q, k_new) + fq) - ft_ref[i, h:h + 1, past:past + tl]
                s_n = jnp.where(causal, s_n, -jnp.inf)
                m = jnp.maximum(jnp.max(s_p, axis=-1, keepdims=True),
                                jnp.max(s_n, axis=-1, keepdims=True))
                p_p = jnp.exp(s_p - m)
                p_n = jnp.exp(s_n - m)
                l = jnp.sum(p_p, axis=-1, keepdims=True) + jnp.sum(p_n, axis=-1, keepdims=True)
                o = _dot(p_p.astype(BF16), v_past) + _dot(p_n.astype(BF16), v_new)
                outs.append(o * (1.0 / l))
            cat_ref[i, :, sl] = jnp.where(lo, outs[0], outs[1]).astype(cat_ref.dtype)
        om = _mem_attend(zi[:, FOX_WIDTH:], qn_ref[...], mk_ref[i].astype(BF16), mv_ref[i].astype(BF16))
        for h in range(MEM_HEADS):
            o = FOX_WIDTH + h * MEM_HEAD_DIM
            cat_ref[i, :, o:o + MEM_HEAD_DIM] = om[h].astype(cat_ref.dtype)


def _mixer_fox_sample(x, k_past, v_past, k_new, v_new, f_t, f_q, mem_k, mem_v, g, w_in, qn,
                      qnf_pair, *, nb):
    B, L, _ = x.shape
    past = k_past.shape[1]
    lk_pad = f_t.shape[-1]
    per_b = lambda *tail: pl.BlockSpec((nb,) + tail, lambda b: (b,) + (0,) * len(tail))
    return pl.pallas_call(
        _mixer_fox_sample_kernel,
        grid=(B // nb,),
        in_specs=[
            per_b(L, D_MODEL),
            _const_spec((1, D_MODEL)),
            _const_spec((D_MODEL, MIX_WIDTH)),
            _const_spec((1, MEM_HEAD_DIM)),
            per_b(MEM_TOKENS, MEM_WIDTH),
            per_b(MEM_TOKENS, MEM_WIDTH),
            _const_spec((1, LANES)),
            per_b(past, FOX_WIDTH),
            per_b(past, FOX_WIDTH),
            per_b(L, FOX_WIDTH),
            per_b(L, FOX_WIDTH),
            per_b(FOX_HEADS, lk_pad),
            per_b(L, FOX_HEADS),
        ],
        out_specs=per_b(L, MIX_WIDTH),
        out_shape=jax.ShapeDtypeStruct((B, L, MIX_WIDTH), BF16),
        compiler_params=_params(1),
        name="mixer_fox_sample",
    )(x, g.reshape(1, D_MODEL), w_in, qn.reshape(1, MEM_HEAD_DIM), mem_k, mem_v, qnf_pair,
      k_past, v_past, k_new, v_new, f_t, f_q)


def _out_ffn_kernel(x_ref, cat_ref, w_out_ref, g_ref, w_gu_ref, w_down_ref, y_ref):
    x1 = x_ref[...] + _dot(cat_ref[...], w_out_ref[...])
    xn = _rms(x1, g_ref[...]).astype(BF16)
    cw = D_FF // FFN_CHUNKS
    acc = x1
    for c in range(FFN_CHUNKS):
        gate = _dot(xn, w_gu_ref[:, c * cw:(c + 1) * cw])
        up = _dot(xn, w_gu_ref[:, D_FF + c * cw:D_FF + (c + 1) * cw])
        h = (gate * (1.0 / (1.0 + jnp.exp(-gate)))) * up
        acc = acc + _dot(h.astype(BF16), w_down_ref[c * cw:(c + 1) * cw, :])
    y_ref[...] = acc


def _out_ffn(x2, cat2, w_out, g_ffn, w_gu, w_down):
    T = x2.shape[0]
    return pl.pallas_call(
        _out_ffn_kernel,
        grid=(T // ROW_BLOCK,),
        in_specs=[
            pl.BlockSpec((ROW_BLOCK, D_MODEL), lambda r: (r, 0)),
            pl.BlockSpec((ROW_BLOCK, MIX_WIDTH), lambda r: (r, 0)),
            _const_spec((MIX_WIDTH, D_MODEL)),
            _const_spec((1, D_MODEL)),
            _const_spec((D_MODEL, 2 * D_FF)),
            _const_spec((D_FF, D_MODEL)),
        ],
        out_specs=pl.BlockSpec((ROW_BLOCK, D_MODEL), lambda r: (r, 0)),
        out_shape=jax.ShapeDtypeStruct((T, D_MODEL), F32),
        compiler_params=_params(1),
        name="out_ffn",
    )(x2, cat2, w_out, g_ffn.reshape(1, D_MODEL), w_gu, w_down)


def _trunk(x, pos0, pool_hist, fox_past, mem_k, mem_v, W, *, nb, tl):
    B, L, _ = x.shape
    pool_states = []
    for i in range(DEPTH):
        if i < N_A:
            cat, state = _mixer_pool(
                x, pool_hist[i], mem_k[i], mem_v[i], W["g_mix"][i], W["w_in"][i],
                W["q_norm_mem"][i], W["w_pool"][i], W["pool_scale"][i], pos0=pos0, nb=nb, tl=tl)
            pool_states.append(state[:, 1:, :])
        else:
            if i == N_A:
                k_new, v_new, lf_new, k_b, v_b = _kv_proj(
                    x.reshape(B * L, D_MODEL), W["g_kv"], W["w_kv"], W["kn_pair"], W["b_f"])
                lf_new = lf_new.reshape(B, L, FOX_HEADS)
                if fox_past is None:
                    lf_all = lf_new
                else:
                    lf_all = jnp.concatenate([fox_past[2], lf_new], axis=1)
                lk = lf_all.shape[1]
                lk_pad = -(-lk // LANES) * LANES
                lf_t = jnp.swapaxes(lf_all, 1, 2).reshape(B * FOX_HEADS, lk)
                lf_t = jnp.pad(lf_t, ((0, 0), (0, lk_pad - lk)))
                f_t = _cumsum_lanes(lf_t).reshape(B, FOX_HEADS, lk_pad)
                f_q = jnp.swapaxes(f_t[:, :, lk - L:lk], 1, 2)
            qnf_pair = jnp.tile(W["q_norm_fox"][i - N_A], 2).reshape(1, LANES)
            if fox_past is None:
                cat = _mixer_fox_prompt(
                    x, k_b.reshape(B, L, FOX_WIDTH), v_b.reshape(B, L, FOX_WIDTH), f_t, f_q,
                    mem_k[i], mem_v[i], W["g_mix"][i], W["w_in"][i], W["q_norm_mem"][i],
                    qnf_pair, tl=tl)
            else:
                cat = _mixer_fox_sample(
                    x, fox_past[0], fox_past[1], k_new.reshape(B, L, FOX_WIDTH),
                    v_new.reshape(B, L, FOX_WIDTH), f_t, f_q, mem_k[i], mem_v[i],
                    W["g_mix"][i], W["w_in"][i], W["q_norm_mem"][i], qnf_pair, nb=2)
        x = _out_ffn(x.reshape(B * L, D_MODEL), cat.reshape(B * L, MIX_WIDTH), W["w_out"][i],
                     W["g_ffn"][i], W["w_gu"][i], W["w_down"][i]).reshape(B, L, D_MODEL)
    fox_new = (k_new.reshape(B, L, FOX_HEADS, FOX_HEAD_DIM),
               v_new.reshape(B, L, FOX_HEADS, FOX_HEAD_DIM), lf_new)
    return x, jnp.stack(pool_states), fox_new


def kernel(x_prompt, x_sample, state_pool, cache_fox_k, cache_fox_v, cache_fox_logf, cache_mem_k,
           cache_mem_v, mem_prompt, g_mix, w_in, w_out, q_norm_mem, g_mem, w_mem_kv, k_norm_mem,
           w_pool, pool_scale, q_norm_fox, g_kv, w_kv, k_norm_fox, b_f, g_ffn, w_gu, w_down):
    B, L, _ = x_prompt.shape
    SB, SL, _ = x_sample.shape
    W = dict(
        g_mix=g_mix, w_in=w_in.astype(BF16), w_out=w_out.astype(BF16), q_norm_mem=q_norm_mem,
        w_pool=w_pool.astype(BF16), pool_scale=pool_scale, q_norm_fox=q_norm_fox, g_kv=g_kv,
        w_kv=jnp.pad(w_kv, ((0, 0), (0, KV_COLS - w_kv.shape[1]))).astype(BF16),
        kn_pair=jnp.tile(k_norm_fox, 2).reshape(1, LANES), b_f=b_f, g_ffn=g_ffn,
        w_gu=w_gu.astype(BF16), w_down=w_down.astype(BF16))

    mem_k_p, mem_v_p = _mem_kv(mem_prompt, g_mem, w_mem_kv.astype(BF16), k_norm_mem)
    hist_p = jnp.zeros((N_A, B, HIST_ROWS, POOL_WIDTH), F32)
    y_p, pool_p, fox_p = _trunk(x_prompt, 0, hist_p, None, mem_k_p, mem_v_p, W, nb=1, tl=ROW_BLOCK)

    hist_s = jnp.pad(state_pool, ((0, 0), (0, 0), (1, 0), (0, 0)))
    past = (cache_fox_k.reshape(SB, PAST_LEN, FOX_WIDTH), cache_fox_v.reshape(SB, PAST_LEN, FOX_WIDTH),
            cache_fox_logf)
    y_s, pool_s, fox_s = _trunk(
        x_sample, PAST_LEN, hist_s, past, cache_mem_k.reshape(DEPTH, SB, MEM_TOKENS, MEM_WIDTH),
        cache_mem_v.reshape(DEPTH, SB, MEM_TOKENS, MEM_WIDTH), W, nb=ROW_BLOCK // SL, tl=SL)

    mem_shape = (DEPTH, B, MEM_TOKENS, MEM_HEADS, MEM_HEAD_DIM)
    return (y_p, y_s, pool_p, fox_p[0], fox_p[1], fox_p[2], mem_k_p.reshape(mem_shape),
            mem_v_p.reshape(mem_shape), pool_s, fox_s[0], fox_s[1], fox_s[2])
```

```python
import functools

import jax
import jax.numpy as jnp
from jax import lax
from jax.experimental import pallas as pl
from jax.experimental.pallas import tpu as pltpu

F32 = jnp.float32
BF16 = jnp.bfloat16

D_MODEL = 1024
DEPTH = 4
N_A = DEPTH // 2
PAST_LEN = 1024
POOL_WINDOWS = (2, 4, 8, 16)
POOL_GROUPS = len(POOL_WINDOWS)
POOL_WIDTH = D_MODEL // 2
POOL_GROUP_DIM = POOL_WIDTH // POOL_GROUPS
POOL_HIST = max(POOL_WINDOWS) - 1
HIST_ROWS = POOL_HIST + 1
FOX_HEAD_DIM = 64
FOX_WIDTH = D_MODEL // 2
FOX_HEADS = FOX_WIDTH // FOX_HEAD_DIM
MEM_TOKENS = 256
MEM_HEADS = 4
MEM_WIDTH = D_MODEL // 2
MEM_HEAD_DIM = MEM_WIDTH // MEM_HEADS
MIX_WIDTH = POOL_WIDTH + MEM_WIDTH
D_FF = ((8 * D_MODEL // 3 + 255) // 256) * 256
EPS = 1e-6
FOX_SCALE = FOX_HEAD_DIM ** -0.5
MEM_SCALE = MEM_HEAD_DIM ** -0.5

LANES = 128
ROW_BLOCK = 512
KV_COLS = 2 * FOX_WIDTH + LANES
FFN_CHUNKS = 2
VMEM_LIMIT = 56 * 1024 * 1024


def _dot(a, b):
    return jnp.dot(a, b, preferred_element_type=F32)


def _dot_nt(a, b):
    return lax.dot_general(a, b, (((1,), (1,)), ((), ())), preferred_element_type=F32)


def _rms(x, g):
    ms = jnp.mean(x * x, axis=-1, keepdims=True)
    return (x * lax.rsqrt(ms + EPS)) * g


def _rms_head64(x, g_pair):
    lo = lax.broadcasted_iota(jnp.int32, (1, LANES), 1) < FOX_HEAD_DIM
    outs = []
    for c in range(x.shape[-1] // LANES):
        xc = x[:, c * LANES:(c + 1) * LANES]
        sq = xc * xc
        s_lo = jnp.sum(jnp.where(lo, sq, 0.0), axis=-1, keepdims=True)
        s_hi = jnp.sum(jnp.where(lo, 0.0, sq), axis=-1, keepdims=True)
        ms = jnp.where(lo, s_lo, s_hi) * (1.0 / FOX_HEAD_DIM)
        outs.append((xc * lax.rsqrt(ms + EPS)) * g_pair)
    return outs


def _const_spec(shape):
    return pl.BlockSpec(shape, lambda *_: (0,) * len(shape), pipeline_mode=pl.Buffered(1))


def _layer_spec(layer, shape):
    return pl.BlockSpec((1,) + shape, lambda *_: (layer,) + (0,) * len(shape),
                        pipeline_mode=pl.Buffered(1))


def _mem_spec(layer, nb):
    return pl.BlockSpec((1, nb, MEM_TOKENS, MEM_HEADS, MEM_HEAD_DIM),
                        lambda b, *_: (layer, b, 0, 0, 0))


def _params(n_grid):
    return pltpu.CompilerParams(
        dimension_semantics=("arbitrary",) * n_grid, vmem_limit_bytes=VMEM_LIMIT)


def _mem_kv_kernel(mem_ref, g_ref, w_ref, kn_ref, k_ref, v_ref):
    nb = mem_ref.shape[0]
    x = mem_ref[...].reshape(nb * MEM_TOKENS, D_MODEL)
    kv = _dot(_rms(x, g_ref[0]).astype(BF16), w_ref[0])
    for h in range(MEM_HEADS):
        ks = slice(h * MEM_HEAD_DIM, (h + 1) * MEM_HEAD_DIM)
        vs = slice(MEM_WIDTH + h * MEM_HEAD_DIM, MEM_WIDTH + (h + 1) * MEM_HEAD_DIM)
        k_ref[0, :, :, h, :] = _rms(kv[:, ks], kn_ref[0]).reshape(nb, MEM_TOKENS, MEM_HEAD_DIM)
        v_ref[0, :, :, h, :] = kv[:, vs].reshape(nb, MEM_TOKENS, MEM_HEAD_DIM)


def _mem_kv(mem, g_mem, w_mem_kv, k_norm_mem):
    B = mem.shape[0]
    nb = 4
    out = jax.ShapeDtypeStruct((DEPTH, B, MEM_TOKENS, MEM_HEADS, MEM_HEAD_DIM), F32)
    out_spec = pl.BlockSpec((1, nb, MEM_TOKENS, MEM_HEADS, MEM_HEAD_DIM),
                            lambda i, b: (i, b, 0, 0, 0))
    return pl.pallas_call(
        _mem_kv_kernel,
        grid=(DEPTH, B // nb),
        in_specs=[
            pl.BlockSpec((nb, MEM_TOKENS, D_MODEL), lambda i, b: (b, 0, 0)),
            pl.BlockSpec((1, 1, D_MODEL), lambda i, b: (i, 0, 0)),
            pl.BlockSpec((1, D_MODEL, 2 * MEM_WIDTH), lambda i, b: (i, 0, 0)),
            pl.BlockSpec((1, 1, MEM_HEAD_DIM), lambda i, b: (i, 0, 0)),
        ],
        out_specs=[out_spec, out_spec],
        out_shape=[out, out],
        compiler_params=_params(2),
        name="mem_kv",
    )(mem, g_mem, w_mem_kv, k_norm_mem)


def _in_proj(x_ref, g_ref, w_ref):
    nb, tl, _ = x_ref.shape
    x = x_ref[...].reshape(nb * tl, D_MODEL)
    return _dot(_rms(x, g_ref[0]).astype(BF16), w_ref[0])


def _mem_attend(zq, qn, mk_ref, mv_ref, i, cat_ref, col0):
    for h in range(MEM_HEADS):
        sl = slice(h * MEM_HEAD_DIM, (h + 1) * MEM_HEAD_DIM)
        q = _rms(zq[:, sl], qn).astype(BF16)
        s = _dot_nt(q, mk_ref[0, i, :, h, :].astype(BF16)) * MEM_SCALE
        p = jnp.exp(s - jnp.max(s, axis=-1, keepdims=True))
        p = p * (1.0 / jnp.sum(p, axis=-1, keepdims=True))
        o = _dot(p.astype(BF16), mv_ref[0, i, :, h, :].astype(BF16))
        cat_ref[i, :, col0 + h * MEM_HEAD_DIM:col0 + (h + 1) * MEM_HEAD_DIM] = o.astype(cat_ref.dtype)


def _mixer_pool_kernel(x_ref, g_ref, w_in_ref, qn_ref, mk_ref, mv_ref, hist_ref, wp_ref, ps_ref,
                       cat_ref, state_ref, ubuf, *, pos0):
    nb, tl, _ = x_ref.shape
    j = pl.program_id(1)
    z = _in_proj(x_ref, g_ref, w_in_ref)

    @pl.when(j == 0)
    def _():
        ubuf[:, 0:HIST_ROWS, :] = hist_ref[0]

    pos = pos0 + j * tl + lax.broadcasted_iota(jnp.int32, (tl, 1), 0)
    for i in range(nb):
        zi = z[i * tl:(i + 1) * tl]
        u = zi[:, :POOL_WIDTH]
        ubuf[i, HIST_ROWS:HIST_ROWS + tl, :] = u
        for g, w in enumerate(POOL_WINDOWS):
            sl = slice(g * POOL_GROUP_DIM, (g + 1) * POOL_GROUP_DIM)
            ug = u[:, sl]
            acc = ug
            for k in range(1, w):
                acc = acc + ubuf[i, HIST_ROWS - k:HIST_ROWS - k + tl, sl]
            cnt = jnp.minimum(pos + 1, w).astype(F32)
            d = acc / cnt - ug
            y = _dot(d.astype(BF16), wp_ref[0, g]) * ps_ref[0, :, sl]
            cat_ref[i, :, sl] = y.astype(cat_ref.dtype)
        _mem_attend(zi[:, POOL_WIDTH:], qn_ref[0], mk_ref, mv_ref, i, cat_ref, POOL_WIDTH)
        tail = ubuf[i, tl:tl + HIST_ROWS, :]
        state_ref[0, i] = tail
        ubuf[i, 0:HIST_ROWS, :] = tail


def _mixer_pool(layer, x, hist, mem_k, mem_v, W, *, pos0, nb, tl):
    B, L, _ = x.shape
    assert tl >= HIST_ROWS and L % tl == 0 and B % nb == 0
    hist_spec = pl.BlockSpec((1, nb, HIST_ROWS, POOL_WIDTH), lambda b, j: (layer, b, 0, 0))
    return pl.pallas_call(
        functools.partial(_mixer_pool_kernel, pos0=pos0),
        grid=(B // nb, L // tl),
        in_specs=[
            pl.BlockSpec((nb, tl, D_MODEL), lambda b, j: (b, j, 0)),
            _layer_spec(layer, (1, D_MODEL)),
            _layer_spec(layer, (D_MODEL, MIX_WIDTH)),
            _layer_spec(layer, (1, MEM_HEAD_DIM)),
            _mem_spec(layer, nb),
            _mem_spec(layer, nb),
            hist_spec,
            _layer_spec(layer, (POOL_GROUPS, POOL_GROUP_DIM, POOL_GROUP_DIM)),
            _layer_spec(layer, (1, POOL_WIDTH)),
        ],
        out_specs=[
            pl.BlockSpec((nb, tl, MIX_WIDTH), lambda b, j: (b, j, 0)),
            pl.BlockSpec((1, nb, HIST_ROWS, POOL_WIDTH), lambda b, j: (0, b, 0, 0)),
        ],
        out_shape=[
            jax.ShapeDtypeStruct((B, L, MIX_WIDTH), BF16),
            jax.ShapeDtypeStruct((1, B, HIST_ROWS, POOL_WIDTH), F32),
        ],
        scratch_shapes=[pltpu.VMEM((nb, HIST_ROWS + tl, POOL_WIDTH), F32)],
        compiler_params=_params(2),
        name="mixer_pool",
    )(x, W["g_mix"], W["w_in"], W["q_norm_mem"], mem_k, mem_v, hist, W["w_pool"], W["pool_scale"])


def _kv_proj_kernel(x_ref, g_ref, w_ref, kn_ref, bf_ref, k_ref, v_ref, lf_ref, kb_ref, vb_ref):
    nb, tl, _ = x_ref.shape
    x = x_ref[...].reshape(nb * tl, D_MODEL)
    z = _dot(_rms(x, g_ref[...]).astype(BF16), w_ref[...])
    ks = _rms_head64(z[:, :FOX_WIDTH], kn_ref[...])
    for c in range(FOX_HEADS // 2):
        kc = ks[c]
        vc = z[:, FOX_WIDTH + c * LANES:FOX_WIDTH + (c + 1) * LANES]
        kb_ref[:, :, c * LANES:(c + 1) * LANES] = kc.astype(BF16).reshape(nb, tl, LANES)
        vb_ref[:, :, c * LANES:(c + 1) * LANES] = vc.astype(BF16).reshape(nb, tl, LANES)
        for half in range(2):
            hs = slice(half * FOX_HEAD_DIM, (half + 1) * FOX_HEAD_DIM)
            k_ref[:, :, 2 * c + half, :] = kc[:, hs].reshape(nb, tl, FOX_HEAD_DIM)
            v_ref[:, :, 2 * c + half, :] = vc[:, hs].reshape(nb, tl, FOX_HEAD_DIM)
    t = -(z[:, 2 * FOX_WIDTH:2 * FOX_WIDTH + FOX_HEADS] + bf_ref[...])
    lf = -(jnp.maximum(t, 0.0) + jnp.log1p(jnp.exp(-jnp.abs(t))))
    lf_ref[...] = lf.reshape(nb, tl, FOX_HEADS)


def _kv_proj(x, W, *, nb, tl):
    B, L, _ = x.shape
    blk = lambda *tail: pl.BlockSpec((nb, tl) + tail, lambda b, j: (b, j) + (0,) * len(tail))
    heads = jax.ShapeDtypeStruct((B, L, FOX_HEADS, FOX_HEAD_DIM), F32)
    flat = jax.ShapeDtypeStruct((B, L, FOX_WIDTH), BF16)
    return pl.pallas_call(
        _kv_proj_kernel,
        grid=(B // nb, L // tl),
        in_specs=[
            blk(D_MODEL),
            _const_spec((1, D_MODEL)),
            _const_spec((D_MODEL, KV_COLS)),
            _const_spec((1, LANES)),
            _const_spec((1, FOX_HEADS)),
        ],
        out_specs=[blk(FOX_HEADS, FOX_HEAD_DIM), blk(FOX_HEADS, FOX_HEAD_DIM), blk(FOX_HEADS),
                   blk(FOX_WIDTH), blk(FOX_WIDTH)],
        out_shape=[heads, heads, jax.ShapeDtypeStruct((B, L, FOX_HEADS), F32), flat, flat],
        compiler_params=_params(2),
        name="kv_proj",
    )(x, W["g_kv"], W["w_kv"], W["kn_pair"], W["b_f"])


def _cumsum_kernel(lf_ref, f_ref):
    rows, n = lf_ref.shape
    r = lax.broadcasted_iota(jnp.int32, (LANES, LANES), 0)
    c = lax.broadcasted_iota(jnp.int32, (LANES, LANES), 1)
    tri = jnp.where(r <= c, 1.0, 0.0).astype(BF16)
    carry = jnp.zeros((rows, 1), F32)
    for ch in range(n // LANES):
        x = lf_ref[:, ch * LANES:(ch + 1) * LANES]
        hi = x.astype(BF16)
        r1 = x - hi.astype(F32)
        mid = r1.astype(BF16)
        low = (r1 - mid.astype(F32)).astype(BF16)
        y = (_dot(hi, tri) + _dot(mid, tri)) + _dot(low, tri) + carry
        f_ref[:, ch * LANES:(ch + 1) * LANES] = y
        carry = y[:, LANES - 1:LANES]


def _cumsum_lanes(lf_t):
    return pl.pallas_call(
        _cumsum_kernel,
        out_shape=jax.ShapeDtypeStruct(lf_t.shape, F32),
        name="logf_cumsum",
    )(lf_t)


def _masked_pair(x):
    lo = lax.broadcasted_iota(jnp.int32, (1, LANES), 1) < FOX_HEAD_DIM
    return jnp.where(lo, x, 0.0).astype(BF16), jnp.where(lo, 0.0, x).astype(BF16), lo


def _mixer_fox_prompt_kernel(x_ref, g_ref, w_in_ref, qn_ref, mk_ref, mv_ref, qnf_ref,
                             k_ref, v_ref, ft_ref, fq_ref, cat_ref):
    _, tl, _ = x_ref.shape
    j = pl.program_id(1)
    z = _in_proj(x_ref, g_ref, w_in_ref)
    qs = _rms_head64(z[:, :FOX_WIDTH], qnf_ref[0])
    causal = (lax.broadcasted_iota(jnp.int32, (tl, 1), 0)
              >= lax.broadcasted_iota(jnp.int32, (1, tl), 1))
    neg = jnp.full((tl, 1), -jnp.inf, F32)
    zero = jnp.zeros((tl, 1), F32)

    for c in range(FOX_HEADS // 2):
        sl = slice(c * LANES, (c + 1) * LANES)
        q_a, q_b, lo = _masked_pair(qs[c] * FOX_SCALE)
        fq_a = fq_ref[0, :, 2 * c:2 * c + 1]
        fq_b = fq_ref[0, :, 2 * c + 1:2 * c + 2]

        def step(kb, carry, masked):
            m_a, l_a, m_b, l_b, acc = carry
            kp = k_ref[0, kb, :, sl]
            vp = v_ref[0, kb, :, sl]
            new = []
            pv = []
            for q, fq, h, m, l in ((q_a, fq_a, 2 * c, m_a, l_a), (q_b, fq_b, 2 * c + 1, m_b, l_b)):
                s = (_dot_nt(q, kp) + fq) - ft_ref[0, kb, h:h + 1, :]
                if masked:
                    s = jnp.where(causal, s, -jnp.inf)
                m_new = jnp.maximum(m, jnp.max(s, axis=-1, keepdims=True))
                p = jnp.exp(s - m_new)
                alpha = jnp.exp(m - m_new)
                new.append((m_new, alpha * l + jnp.sum(p, axis=-1, keepdims=True), alpha))
                pv.append(_dot(p.astype(BF16), vp))
            (m_a, l_a, al_a), (m_b, l_b, al_b) = new
            acc = jnp.where(lo, al_a, al_b) * acc + jnp.where(lo, pv[0], pv[1])
            return m_a, l_a, m_b, l_b, acc

        init = (neg, zero, neg, zero, jnp.zeros((tl, LANES), F32))
        carry = lax.fori_loop(0, j, functools.partial(step, masked=False), init)
        _, l_a, _, l_b, acc = step(j, carry, masked=True)
        cat_ref[0, :, sl] = (acc * (1.0 / jnp.where(lo, l_a, l_b))).astype(cat_ref.dtype)

    _mem_attend(z[:, FOX_WIDTH:], qn_ref[0], mk_ref, mv_ref, 0, cat_ref, FOX_WIDTH)


def _mixer_fox_prompt(layer, x, k_b, v_b, f_t, f_q, mem_k, mem_v, W, *, tl):
    B, L, _ = x.shape
    nkb = L // tl
    k4 = k_b.reshape(B, nkb, tl, FOX_WIDTH)
    v4 = v_b.reshape(B, nkb, tl, FOX_WIDTH)
    ft4 = jnp.swapaxes(f_t.reshape(B, FOX_HEADS, nkb, tl), 1, 2)
    kv_spec = pl.BlockSpec((1, nkb, tl, FOX_WIDTH), lambda b, j: (b, 0, 0, 0))
    return pl.pallas_call(
        _mixer_fox_prompt_kernel,
        grid=(B, nkb),
        in_specs=[
            pl.BlockSpec((1, tl, D_MODEL), lambda b, j: (b, j, 0)),
            _layer_spec(layer, (1, D_MODEL)),
            _layer_spec(layer, (D_MODEL, MIX_WIDTH)),
            _layer_spec(layer, (1, MEM_HEAD_DIM)),
            _mem_spec(layer, 1),
            _mem_spec(layer, 1),
            _layer_spec(layer - N_A, (1, LANES)),
            kv_spec,
            kv_spec,
            pl.BlockSpec((1, nkb, FOX_HEADS, tl), lambda b, j: (b, 0, 0, 0)),
            pl.BlockSpec((1, tl, FOX_HEADS), lambda b, j: (b, j, 0)),
        ],
        out_specs=pl.BlockSpec((1, tl, MIX_WIDTH), lambda b, j: (b, j, 0)),
        out_shape=jax.ShapeDtypeStruct((B, L, MIX_WIDTH), BF16),
        compiler_params=_params(2),
        name="mixer_fox_prompt",
    )(x, W["g_mix"], W["w_in"], W["q_norm_mem"], mem_k, mem_v, W["qnf_pair"], k4, v4, ft4, f_q)


def _mixer_fox_sample_kernel(x_ref, g_ref, w_in_ref, qn_ref, mk_ref, mv_ref, qnf_ref,
                             kp_ref, vp_ref, kn_ref, vn_ref, ft_ref, fq_ref, cat_ref):
    nb, tl, _ = x_ref.shape
    past = kp_ref.shape[1]
    z = _in_proj(x_ref, g_ref, w_in_ref)
    causal = (lax.broadcasted_iota(jnp.int32, (tl, 1), 0)
              >= lax.broadcasted_iota(jnp.int32, (1, tl), 1))
    for i in range(nb):
        zi = z[i * tl:(i + 1) * tl]
        qs = _rms_head64(zi[:, :FOX_WIDTH], qnf_ref[0])
        for h in range(FOX_HEADS):
            half = h % 2
            q = (qs[h // 2][:, half * FOX_HEAD_DIM:(half + 1) * FOX_HEAD_DIM] * FOX_SCALE).astype(BF16)
            fq = fq_ref[i, :, h:h + 1]
            s_p = (_dot_nt(q, kp_ref[i, :, h, :].astype(BF16)) + fq) - ft_ref[i, h:h + 1, 0:past]
            s_n = (_dot_nt(q, kn_ref[i, :, h, :].astype(BF16)) + fq) - ft_ref[i, h:h + 1, past:past + tl]
            s_n = jnp.where(causal, s_n, -jnp.inf)
            m = jnp.maximum(jnp.max(s_p, axis=-1, keepdims=True), jnp.max(s_n, axis=-1, keepdims=True))
            p_p = jnp.exp(s_p - m)
            p_n = jnp.exp(s_n - m)
            l = jnp.sum(p_p, axis=-1, keepdims=True) + jnp.sum(p_n, axis=-1, keepdims=True)
            o = (_dot(p_p.astype(BF16), vp_ref[i, :, h, :].astype(BF16))
                 + _dot(p_n.astype(BF16), vn_ref[i, :, h, :].astype(BF16)))
            cat_ref[i, :, h * FOX_HEAD_DIM:(h + 1) * FOX_HEAD_DIM] = (o * (1.0 / l)).astype(cat_ref.dtype)
        _mem_attend(zi[:, FOX_WIDTH:], qn_ref[0], mk_ref, mv_ref, i, cat_ref, FOX_WIDTH)


def _mixer_fox_sample(layer, x, k_past, v_past, k_new, v_new, f_t, f_q, mem_k, mem_v, W, *, nb):
    B, L, _ = x.shape
    past = k_past.shape[1]
    lk_pad = f_t.shape[-1]
    per_b = lambda *tail: pl.BlockSpec((nb,) + tail, lambda b: (b,) + (0,) * len(tail))
    return pl.pallas_call(
        _mixer_fox_sample_kernel,
        grid=(B // nb,),
        in_specs=[
            per_b(L, D_MODEL),
            _layer_spec(layer, (1, D_MODEL)),
            _layer_spec(layer, (D_MODEL, MIX_WIDTH)),
            _layer_spec(layer, (1, MEM_HEAD_DIM)),
            _mem_spec(layer, nb),
            _mem_spec(layer, nb),
            _layer_spec(layer - N_A, (1, LANES)),
            per_b(past, FOX_HEADS, FOX_HEAD_DIM),
            per_b(past, FOX_HEADS, FOX_HEAD_DIM),
            per_b(L, FOX_HEADS, FOX_HEAD_DIM),
            per_b(L, FOX_HEADS, FOX_HEAD_DIM),
            per_b(FOX_HEADS, lk_pad),
            per_b(L, FOX_HEADS),
        ],
        out_specs=per_b(L, MIX_WIDTH),
        out_shape=jax.ShapeDtypeStruct((B, L, MIX_WIDTH), BF16),
        compiler_params=_params(1),
        name="mixer_fox_sample",
    )(x, W["g_mix"], W["w_in"], W["q_norm_mem"], mem_k, mem_v, W["qnf_pair"],
      k_past, v_past, k_new, v_new, f_t, f_q)


def _out_ffn_kernel(x_ref, cat_ref, w_out_ref, g_ref, w_gu_ref, w_down_ref, y_ref):
    x1 = x_ref[...] + _dot(cat_ref[...], w_out_ref[0])
    xn = _rms(x1, g_ref[0]).astype(BF16)
    cw = D_FF // FFN_CHUNKS
    acc = x1
    for c in range(FFN_CHUNKS):
        gate = _dot(xn, w_gu_ref[0, :, c * cw:(c + 1) * cw])
        up = _dot(xn, w_gu_ref[0, :, D_FF + c * cw:D_FF + (c + 1) * cw])
        h = (gate * (1.0 / (1.0 + jnp.exp(-gate)))) * up
        acc = acc + _dot(h.astype(BF16), w_down_ref[0, c * cw:(c + 1) * cw, :])
    y_ref[...] = acc


def _out_ffn(layer, x2, cat2, W):
    T = x2.shape[0]
    return pl.pallas_call(
        _out_ffn_kernel,
        grid=(T // ROW_BLOCK,),
        in_specs=[
            pl.BlockSpec((ROW_BLOCK, D_MODEL), lambda r: (r, 0)),
            pl.BlockSpec((ROW_BLOCK, MIX_WIDTH), lambda r: (r, 0)),
            _layer_spec(layer, (MIX_WIDTH, D_MODEL)),
            _layer_spec(layer, (1, D_MODEL)),
            _layer_spec(layer, (D_MODEL, 2 * D_FF)),
            _layer_spec(layer, (D_FF, D_MODEL)),
        ],
        out_specs=pl.BlockSpec((ROW_BLOCK, D_MODEL), lambda r: (r, 0)),
        out_shape=jax.ShapeDtypeStruct((T, D_MODEL), F32),
        compiler_params=_params(1),
        name="out_ffn",
    )(x2, cat2, W["w_out"], W["g_ffn"], W["w_gu"], W["w_down"])


def _trunk(x, pos0, pool_hist, fox_past, mem_k, mem_v, W, *, nb, tl):
    B, L, _ = x.shape
    pool_states = []
    for i in range(DEPTH):
        if i < N_A:
            cat, state = _mixer_pool(i, x, pool_hist, mem_k, mem_v, W, pos0=pos0, nb=nb, tl=tl)
            pool_states.append(state[:, :, 1:, :])
        else:
            if i == N_A:
                k_new, v_new, lf_new, k_b, v_b = _kv_proj(x, W, nb=nb, tl=tl)
                if fox_past is None:
                    lf_all = lf_new
                else:
                    lf_all = jnp.concatenate([fox_past[2], lf_new], axis=1)
                lk = lf_all.shape[1]
                lk_pad = -(-lk // LANES) * LANES
                lf_t = jnp.swapaxes(lf_all, 1, 2).reshape(B * FOX_HEADS, lk)
                lf_t = jnp.pad(lf_t, ((0, 0), (0, lk_pad - lk)))
                f_t = _cumsum_lanes(lf_t).reshape(B, FOX_HEADS, lk_pad)
                f_q = jnp.swapaxes(f_t[:, :, lk - L:lk], 1, 2)
            if fox_past is None:
                cat = _mixer_fox_prompt(i, x, k_b, v_b, f_t, f_q, mem_k, mem_v, W, tl=tl)
            else:
                cat = _mixer_fox_sample(i, x, fox_past[0], fox_past[1], k_new, v_new, f_t, f_q,
                                        mem_k, mem_v, W, nb=2)
        x = _out_ffn(i, x.reshape(B * L, D_MODEL), cat.reshape(B * L, MIX_WIDTH), W
                     ).reshape(B, L, D_MODEL)
    return x, jnp.concatenate(pool_states, axis=0), (k_new, v_new, lf_new)


def kernel(x_prompt, x_sample, state_pool, cache_fox_k, cache_fox_v, cache_fox_logf, cache_mem_k,
           cache_mem_v, mem_prompt, g_mix, w_in, w_out, q_norm_mem, g_mem, w_mem_kv, k_norm_mem,
           w_pool, pool_scale, q_norm_fox, g_kv, w_kv, k_norm_fox, b_f, g_ffn, w_gu, w_down):
    B, L, _ = x_prompt.shape
    SB, SL, _ = x_sample.shape
    W = dict(
        g_mix=g_mix.reshape(DEPTH, 1, D_MODEL), w_in=w_in.astype(BF16), w_out=w_out.astype(BF16),
        q_norm_mem=q_norm_mem.reshape(DEPTH, 1, MEM_HEAD_DIM), w_pool=w_pool.astype(BF16),
        pool_scale=pool_scale.reshape(N_A, 1, POOL_WIDTH),
        qnf_pair=jnp.tile(q_norm_fox, (1, 2)).reshape(DEPTH - N_A, 1, LANES),
        g_kv=g_kv.reshape(1, D_MODEL),
        w_kv=jnp.pad(w_kv, ((0, 0), (0, KV_COLS - w_kv.shape[1]))).astype(BF16),
        kn_pair=jnp.tile(k_norm_fox, 2).reshape(1, LANES), b_f=b_f.reshape(1, FOX_HEADS),
        g_ffn=g_ffn.reshape(DEPTH, 1, D_MODEL), w_gu=w_gu.astype(BF16), w_down=w_down.astype(BF16))

    mem_k_p, mem_v_p = _mem_kv(mem_prompt, g_mem.reshape(DEPTH, 1, D_MODEL), w_mem_kv.astype(BF16),
                               k_norm_mem.reshape(DEPTH, 1, MEM_HEAD_DIM))
    hist_p = jnp.zeros((N_A, B, HIST_ROWS, POOL_WIDTH), F32)
    y_p, pool_p, fox_p = _trunk(x_prompt, 0, hist_p, None, mem_k_p, mem_v_p, W, nb=1, tl=ROW_BLOCK)

    hist_s = jnp.pad(state_pool, ((0, 0), (0, 0), (1, 0), (0, 0)))
    past = (cache_fox_k, cache_fox_v, cache_fox_logf)
    y_s, pool_s, fox_s = _trunk(x_sample, PAST_LEN, hist_s, past, cache_mem_k, cache_mem_v, W,
                                nb=ROW_BLOCK // SL, tl=SL)

    return (y_p, y_s, pool_p, fox_p[0], fox_p[1], fox_p[2], mem_k_p, mem_v_p,
            pool_s, fox_s[0], fox_s[1], fox_s[2])
```

```python
import functools

import jax
import jax.numpy as jnp
from jax import lax
from jax.experimental import pallas as pl
from jax.experimental.pallas import tpu as pltpu

F32 = jnp.float32
BF16 = jnp.bfloat16

D_MODEL = 1024
DEPTH = 4
N_A = DEPTH // 2
PAST_LEN = 1024
POOL_WINDOWS = (2, 4, 8, 16)
POOL_GROUPS = len(POOL_WINDOWS)
POOL_WIDTH = D_MODEL // 2
POOL_GROUP_DIM = POOL_WIDTH // POOL_GROUPS
POOL_HIST = max(POOL_WINDOWS) - 1
HIST_ROWS = POOL_HIST + 1
FOX_HEAD_DIM = 64
FOX_WIDTH = D_MODEL // 2
FOX_HEADS = FOX_WIDTH // FOX_HEAD_DIM
MEM_TOKENS = 256
MEM_HEADS = 4
MEM_WIDTH = D_MODEL // 2
MEM_HEAD_DIM = MEM_WIDTH // MEM_HEADS
MIX_WIDTH = POOL_WIDTH + MEM_WIDTH
D_FF = ((8 * D_MODEL // 3 + 255) // 256) * 256
EPS = 1e-6
FOX_SCALE = FOX_HEAD_DIM ** -0.5
MEM_SCALE = MEM_HEAD_DIM ** -0.5

LANES = 128
ROW_BLOCK = 512
KV_ROWS = 2 * FOX_WIDTH + 16
FFN_CHUNKS = 2
VMEM_LIMIT = 56 * 1024 * 1024


def _dot(a, b):
    return jnp.dot(a, b, preferred_element_type=F32)


def _dot_nt(a, b):
    return lax.dot_general(a, b, (((1,), (1,)), ((), ())), preferred_element_type=F32)


def _rms(x, g):
    ms = jnp.mean(x * x, axis=-1, keepdims=True)
    return (x * lax.rsqrt(ms + EPS)) * g


def _rms_head64(x, g_pair):
    lo = lax.broadcasted_iota(jnp.int32, (1, LANES), 1) < FOX_HEAD_DIM
    outs = []
    for c in range(x.shape[-1] // LANES):
        xc = x[:, c * LANES:(c + 1) * LANES]
        sq = xc * xc
        s_lo = jnp.sum(jnp.where(lo, sq, 0.0), axis=-1, keepdims=True)
        s_hi = jnp.sum(jnp.where(lo, 0.0, sq), axis=-1, keepdims=True)
        ms = jnp.where(lo, s_lo, s_hi) * (1.0 / FOX_HEAD_DIM)
        outs.append((xc * lax.rsqrt(ms + EPS)) * g_pair)
    return outs


def _const_spec(shape):
    return pl.BlockSpec(shape, lambda *_: (0,) * len(shape), pipeline_mode=pl.Buffered(1))


def _layer_spec(layer, shape):
    return pl.BlockSpec((1,) + shape, lambda *_: (layer,) + (0,) * len(shape),
                        pipeline_mode=pl.Buffered(1))


def _mem_spec(layer, nb):
    return pl.BlockSpec((1, nb, MEM_TOKENS, MEM_HEADS, MEM_HEAD_DIM),
                        lambda b, *_: (layer, b, 0, 0, 0))


def _params(n_grid):
    return pltpu.CompilerParams(
        dimension_semantics=("arbitrary",) * n_grid, vmem_limit_bytes=VMEM_LIMIT)


def _mem_kv_kernel(mem_ref, g_ref, w_ref, kn_ref, k_ref, v_ref):
    nb = mem_ref.shape[0]
    x = mem_ref[...].reshape(nb * MEM_TOKENS, D_MODEL)
    kv = _dot(_rms(x, g_ref[0]).astype(BF16), w_ref[0])
    for h in range(MEM_HEADS):
        ks = slice(h * MEM_HEAD_DIM, (h + 1) * MEM_HEAD_DIM)
        vs = slice(MEM_WIDTH + h * MEM_HEAD_DIM, MEM_WIDTH + (h + 1) * MEM_HEAD_DIM)
        k_ref[0, :, :, h, :] = _rms(kv[:, ks], kn_ref[0]).reshape(nb, MEM_TOKENS, MEM_HEAD_DIM)
        v_ref[0, :, :, h, :] = kv[:, vs].reshape(nb, MEM_TOKENS, MEM_HEAD_DIM)


def _mem_kv(mem, g_mem, w_mem_kv, k_norm_mem):
    B = mem.shape[0]
    nb = 4
    out = jax.ShapeDtypeStruct((DEPTH, B, MEM_TOKENS, MEM_HEADS, MEM_HEAD_DIM), F32)
    out_spec = pl.BlockSpec((1, nb, MEM_TOKENS, MEM_HEADS, MEM_HEAD_DIM),
                            lambda i, b: (i, b, 0, 0, 0))
    return pl.pallas_call(
        _mem_kv_kernel,
        grid=(DEPTH, B // nb),
        in_specs=[
            pl.BlockSpec((nb, MEM_TOKENS, D_MODEL), lambda i, b: (b, 0, 0)),
            pl.BlockSpec((1, 1, D_MODEL), lambda i, b: (i, 0, 0)),
            pl.BlockSpec((1, D_MODEL, 2 * MEM_WIDTH), lambda i, b: (i, 0, 0)),
            pl.BlockSpec((1, 1, MEM_HEAD_DIM), lambda i, b: (i, 0, 0)),
        ],
        out_specs=[out_spec, out_spec],
        out_shape=[out, out],
        compiler_params=_params(2),
        name="mem_kv",
    )(mem, g_mem, w_mem_kv, k_norm_mem)


def _in_proj(x_ref, g_ref, w_ref):
    nb, tl, _ = x_ref.shape
    x = x_ref[...].reshape(nb * tl, D_MODEL)
    return _dot(_rms(x, g_ref[0]).astype(BF16), w_ref[0])


def _mem_attend(zq, qn, mk_ref, mv_ref, i, cat_ref, col0):
    for h in range(MEM_HEADS):
        sl = slice(h * MEM_HEAD_DIM, (h + 1) * MEM_HEAD_DIM)
        q = _rms(zq[:, sl], qn).astype(BF16)
        s = _dot_nt(q, mk_ref[0, i, :, h, :].astype(BF16)) * MEM_SCALE
        p = jnp.exp(s - jnp.max(s, axis=-1, keepdims=True))
        p = p * (1.0 / jnp.sum(p, axis=-1, keepdims=True))
        o = _dot(p.astype(BF16), mv_ref[0, i, :, h, :].astype(BF16))
        cat_ref[i, :, col0 + h * MEM_HEAD_DIM:col0 + (h + 1) * MEM_HEAD_DIM] = o.astype(cat_ref.dtype)


def _mixer_pool_kernel(x_ref, g_ref, w_in_ref, qn_ref, mk_ref, mv_ref, hist_ref, wp_ref, ps_ref,
                       cat_ref, state_ref, ubuf, *, pos0):
    nb, tl, _ = x_ref.shape
    j = pl.program_id(1)
    z = _in_proj(x_ref, g_ref, w_in_ref)

    @pl.when(j == 0)
    def _():
        ubuf[:, 0:HIST_ROWS, :] = hist_ref[0]

    pos = pos0 + j * tl + lax.broadcasted_iota(jnp.int32, (tl, 1), 0)
    for i in range(nb):
        zi = z[i * tl:(i + 1) * tl]
        u = zi[:, :POOL_WIDTH]
        ubuf[i, HIST_ROWS:HIST_ROWS + tl, :] = u
        for g, w in enumerate(POOL_WINDOWS):
            sl = slice(g * POOL_GROUP_DIM, (g + 1) * POOL_GROUP_DIM)
            ug = u[:, sl]
            acc = ug
            for k in range(1, w):
                acc = acc + ubuf[i, HIST_ROWS - k:HIST_ROWS - k + tl, sl]
            cnt = jnp.minimum(pos + 1, w).astype(F32)
            d = acc / cnt - ug
            y = _dot(d.astype(BF16), wp_ref[0, g]) * ps_ref[0, :, sl]
            cat_ref[i, :, sl] = y.astype(cat_ref.dtype)
        _mem_attend(zi[:, POOL_WIDTH:], qn_ref[0], mk_ref, mv_ref, i, cat_ref, POOL_WIDTH)
        tail = ubuf[i, tl:tl + HIST_ROWS, :]
        state_ref[0, i] = tail
        ubuf[i, 0:HIST_ROWS, :] = tail


def _mixer_pool(layer, x, hist, mem_k, mem_v, W, *, pos0, nb, tl):
    B, L, _ = x.shape
    assert tl >= HIST_ROWS and L % tl == 0 and B % nb == 0
    hist_spec = pl.BlockSpec((1, nb, HIST_ROWS, POOL_WIDTH), lambda b, j: (layer, b, 0, 0))
    return pl.pallas_call(
        functools.partial(_mixer_pool_kernel, pos0=pos0),
        grid=(B // nb, L // tl),
        in_specs=[
            pl.BlockSpec((nb, tl, D_MODEL), lambda b, j: (b, j, 0)),
            _layer_spec(layer, (1, D_MODEL)),
            _layer_spec(layer, (D_MODEL, MIX_WIDTH)),
            _layer_spec(layer, (1, MEM_HEAD_DIM)),
            _mem_spec(layer, nb),
            _mem_spec(layer, nb),
            hist_spec,
            _layer_spec(layer, (POOL_GROUPS, POOL_GROUP_DIM, POOL_GROUP_DIM)),
            _layer_spec(layer, (1, POOL_WIDTH)),
        ],
        out_specs=[
            pl.BlockSpec((nb, tl, MIX_WIDTH), lambda b, j: (b, j, 0)),
            pl.BlockSpec((1, nb, HIST_ROWS, POOL_WIDTH), lambda b, j: (0, b, 0, 0)),
        ],
        out_shape=[
            jax.ShapeDtypeStruct((B, L, MIX_WIDTH), BF16),
            jax.ShapeDtypeStruct((1, B, HIST_ROWS, POOL_WIDTH), F32),
        ],
        scratch_shapes=[pltpu.VMEM((nb, HIST_ROWS + tl, POOL_WIDTH), F32)],
        compiler_params=_params(2),
        name="mixer_pool",
    )(x, W["g_mix"], W["w_in"], W["q_norm_mem"], mem_k, mem_v, hist, W["w_pool"], W["pool_scale"])


def _kv_proj_kernel(x_ref, g_ref, w_ref, kn_ref, bf_ref, k_ref, v_ref, lf_ref, kb_ref, vb_ref):
    nb, tl, _ = x_ref.shape
    rows = nb * tl
    x = x_ref[...].reshape(rows, D_MODEL)
    zt = _dot_nt(w_ref[...], _rms(x, g_ref[...]).astype(BF16))
    k3 = zt[:FOX_WIDTH].reshape(FOX_HEADS, FOX_HEAD_DIM, rows)
    ms = jnp.mean(k3 * k3, axis=1, keepdims=True)
    k3 = (k3 * lax.rsqrt(ms + EPS)) * kn_ref[...]
    v3 = zt[FOX_WIDTH:2 * FOX_WIDTH].reshape(FOX_HEADS, FOX_HEAD_DIM, rows)
    t = -(zt[2 * FOX_WIDTH:2 * FOX_WIDTH + FOX_HEADS] + bf_ref[...])
    lf = -(jnp.maximum(t, 0.0) + jnp.log1p(jnp.exp(-jnp.abs(t))))
    ones = jnp.ones((FOX_HEADS, FOX_HEAD_DIM, tl), BF16)
    for i in range(nb):
        cols = slice(i * tl, (i + 1) * tl)
        k_ref[i] = k3[:, :, cols]
        v_ref[i] = v3[:, :, cols]
        lf_ref[i] = lf[:, cols]
        kb_ref[i] = k3[:, :, cols].astype(BF16)
        vb_ref[i, :, 0:FOX_HEAD_DIM, :] = v3[:, :, cols].astype(BF16)
        vb_ref[i, :, FOX_HEAD_DIM:2 * FOX_HEAD_DIM, :] = ones


def _kv_proj(x, W, *, nb, tl):
    B, L, _ = x.shape
    hd = lambda rows: pl.BlockSpec((nb, FOX_HEADS, rows, tl), lambda b, j: (b, 0, 0, j))
    heads = jax.ShapeDtypeStruct((B, FOX_HEADS, FOX_HEAD_DIM, L), F32)
    return pl.pallas_call(
        _kv_proj_kernel,
        grid=(B // nb, L // tl),
        in_specs=[
            pl.BlockSpec((nb, tl, D_MODEL), lambda b, j: (b, j, 0)),
            _const_spec((1, D_MODEL)),
            _const_spec((KV_ROWS, D_MODEL)),
            _const_spec((1, FOX_HEAD_DIM, 1)),
            _const_spec((FOX_HEADS, 1)),
        ],
        out_specs=[hd(FOX_HEAD_DIM), hd(FOX_HEAD_DIM),
                   pl.BlockSpec((nb, FOX_HEADS, tl), lambda b, j: (b, 0, j)),
                   hd(FOX_HEAD_DIM), hd(2 * FOX_HEAD_DIM)],
        out_shape=[heads, heads, jax.ShapeDtypeStruct((B, FOX_HEADS, L), F32),
                   jax.ShapeDtypeStruct((B, FOX_HEADS, FOX_HEAD_DIM, L), BF16),
                   jax.ShapeDtypeStruct((B, FOX_HEADS, 2 * FOX_HEAD_DIM, L), BF16)],
        compiler_params=_params(2),
        name="kv_proj",
    )(x, W["g_kv"], W["w_kv_t"], W["kn_col"], W["b_f"])


def _cumsum_kernel(lf_ref, f_ref):
    rows, n = lf_ref.shape
    r = lax.broadcasted_iota(jnp.int32, (LANES, LANES), 0)
    c = lax.broadcasted_iota(jnp.int32, (LANES, LANES), 1)
    tri = jnp.where(r <= c, 1.0, 0.0).astype(BF16)
    carry = jnp.zeros((rows, 1), F32)
    for ch in range(n // LANES):
        x = lf_ref[:, ch * LANES:(ch + 1) * LANES]
        hi = x.astype(BF16)
        r1 = x - hi.astype(F32)
        mid = r1.astype(BF16)
        low = (r1 - mid.astype(F32)).astype(BF16)
        y = (_dot(hi, tri) + _dot(mid, tri)) + _dot(low, tri) + carry
        f_ref[:, ch * LANES:(ch + 1) * LANES] = y
        carry = y[:, LANES - 1:LANES]


def _cumsum_lanes(lf_t):
    return pl.pallas_call(
        _cumsum_kernel,
        out_shape=jax.ShapeDtypeStruct(lf_t.shape, F32),
        name="logf_cumsum",
    )(lf_t)


def _split3(x):
    hi = x.astype(BF16).astype(F32)
    r1 = x - hi
    mid = r1.astype(BF16).astype(F32)
    low = (r1 - mid).astype(BF16).astype(F32)
    return hi, mid, low


def _mixer_fox_prompt_kernel(x_ref, g_ref, w_in_ref, qn_ref, mk_ref, mv_ref, qnf_ref,
                             kt_ref, vt_ref, ft_ref, fq_ref, cat_ref, q_sc, m_sc, acc_sc):
    _, tl, _ = x_ref.shape
    tk = kt_ref.shape[-1]
    j = pl.program_id(1)
    kb = pl.program_id(2)
    lane = lax.broadcasted_iota(jnp.int32, (1, LANES), 1)
    row = lax.broadcasted_iota(jnp.int32, (FOX_HEAD_DIM, 1), 0)

    @pl.when(kb == 0)
    def _():
        z = _in_proj(x_ref, g_ref, w_in_ref)
        _mem_attend(z[:, FOX_WIDTH:], qn_ref[0], mk_ref, mv_ref, 0, cat_ref, FOX_WIDTH)
        qs = _rms_head64(z[:, :FOX_WIDTH], qnf_ref[0])
        for h in range(FOX_HEADS):
            base = qs[h // 2] if h % 2 == 0 else pltpu.roll(qs[h // 2], FOX_HEAD_DIM, 1)
            hi, mid, low = _split3(fq_ref[0, :, h:h + 1])
            tail = jnp.where(lane < FOX_HEAD_DIM + 3, 1.0,
                             jnp.where(lane == FOX_HEAD_DIM + 3, hi,
                                       jnp.where(lane == FOX_HEAD_DIM + 4, mid,
                                                 jnp.where(lane == FOX_HEAD_DIM + 5, low, 0.0))))
            q_sc[h] = jnp.where(lane < FOX_HEAD_DIM, base * FOX_SCALE, tail).astype(BF16)
        m_sc[...] = jnp.full(m_sc.shape, -jnp.inf, F32)
        acc_sc[...] = jnp.zeros(acc_sc.shape, F32)

    def attend(masked):
        causal = (lax.broadcasted_iota(jnp.int32, (tl, 1), 0)
                  >= lax.broadcasted_iota(jnp.int32, (1, tk), 1))
        for h in range(FOX_HEADS):
            hi, mid, low = _split3(-ft_ref[0, 0, h:h + 1, :])
            aug = jnp.where(row == 0, hi, jnp.where(row == 1, mid, jnp.where(row == 2, low,
                            jnp.where(row < 6, 1.0, 0.0)))).astype(BF16)
            k_aug = jnp.concatenate([kt_ref[0, h], aug], axis=0)
            s = _dot(q_sc[h], k_aug)
            if masked:
                s = jnp.where(causal, s, -jnp.inf)
            m_old = m_sc[:, h:h + 1]
            m_new = jnp.maximum(m_old, jnp.max(s, axis=-1, keepdims=True))
            p = jnp.exp(s - m_new).astype(BF16)
            acc_sc[h] = jnp.exp(m_old - m_new) * acc_sc[h] + _dot_nt(p, vt_ref[0, h])
            m_sc[:, h:h + 1] = m_new

    @pl.when(kb < j)
    def _():
        attend(False)

    @pl.when(kb == j)
    def _():
        attend(True)
        for c in range(FOX_HEADS // 2):
            a0 = acc_sc[2 * c]
            a1 = acc_sc[2 * c + 1]
            o0 = a0 * (1.0 / a0[:, FOX_HEAD_DIM:FOX_HEAD_DIM + 1])
            o1 = a1 * (1.0 / a1[:, FOX_HEAD_DIM:FOX_HEAD_DIM + 1])
            pair = jnp.where(lane < FOX_HEAD_DIM, o0, pltpu.roll(o1, FOX_HEAD_DIM, 1))
            cat_ref[0, :, c * LANES:(c + 1) * LANES] = pair.astype(cat_ref.dtype)


def _mixer_fox_prompt(layer, x, kt_b, vt_b, f_t, f_q, mem_k, mem_v, W, *, tl):
    B, L, _ = x.shape
    nkb = L // tl
    ft4 = jnp.swapaxes(f_t.reshape(B, FOX_HEADS, nkb, tl), 1, 2)
    kv_spec = lambda rows: pl.BlockSpec((1, FOX_HEADS, rows, tl),
                                        lambda b, j, kb: (b, 0, 0, jnp.minimum(kb, j)))
    return pl.pallas_call(
        _mixer_fox_prompt_kernel,
        grid=(B, nkb, nkb),
        in_specs=[
            pl.BlockSpec((1, tl, D_MODEL), lambda b, j, kb: (b, j, 0)),
            _layer_spec(layer, (1, D_MODEL)),
            _layer_spec(layer, (D_MODEL, MIX_WIDTH)),
            _layer_spec(layer, (1, MEM_HEAD_DIM)),
            _mem_spec(layer, 1),
            _mem_spec(layer, 1),
            _layer_spec(layer - N_A, (1, LANES)),
            kv_spec(FOX_HEAD_DIM),
            kv_spec(2 * FOX_HEAD_DIM),
            pl.BlockSpec((1, 1, FOX_HEADS, tl), lambda b, j, kb: (b, jnp.minimum(kb, j), 0, 0)),
            pl.BlockSpec((1, tl, FOX_HEADS), lambda b, j, kb: (b, j, 0)),
        ],
        out_specs=pl.BlockSpec((1, tl, MIX_WIDTH), lambda b, j, kb: (b, j, 0)),
        out_shape=jax.ShapeDtypeStruct((B, L, MIX_WIDTH), BF16),
        scratch_shapes=[
            pltpu.VMEM((FOX_HEADS, tl, LANES), BF16),
            pltpu.VMEM((tl, LANES), F32),
            pltpu.VMEM((FOX_HEADS, tl, LANES), F32),
        ],
        compiler_params=_params(3),
        name="mixer_fox_prompt",
    )(x, W["g_mix"], W["w_in"], W["q_norm_mem"], mem_k, mem_v, W["qnf_pair"], kt_b, vt_b, ft4, f_q)


def _mixer_fox_sample_kernel(x_ref, g_ref, w_in_ref, qn_ref, mk_ref, mv_ref, qnf_ref,
                             ktp_ref, vtp_ref, ktn_ref, vtn_ref, ft_ref, fq_ref, cat_ref):
    nb, tl, _ = x_ref.shape
    past = ktp_ref.shape[-1]
    z = _in_proj(x_ref, g_ref, w_in_ref)
    causal = (lax.broadcasted_iota(jnp.int32, (tl, 1), 0)
              >= lax.broadcasted_iota(jnp.int32, (1, tl), 1))
    for i in range(nb):
        zi = z[i * tl:(i + 1) * tl]
        qs = _rms_head64(zi[:, :FOX_WIDTH], qnf_ref[0])
        for h in range(FOX_HEADS):
            half = h % 2
            q = (qs[h // 2][:, half * FOX_HEAD_DIM:(half + 1) * FOX_HEAD_DIM] * FOX_SCALE).astype(BF16)
            fq = fq_ref[i, :, h:h + 1]
            s_p = (_dot(q, ktp_ref[i, h].astype(BF16)) + fq) - ft_ref[i, h:h + 1, 0:past]
            s_n = (_dot(q, ktn_ref[i, h].astype(BF16)) + fq) - ft_ref[i, h:h + 1, past:past + tl]
            s_n = jnp.where(causal, s_n, -jnp.inf)
            m = jnp.maximum(jnp.max(s_p, axis=-1, keepdims=True), jnp.max(s_n, axis=-1, keepdims=True))
            p_p = jnp.exp(s_p - m)
            p_n = jnp.exp(s_n - m)
            l = jnp.sum(p_p, axis=-1, keepdims=True) + jnp.sum(p_n, axis=-1, keepdims=True)
            o = (_dot_nt(p_p.astype(BF16), vtp_ref[i, h].astype(BF16))
                 + _dot_nt(p_n.astype(BF16), vtn_ref[i, h].astype(BF16)))
            cat_ref[i, :, h * FOX_HEAD_DIM:(h + 1) * FOX_HEAD_DIM] = (o * (1.0 / l)).astype(cat_ref.dtype)
        _mem_attend(zi[:, FOX_WIDTH:], qn_ref[0], mk_ref, mv_ref, i, cat_ref, FOX_WIDTH)


def _mixer_fox_sample(layer, x, kt_past, vt_past, kt_new, vt_new, f_t, f_q, mem_k, mem_v, W, *, nb):
    B, L, _ = x.shape
    past = kt_past.shape[-1]
    lk_pad = f_t.shape[-1]
    per_b = lambda *tail: pl.BlockSpec((nb,) + tail, lambda b: (b,) + (0,) * len(tail))
    return pl.pallas_call(
        _mixer_fox_sample_kernel,
        grid=(B // nb,),
        in_specs=[
            per_b(L, D_MODEL),
            _layer_spec(layer, (1, D_MODEL)),
            _layer_spec(layer, (D_MODEL, MIX_WIDTH)),
            _layer_spec(layer, (1, MEM_HEAD_DIM)),
            _mem_spec(layer, nb),
            _mem_spec(layer, nb),
            _layer_spec(layer - N_A, (1, LANES)),
            per_b(FOX_HEADS, FOX_HEAD_DIM, past),
            per_b(FOX_HEADS, FOX_HEAD_DIM, past),
            per_b(FOX_HEADS, FOX_HEAD_DIM, L),
            per_b(FOX_HEADS, FOX_HEAD_DIM, L),
            per_b(FOX_HEADS, lk_pad),
            per_b(L, FOX_HEADS),
        ],
        out_specs=per_b(L, MIX_WIDTH),
        out_shape=jax.ShapeDtypeStruct((B, L, MIX_WIDTH), BF16),
        compiler_params=_params(1),
        name="mixer_fox_sample",
    )(x, W["g_mix"], W["w_in"], W["q_norm_mem"], mem_k, mem_v, W["qnf_pair"],
      kt_past, vt_past, kt_new, vt_new, f_t, f_q)


def _out_ffn_kernel(x_ref, cat_ref, w_out_ref, g_ref, w_gu_ref, w_down_ref, y_ref):
    x1 = x_ref[...] + _dot(cat_ref[...], w_out_ref[0])
    xn = _rms(x1, g_ref[0]).astype(BF16)
    cw = D_FF // FFN_CHUNKS
    acc = x1
    for c in range(FFN_CHUNKS):
        gate = _dot(xn, w_gu_ref[0, :, c * cw:(c + 1) * cw])
        up = _dot(xn, w_gu_ref[0, :, D_FF + c * cw:D_FF + (c + 1) * cw])
        h = (gate * (1.0 / (1.0 + jnp.exp(-gate)))) * up
        acc = acc + _dot(h.astype(BF16), w_down_ref[0, c * cw:(c + 1) * cw, :])
    y_ref[...] = acc


def _out_ffn(layer, x2, cat2, W):
    T = x2.shape[0]
    return pl.pallas_call(
        _out_ffn_kernel,
        grid=(T // ROW_BLOCK,),
        in_specs=[
            pl.BlockSpec((ROW_BLOCK, D_MODEL), lambda r: (r, 0)),
            pl.BlockSpec((ROW_BLOCK, MIX_WIDTH), lambda r: (r, 0)),
            _layer_spec(layer, (MIX_WIDTH, D_MODEL)),
            _layer_spec(layer, (1, D_MODEL)),
            _layer_spec(layer, (D_MODEL, 2 * D_FF)),
            _layer_spec(layer, (D_FF, D_MODEL)),
        ],
        out_specs=pl.BlockSpec((ROW_BLOCK, D_MODEL), lambda r: (r, 0)),
        out_shape=jax.ShapeDtypeStruct((T, D_MODEL), F32),
        compiler_params=_params(1),
        name="out_ffn",
    )(x2, cat2, W["w_out"], W["g_ffn"], W["w_gu"], W["w_down"])


def _trunk(x, pos0, pool_hist, fox_past, mem_k, mem_v, W, *, nb, tl):
    B, L, _ = x.shape
    pool_states = []
    for i in range(DEPTH):
        if i < N_A:
            cat, state = _mixer_pool(i, x, pool_hist, mem_k, mem_v, W, pos0=pos0, nb=nb, tl=tl)
            pool_states.append(state[:, :, 1:, :])
        else:
            if i == N_A:
                kt_new, vt_new, lft_new, kt_b, vt_b = _kv_proj(x, W, nb=nb, tl=tl)
                if fox_past is None:
                    lft_all = lft_new
                else:
                    lft_all = jnp.concatenate([fox_past[2], lft_new], axis=2)
                lk = lft_all.shape[2]
                lk_pad = -(-lk // LANES) * LANES
                lf_t = jnp.pad(lft_all.reshape(B * FOX_HEADS, lk), ((0, 0), (0, lk_pad - lk)))
                f_t = _cumsum_lanes(lf_t).reshape(B, FOX_HEADS, lk_pad)
                f_q = jnp.swapaxes(f_t[:, :, lk - L:lk], 1, 2)
            if fox_past is None:
                cat = _mixer_fox_prompt(i, x, kt_b, vt_b, f_t, f_q, mem_k, mem_v, W, tl=tl)
            else:
                cat = _mixer_fox_sample(i, x, fox_past[0], fox_past[1], kt_new, vt_new, f_t, f_q,
                                        mem_k, mem_v, W, nb=2)
        x = _out_ffn(i, x.reshape(B * L, D_MODEL), cat.reshape(B * L, MIX_WIDTH), W
                     ).reshape(B, L, D_MODEL)
    fox_new = (jnp.transpose(kt_new, (0, 3, 1, 2)), jnp.transpose(vt_new, (0, 3, 1, 2)),
               jnp.swapaxes(lft_new, 1, 2))
    return x, jnp.concatenate(pool_states, axis=0), fox_new


def kernel(x_prompt, x_sample, state_pool, cache_fox_k, cache_fox_v, cache_fox_logf, cache_mem_k,
           cache_mem_v, mem_prompt, g_mix, w_in, w_out, q_norm_mem, g_mem, w_mem_kv, k_norm_mem,
           w_pool, pool_scale, q_norm_fox, g_kv, w_kv, k_norm_fox, b_f, g_ffn, w_gu, w_down):
    B, L, _ = x_prompt.shape
    SB, SL, _ = x_sample.shape
    W = dict(
        g_mix=g_mix.reshape(DEPTH, 1, D_MODEL), w_in=w_in.astype(BF16), w_out=w_out.astype(BF16),
        q_norm_mem=q_norm_mem.reshape(DEPTH, 1, MEM_HEAD_DIM), w_pool=w_pool.astype(BF16),
        pool_scale=pool_scale.reshape(N_A, 1, POOL_WIDTH),
        qnf_pair=jnp.tile(q_norm_fox, (1, 2)).reshape(DEPTH - N_A, 1, LANES),
        g_kv=g_kv.reshape(1, D_MODEL),
        w_kv_t=jnp.pad(w_kv.T, ((0, KV_ROWS - w_kv.shape[1]), (0, 0))).astype(BF16),
        kn_col=k_norm_fox.reshape(1, FOX_HEAD_DIM, 1), b_f=b_f.reshape(FOX_HEADS, 1),
        g_ffn=g_ffn.reshape(DEPTH, 1, D_MODEL), w_gu=w_gu.astype(BF16), w_down=w_down.astype(BF16))

    mem_k_p, mem_v_p = _mem_kv(mem_prompt, g_mem.reshape(DEPTH, 1, D_MODEL), w_mem_kv.astype(BF16),
                               k_norm_mem.reshape(DEPTH, 1, MEM_HEAD_DIM))
    hist_p = jnp.zeros((N_A, B, HIST_ROWS, POOL_WIDTH), F32)
    y_p, pool_p, fox_p = _trunk(x_prompt, 0, hist_p, None, mem_k_p, mem_v_p, W, nb=1, tl=ROW_BLOCK)

    hist_s = jnp.pad(state_pool, ((0, 0), (0, 0), (1, 0), (0, 0)))
    past = (jnp.transpose(cache_fox_k, (0, 2, 3, 1)), jnp.transpose(cache_fox_v, (0, 2, 3, 1)),
            jnp.swapaxes(cache_fox_logf, 1, 2))
    y_s, pool_s, fox_s = _trunk(x_sample, PAST_LEN, hist_s, past, cache_mem_k, cache_mem_v, W,
                                nb=ROW_BLOCK // SL, tl=SL)

    return (y_p, y_s, pool_p, fox_p[0], fox_p[1], fox_p[2], mem_k_p, mem_v_p,
            pool_s, fox_s[0], fox_s[1], fox_s[2])
```

```python
import functools

import jax
import jax.numpy as jnp
from jax import lax
from jax.experimental import pallas as pl
from jax.experimental.pallas import tpu as pltpu

F32 = jnp.float32
BF16 = jnp.bfloat16

D_MODEL = 1024
DEPTH = 4
N_A = DEPTH // 2
PAST_LEN = 1024
POOL_WINDOWS = (2, 4, 8, 16)
POOL_GROUPS = len(POOL_WINDOWS)
POOL_WIDTH = D_MODEL // 2
POOL_GROUP_DIM = POOL_WIDTH // POOL_GROUPS
POOL_HIST = max(POOL_WINDOWS) - 1
HIST_ROWS = POOL_HIST + 1
FOX_HEAD_DIM = 64
FOX_WIDTH = D_MODEL // 2
FOX_HEADS = FOX_WIDTH // FOX_HEAD_DIM
MEM_TOKENS = 256
MEM_HEADS = 4
MEM_WIDTH = D_MODEL // 2
MEM_HEAD_DIM = MEM_WIDTH // MEM_HEADS
MIX_WIDTH = POOL_WIDTH + MEM_WIDTH
D_FF = ((8 * D_MODEL // 3 + 255) // 256) * 256
EPS = 1e-6
FOX_SCALE = FOX_HEAD_DIM ** -0.5
MEM_SCALE = MEM_HEAD_DIM ** -0.5

LANES = 128
ROW_BLOCK = 512
KV_ROWS = 2 * FOX_WIDTH + 16
FFN_CHUNKS = 2
VMEM_LIMIT = 56 * 1024 * 1024


def _dot(a, b):
    return jnp.dot(a, b, preferred_element_type=F32)


def _dot_nt(a, b):
    return lax.dot_general(a, b, (((1,), (1,)), ((), ())), preferred_element_type=F32)


def _rms(x, g):
    ms = jnp.mean(x * x, axis=-1, keepdims=True)
    return (x * lax.rsqrt(ms + EPS)) * g


def _rms_head64(x, g_pair):
    lo = lax.broadcasted_iota(jnp.int32, (1, LANES), 1) < FOX_HEAD_DIM
    outs = []
    for c in range(x.shape[-1] // LANES):
        xc = x[:, c * LANES:(c + 1) * LANES]
        sq = xc * xc
        s_lo = jnp.sum(jnp.where(lo, sq, 0.0), axis=-1, keepdims=True)
        s_hi = jnp.sum(jnp.where(lo, 0.0, sq), axis=-1, keepdims=True)
        ms = jnp.where(lo, s_lo, s_hi) * (1.0 / FOX_HEAD_DIM)
        outs.append((xc * lax.rsqrt(ms + EPS)) * g_pair)
    return outs


def _const_spec(shape):
    return pl.BlockSpec(shape, lambda *_: (0,) * len(shape), pipeline_mode=pl.Buffered(1))


def _layer_spec(layer, shape):
    return pl.BlockSpec((1,) + shape, lambda *_: (layer,) + (0,) * len(shape),
                        pipeline_mode=pl.Buffered(1))


MEM_ROWS = MEM_TOKENS * MEM_HEADS


def _mem_spec(layer, nb):
    return pl.BlockSpec((1, nb, MEM_ROWS, MEM_HEAD_DIM), lambda b, *_: (layer, b, 0, 0))


def _head_rows(h):
    return pl.ds(h, MEM_TOKENS, stride=MEM_HEADS)


def _params(n_grid):
    return pltpu.CompilerParams(
        dimension_semantics=("arbitrary",) * n_grid, vmem_limit_bytes=VMEM_LIMIT)


def _mem_kv_kernel(mem_ref, g_ref, w_ref, kn_ref, k_ref, v_ref):
    nb = mem_ref.shape[0]
    x = mem_ref[...].reshape(nb * MEM_TOKENS, D_MODEL)
    kv = _dot(_rms(x, g_ref[0]).astype(BF16), w_ref[0])
    for h in range(MEM_HEADS):
        ks = slice(h * MEM_HEAD_DIM, (h + 1) * MEM_HEAD_DIM)
        vs = slice(MEM_WIDTH + h * MEM_HEAD_DIM, MEM_WIDTH + (h + 1) * MEM_HEAD_DIM)
        k_ref[0, :, _head_rows(h), :] = _rms(kv[:, ks], kn_ref[0]).reshape(nb, MEM_TOKENS, MEM_HEAD_DIM)
        v_ref[0, :, _head_rows(h), :] = kv[:, vs].reshape(nb, MEM_TOKENS, MEM_HEAD_DIM)


def _mem_kv(mem, g_mem, w_mem_kv, k_norm_mem):
    B = mem.shape[0]
    nb = 4
    out = jax.ShapeDtypeStruct((DEPTH, B, MEM_ROWS, MEM_HEAD_DIM), F32)
    out_spec = pl.BlockSpec((1, nb, MEM_ROWS, MEM_HEAD_DIM), lambda i, b: (i, b, 0, 0))
    return pl.pallas_call(
        _mem_kv_kernel,
        grid=(DEPTH, B // nb),
        in_specs=[
            pl.BlockSpec((nb, MEM_TOKENS, D_MODEL), lambda i, b: (b, 0, 0)),
            pl.BlockSpec((1, 1, D_MODEL), lambda i, b: (i, 0, 0)),
            pl.BlockSpec((1, D_MODEL, 2 * MEM_WIDTH), lambda i, b: (i, 0, 0)),
            pl.BlockSpec((1, 1, MEM_HEAD_DIM), lambda i, b: (i, 0, 0)),
        ],
        out_specs=[out_spec, out_spec],
        out_shape=[out, out],
        compiler_params=_params(2),
        name="mem_kv",
    )(mem, g_mem, w_mem_kv, k_norm_mem)


def _in_proj(x_ref, g_ref, w_ref):
    nb, tl, _ = x_ref.shape
    x = x_ref[...].reshape(nb * tl, D_MODEL)
    return _dot(_rms(x, g_ref[0]).astype(BF16), w_ref[0])


def _mem_attend(zq, qn, mk_ref, mv_ref, i, cat_ref, col0):
    for h in range(MEM_HEADS):
        sl = slice(h * MEM_HEAD_DIM, (h + 1) * MEM_HEAD_DIM)
        q = _rms(zq[:, sl], qn).astype(BF16)
        s = _dot_nt(q, mk_ref[0, i, _head_rows(h), :].astype(BF16)) * MEM_SCALE
        p = jnp.exp(s - jnp.max(s, axis=-1, keepdims=True))
        p = p * (1.0 / jnp.sum(p, axis=-1, keepdims=True))
        o = _dot(p.astype(BF16), mv_ref[0, i, _head_rows(h), :].astype(BF16))
        cat_ref[i, :, col0 + h * MEM_HEAD_DIM:col0 + (h + 1) * MEM_HEAD_DIM] = o.astype(cat_ref.dtype)


def _mixer_pool_kernel(x_ref, g_ref, w_in_ref, qn_ref, mk_ref, mv_ref, hist_ref, wp_ref, ps_ref,
                       cat_ref, state_ref, ubuf, *, pos0):
    nb, tl, _ = x_ref.shape
    j = pl.program_id(1)
    z = _in_proj(x_ref, g_ref, w_in_ref)

    @pl.when(j == 0)
    def _():
        ubuf[:, 0:HIST_ROWS, :] = hist_ref[0]

    pos = pos0 + j * tl + lax.broadcasted_iota(jnp.int32, (tl, 1), 0)
    for i in range(nb):
        zi = z[i * tl:(i + 1) * tl]
        u = zi[:, :POOL_WIDTH]
        ubuf[i, HIST_ROWS:HIST_ROWS + tl, :] = u
        for g, w in enumerate(POOL_WINDOWS):
            sl = slice(g * POOL_GROUP_DIM, (g + 1) * POOL_GROUP_DIM)
            ug = u[:, sl]
            acc = ug
            for k in range(1, w):
                acc = acc + ubuf[i, HIST_ROWS - k:HIST_ROWS - k + tl, sl]
            cnt = jnp.minimum(pos + 1, w).astype(F32)
            d = acc / cnt - ug
            y = _dot(d.astype(BF16), wp_ref[0, g]) * ps_ref[0, :, sl]
            cat_ref[i, :, sl] = y.astype(cat_ref.dtype)
        _mem_attend(zi[:, POOL_WIDTH:], qn_ref[0], mk_ref, mv_ref, i, cat_ref, POOL_WIDTH)
        tail = ubuf[i, tl:tl + HIST_ROWS, :]
        state_ref[0, i] = tail
        ubuf[i, 0:HIST_ROWS, :] = tail


def _mixer_pool(layer, x, hist, mem_k, mem_v, W, *, pos0, nb, tl):
    B, L, _ = x.shape
    assert tl >= HIST_ROWS and L % tl == 0 and B % nb == 0
    hist_spec = pl.BlockSpec((1, nb, HIST_ROWS, POOL_WIDTH), lambda b, j: (layer, b, 0, 0))
    return pl.pallas_call(
        functools.partial(_mixer_pool_kernel, pos0=pos0),
        grid=(B // nb, L // tl),
        in_specs=[
            pl.BlockSpec((nb, tl, D_MODEL), lambda b, j: (b, j, 0)),
            _layer_spec(layer, (1, D_MODEL)),
            _layer_spec(layer, (D_MODEL, MIX_WIDTH)),
            _layer_spec(layer, (1, MEM_HEAD_DIM)),
            _mem_spec(layer, nb),
            _mem_spec(layer, nb),
            hist_spec,
            _layer_spec(layer, (POOL_GROUPS, POOL_GROUP_DIM, POOL_GROUP_DIM)),
            _layer_spec(layer, (1, POOL_WIDTH)),
        ],
        out_specs=[
            pl.BlockSpec((nb, tl, MIX_WIDTH), lambda b, j: (b, j, 0)),
            pl.BlockSpec((1, nb, HIST_ROWS, POOL_WIDTH), lambda b, j: (0, b, 0, 0)),
        ],
        out_shape=[
            jax.ShapeDtypeStruct((B, L, MIX_WIDTH), BF16),
            jax.ShapeDtypeStruct((1, B, HIST_ROWS, POOL_WIDTH), F32),
        ],
        scratch_shapes=[pltpu.VMEM((nb, HIST_ROWS + tl, POOL_WIDTH), F32)],
        compiler_params=_params(2),
        name="mixer_pool",
    )(x, W["g_mix"], W["w_in"], W["q_norm_mem"], mem_k, mem_v, hist, W["w_pool"], W["pool_scale"])


def _kv_proj_kernel(x_ref, g_ref, w_ref, kn_ref, bf_ref, k_ref, v_ref, lf_ref, kb_ref, vb_ref):
    nb, tl, _ = x_ref.shape
    rows = nb * tl
    x = x_ref[...].reshape(rows, D_MODEL)
    zt = _dot_nt(w_ref[...], _rms(x, g_ref[...]).astype(BF16))
    k3 = zt[:FOX_WIDTH].reshape(FOX_HEADS, FOX_HEAD_DIM, rows)
    ms = jnp.mean(k3 * k3, axis=1, keepdims=True)
    k3 = (k3 * lax.rsqrt(ms + EPS)) * kn_ref[...]
    v3 = zt[FOX_WIDTH:2 * FOX_WIDTH].reshape(FOX_HEADS, FOX_HEAD_DIM, rows)
    t = -(zt[2 * FOX_WIDTH:2 * FOX_WIDTH + FOX_HEADS] + bf_ref[...])
    lf = -(jnp.maximum(t, 0.0) + jnp.log1p(jnp.exp(-jnp.abs(t))))
    ones = jnp.ones((FOX_HEADS, FOX_HEAD_DIM, tl), BF16)
    for i in range(nb):
        cols = slice(i * tl, (i + 1) * tl)
        k_ref[i] = k3[:, :, cols]
        v_ref[i] = v3[:, :, cols]
        lf_ref[i] = lf[:, cols]
        kb_ref[i] = k3[:, :, cols].astype(BF16)
        vb_ref[i, :, 0:FOX_HEAD_DIM, :] = v3[:, :, cols].astype(BF16)
        vb_ref[i, :, FOX_HEAD_DIM:2 * FOX_HEAD_DIM, :] = ones


def _kv_proj(x, W, *, nb, tl):
    B, L, _ = x.shape
    hd = lambda rows: pl.BlockSpec((nb, FOX_HEADS, rows, tl), lambda b, j: (b, 0, 0, j))
    heads = jax.ShapeDtypeStruct((B, FOX_HEADS, FOX_HEAD_DIM, L), F32)
    return pl.pallas_call(
        _kv_proj_kernel,
        grid=(B // nb, L // tl),
        in_specs=[
            pl.BlockSpec((nb, tl, D_MODEL), lambda b, j: (b, j, 0)),
            _const_spec((1, D_MODEL)),
            _const_spec((KV_ROWS, D_MODEL)),
            _const_spec((1, FOX_HEAD_DIM, 1)),
            _const_spec((FOX_HEADS, 1)),
        ],
        out_specs=[hd(FOX_HEAD_DIM), hd(FOX_HEAD_DIM),
                   pl.BlockSpec((nb, FOX_HEADS, tl), lambda b, j: (b, 0, j)),
                   hd(FOX_HEAD_DIM), hd(2 * FOX_HEAD_DIM)],
        out_shape=[heads, heads, jax.ShapeDtypeStruct((B, FOX_HEADS, L), F32),
                   jax.ShapeDtypeStruct((B, FOX_HEADS, FOX_HEAD_DIM, L), BF16),
                   jax.ShapeDtypeStruct((B, FOX_HEADS, 2 * FOX_HEAD_DIM, L), BF16)],
        compiler_params=_params(2),
        name="kv_proj",
    )(x, W["g_kv"], W["w_kv_t"], W["kn_col"], W["b_f"])


def _cumsum_kernel(lf_ref, f_ref):
    rows, n = lf_ref.shape
    r = lax.broadcasted_iota(jnp.int32, (LANES, LANES), 0)
    c = lax.broadcasted_iota(jnp.int32, (LANES, LANES), 1)
    tri = jnp.where(r <= c, 1.0, 0.0).astype(BF16)
    carry = jnp.zeros((rows, 1), F32)
    for ch in range(n // LANES):
        x = lf_ref[:, ch * LANES:(ch + 1) * LANES]
        hi = x.astype(BF16)
        r1 = x - hi.astype(F32)
        mid = r1.astype(BF16)
        low = (r1 - mid.astype(F32)).astype(BF16)
        y = (_dot(hi, tri) + _dot(mid, tri)) + _dot(low, tri) + carry
        f_ref[:, ch * LANES:(ch + 1) * LANES] = y
        carry = y[:, LANES - 1:LANES]


def _cumsum_lanes(lf_t):
    return pl.pallas_call(
        _cumsum_kernel,
        out_shape=jax.ShapeDtypeStruct(lf_t.shape, F32),
        name="logf_cumsum",
    )(lf_t)


def _split3(x):
    hi = x.astype(BF16).astype(F32)
    r1 = x - hi
    mid = r1.astype(BF16).astype(F32)
    low = (r1 - mid).astype(BF16).astype(F32)
    return hi, mid, low


def _mixer_fox_prompt_kernel(x_ref, g_ref, w_in_ref, qn_ref, mk_ref, mv_ref, qnf_ref,
                             kt_ref, vt_ref, ft_ref, fq_ref, cat_ref, q_sc, m_sc, acc_sc):
    _, tl, _ = x_ref.shape
    tk = kt_ref.shape[-1]
    j = pl.program_id(1)
    kb = pl.program_id(2)
    lane = lax.broadcasted_iota(jnp.int32, (1, LANES), 1)
    row = lax.broadcasted_iota(jnp.int32, (FOX_HEAD_DIM, 1), 0)

    @pl.when(kb == 0)
    def _():
        z = _in_proj(x_ref, g_ref, w_in_ref)
        _mem_attend(z[:, FOX_WIDTH:], qn_ref[0], mk_ref, mv_ref, 0, cat_ref, FOX_WIDTH)
        qs = _rms_head64(z[:, :FOX_WIDTH], qnf_ref[0])
        for h in range(FOX_HEADS):
            base = qs[h // 2] if h % 2 == 0 else pltpu.roll(qs[h // 2], FOX_HEAD_DIM, 1)
            hi, mid, low = _split3(fq_ref[0, :, h:h + 1])
            tail = jnp.where(lane < FOX_HEAD_DIM + 3, 1.0,
                             jnp.where(lane == FOX_HEAD_DIM + 3, hi,
                                       jnp.where(lane == FOX_HEAD_DIM + 4, mid,
                                                 jnp.where(lane == FOX_HEAD_DIM + 5, low, 0.0))))
            q_sc[h] = jnp.where(lane < FOX_HEAD_DIM, base * FOX_SCALE, tail).astype(BF16)
        m_sc[...] = jnp.full(m_sc.shape, -jnp.inf, F32)
        acc_sc[...] = jnp.zeros(acc_sc.shape, F32)

    def attend(masked):
        causal = (lax.broadcasted_iota(jnp.int32, (tl, 1), 0)
                  >= lax.broadcasted_iota(jnp.int32, (1, tk), 1))
        for h in range(FOX_HEADS):
            hi, mid, low = _split3(-ft_ref[0, 0, h:h + 1, :])
            aug = jnp.where(row == 0, hi, jnp.where(row == 1, mid, jnp.where(row == 2, low,
                            jnp.where(row < 6, 1.0, 0.0)))).astype(BF16)
            k_aug = jnp.concatenate([kt_ref[0, h], aug], axis=0)
            s = _dot(q_sc[h], k_aug)
            if masked:
                s = jnp.where(causal, s, -jnp.inf)
            m_old = m_sc[h]
            m_new = jnp.maximum(m_old, jnp.max(s, axis=-1, keepdims=True))
            p = jnp.exp(s - m_new[:, 0:1]).astype(BF16)
            acc_sc[h] = jnp.exp(m_old - m_new) * acc_sc[h] + _dot_nt(p, vt_ref[0, h])
            m_sc[h] = m_new

    @pl.when(kb < j)
    def _():
        attend(False)

    @pl.when(kb == j)
    def _():
        attend(True)
        for c in range(FOX_HEADS // 2):
            a0 = acc_sc[2 * c]
            a1 = acc_sc[2 * c + 1]
            o0 = a0 * (1.0 / a0[:, FOX_HEAD_DIM:FOX_HEAD_DIM + 1])
            o1 = a1 * (1.0 / a1[:, FOX_HEAD_DIM:FOX_HEAD_DIM + 1])
            pair = jnp.where(lane < FOX_HEAD_DIM, o0, pltpu.roll(o1, FOX_HEAD_DIM, 1))
            cat_ref[0, :, c * LANES:(c + 1) * LANES] = pair.astype(cat_ref.dtype)


def _mixer_fox_prompt(layer, x, kt_b, vt_b, f_t, f_q, mem_k, mem_v, W, *, tl):
    B, L, _ = x.shape
    nkb = L // tl
    ft4 = jnp.swapaxes(f_t.reshape(B, FOX_HEADS, nkb, tl), 1, 2)
    kv_spec = lambda rows: pl.BlockSpec((1, FOX_HEADS, rows, tl),
                                        lambda b, j, kb: (b, 0, 0, jnp.minimum(kb, j)))
    return pl.pallas_call(
        _mixer_fox_prompt_kernel,
        grid=(B, nkb, nkb),
        in_specs=[
            pl.BlockSpec((1, tl, D_MODEL), lambda b, j, kb: (b, j, 0)),
            _layer_spec(layer, (1, D_MODEL)),
            _layer_spec(layer, (D_MODEL, MIX_WIDTH)),
            _layer_spec(layer, (1, MEM_HEAD_DIM)),
            _mem_spec(layer, 1),
            _mem_spec(layer, 1),
            _layer_spec(layer - N_A, (1, LANES)),
            kv_spec(FOX_HEAD_DIM),
            kv_spec(2 * FOX_HEAD_DIM),
            pl.BlockSpec((1, 1, FOX_HEADS, tl), lambda b, j, kb: (b, jnp.minimum(kb, j), 0, 0)),
            pl.BlockSpec((1, tl, FOX_HEADS), lambda b, j, kb: (b, j, 0)),
        ],
        out_specs=pl.BlockSpec((1, tl, MIX_WIDTH), lambda b, j, kb: (b, j, 0)),
        out_shape=jax.ShapeDtypeStruct((B, L, MIX_WIDTH), BF16),
        scratch_shapes=[
            pltpu.VMEM((FOX_HEADS, tl, LANES), BF16),
            pltpu.VMEM((FOX_HEADS, tl, LANES), F32),
            pltpu.VMEM((FOX_HEADS, tl, LANES), F32),
        ],
        compiler_params=_params(3),
        name="mixer_fox_prompt",
    )(x, W["g_mix"], W["w_in"], W["q_norm_mem"], mem_k, mem_v, W["qnf_pair"], kt_b, vt_b, ft4, f_q)


def _mixer_fox_sample_kernel(x_ref, g_ref, w_in_ref, qn_ref, mk_ref, mv_ref, qnf_ref,
                             ktp_ref, vtp_ref, ktn_ref, vtn_ref, ft_ref, fq_ref, cat_ref):
    nb, tl, _ = x_ref.shape
    past = ktp_ref.shape[-1]
    z = _in_proj(x_ref, g_ref, w_in_ref)
    causal = (lax.broadcasted_iota(jnp.int32, (tl, 1), 0)
              >= lax.broadcasted_iota(jnp.int32, (1, tl), 1))
    for i in range(nb):
        zi = z[i * tl:(i + 1) * tl]
        qs = _rms_head64(zi[:, :FOX_WIDTH], qnf_ref[0])
        for h in range(FOX_HEADS):
            half = h % 2
            q = (qs[h // 2][:, half * FOX_HEAD_DIM:(half + 1) * FOX_HEAD_DIM] * FOX_SCALE).astype(BF16)
            fq = fq_ref[i, :, h:h + 1]
            s_p = (_dot(q, ktp_ref[i, h].astype(BF16)) + fq) - ft_ref[i, h:h + 1, 0:past]
            s_n = (_dot(q, ktn_ref[i, h].astype(BF16)) + fq) - ft_ref[i, h:h + 1, past:past + tl]
            s_n = jnp.where(causal, s_n, -jnp.inf)
            m = jnp.maximum(jnp.max(s_p, axis=-1, keepdims=True), jnp.max(s_n, axis=-1, keepdims=True))
            p_p = jnp.exp(s_p - m)
            p_n = jnp.exp(s_n - m)
            l = jnp.sum(p_p, axis=-1, keepdims=True) + jnp.sum(p_n, axis=-1, keepdims=True)
            o = (_dot_nt(p_p.astype(BF16), vtp_ref[i, h].astype(BF16))
                 + _dot_nt(p_n.astype(BF16), vtn_ref[i, h].astype(BF16)))
            cat_ref[i, :, h * FOX_HEAD_DIM:(h + 1) * FOX_HEAD_DIM] = (o * (1.0 / l)).astype(cat_ref.dtype)
        _mem_attend(zi[:, FOX_WIDTH:], qn_ref[0], mk_ref, mv_ref, i, cat_ref, FOX_WIDTH)


def _mixer_fox_sample(layer, x, kt_past, vt_past, kt_new, vt_new, f_t, f_q, mem_k, mem_v, W, *, nb):
    B, L, _ = x.shape
    past = kt_past.shape[-1]
    lk_pad = f_t.shape[-1]
    per_b = lambda *tail: pl.BlockSpec((nb,) + tail, lambda b: (b,) + (0,) * len(tail))
    return pl.pallas_call(
        _mixer_fox_sample_kernel,
        grid=(B // nb,),
        in_specs=[
            per_b(L, D_MODEL),
            _layer_spec(layer, (1, D_MODEL)),
            _layer_spec(layer, (D_MODEL, MIX_WIDTH)),
            _layer_spec(layer, (1, MEM_HEAD_DIM)),
            _mem_spec(layer, nb),
            _mem_spec(layer, nb),
            _layer_spec(layer - N_A, (1, LANES)),
            per_b(FOX_HEADS, FOX_HEAD_DIM, past),
            per_b(FOX_HEADS, FOX_HEAD_DIM, past),
            per_b(FOX_HEADS, FOX_HEAD_DIM, L),
            per_b(FOX_HEADS, FOX_HEAD_DIM, L),
            per_b(FOX_HEADS, lk_pad),
            per_b(L, FOX_HEADS),
        ],
        out_specs=per_b(L, MIX_WIDTH),
        out_shape=jax.ShapeDtypeStruct((B, L, MIX_WIDTH), BF16),
        compiler_params=_params(1),
        name="mixer_fox_sample",
    )(x, W["g_mix"], W["w_in"], W["q_norm_mem"], mem_k, mem_v, W["qnf_pair"],
      kt_past, vt_past, kt_new, vt_new, f_t, f_q)


def _out_ffn_kernel(x_ref, cat_ref, w_out_ref, g_ref, w_gu_ref, w_down_ref, y_ref):
    x1 = x_ref[...] + _dot(cat_ref[...], w_out_ref[0])
    xn = _rms(x1, g_ref[0]).astype(BF16)
    cw = D_FF // FFN_CHUNKS
    acc = x1
    for c in range(FFN_CHUNKS):
        gate = _dot(xn, w_gu_ref[0, :, c * cw:(c + 1) * cw])
        up = _dot(xn, w_gu_ref[0, :, D_FF + c * cw:D_FF + (c + 1) * cw])
        h = (gate * (1.0 / (1.0 + jnp.exp(-gate)))) * up
        acc = acc + _dot(h.astype(BF16), w_down_ref[0, c * cw:(c + 1) * cw, :])
    y_ref[...] = acc


def _out_ffn(layer, x2, cat2, W):
    T = x2.shape[0]
    return pl.pallas_call(
        _out_ffn_kernel,
        grid=(T // ROW_BLOCK,),
        in_specs=[
            pl.BlockSpec((ROW_BLOCK, D_MODEL), lambda r: (r, 0)),
            pl.BlockSpec((ROW_BLOCK, MIX_WIDTH), lambda r: (r, 0)),
            _layer_spec(layer, (MIX_WIDTH, D_MODEL)),
            _layer_spec(layer, (1, D_MODEL)),
            _layer_spec(layer, (D_MODEL, 2 * D_FF)),
            _layer_spec(layer, (D_FF, D_MODEL)),
        ],
        out_specs=pl.BlockSpec((ROW_BLOCK, D_MODEL), lambda r: (r, 0)),
        out_shape=jax.ShapeDtypeStruct((T, D_MODEL), F32),
        compiler_params=_params(1),
        name="out_ffn",
    )(x2, cat2, W["w_out"], W["g_ffn"], W["w_gu"], W["w_down"])


def _trunk(x, pos0, pool_hist, fox_past, mem_k, mem_v, W, *, nb, tl):
    B, L, _ = x.shape
    pool_states = []
    for i in range(DEPTH):
        if i < N_A:
            cat, state = _mixer_pool(i, x, pool_hist, mem_k, mem_v, W, pos0=pos0, nb=nb, tl=tl)
            pool_states.append(state[:, :, 1:, :])
        else:
            if i == N_A:
                kt_new, vt_new, lft_new, kt_b, vt_b = _kv_proj(x, W, nb=nb, tl=tl)
                if fox_past is None:
                    lft_all = lft_new
                else:
                    lft_all = jnp.concatenate([fox_past[2], lft_new], axis=2)
                lk = lft_all.shape[2]
                lk_pad = -(-lk // LANES) * LANES
                lf_t = jnp.pad(lft_all.reshape(B * FOX_HEADS, lk), ((0, 0), (0, lk_pad - lk)))
                f_t = _cumsum_lanes(lf_t).reshape(B, FOX_HEADS, lk_pad)
                f_q = jnp.swapaxes(f_t[:, :, lk - L:lk], 1, 2)
            if fox_past is None:
                cat = _mixer_fox_prompt(i, x, kt_b, vt_b, f_t, f_q, mem_k, mem_v, W, tl=tl)
            else:
                cat = _mixer_fox_sample(i, x, fox_past[0], fox_past[1], kt_new, vt_new, f_t, f_q,
                                        mem_k, mem_v, W, nb=2)
        x = _out_ffn(i, x.reshape(B * L, D_MODEL), cat.reshape(B * L, MIX_WIDTH), W
                     ).reshape(B, L, D_MODEL)
    fox_new = (jnp.transpose(kt_new, (0, 3, 1, 2)), jnp.transpose(vt_new, (0, 3, 1, 2)),
               jnp.swapaxes(lft_new, 1, 2))
    return x, jnp.concatenate(pool_states, axis=0), fox_new


def kernel(x_prompt, x_sample, state_pool, cache_fox_k, cache_fox_v, cache_fox_logf, cache_mem_k,
           cache_mem_v, mem_prompt, g_mix, w_in, w_out, q_norm_mem, g_mem, w_mem_kv, k_norm_mem,
           w_pool, pool_scale, q_norm_fox, g_kv, w_kv, k_norm_fox, b_f, g_ffn, w_gu, w_down):
    B, L, _ = x_prompt.shape
    SB, SL, _ = x_sample.shape
    W = dict(
        g_mix=g_mix.reshape(DEPTH, 1, D_MODEL), w_in=w_in.astype(BF16), w_out=w_out.astype(BF16),
        q_norm_mem=q_norm_mem.reshape(DEPTH, 1, MEM_HEAD_DIM), w_pool=w_pool.astype(BF16),
        pool_scale=pool_scale.reshape(N_A, 1, POOL_WIDTH),
        qnf_pair=jnp.tile(q_norm_fox, (1, 2)).reshape(DEPTH - N_A, 1, LANES),
        g_kv=g_kv.reshape(1, D_MODEL),
        w_kv_t=jnp.pad(w_kv.T, ((0, KV_ROWS - w_kv.shape[1]), (0, 0))).astype(BF16),
        kn_col=k_norm_fox.reshape(1, FOX_HEAD_DIM, 1), b_f=b_f.reshape(FOX_HEADS, 1),
        g_ffn=g_ffn.reshape(DEPTH, 1, D_MODEL), w_gu=w_gu.astype(BF16), w_down=w_down.astype(BF16))

    mem_k_p, mem_v_p = _mem_kv(mem_prompt, g_mem.reshape(DEPTH, 1, D_MODEL), w_mem_kv.astype(BF16),
                               k_norm_mem.reshape(DEPTH, 1, MEM_HEAD_DIM))
    hist_p = jnp.zeros((N_A, B, HIST_ROWS, POOL_WIDTH), F32)
    y_p, pool_p, fox_p = _trunk(x_prompt, 0, hist_p, None, mem_k_p, mem_v_p, W, nb=1, tl=ROW_BLOCK)

    hist_s = jnp.pad(state_pool, ((0, 0), (0, 0), (1, 0), (0, 0)))
    past = (jnp.transpose(cache_fox_k, (0, 2, 3, 1)), jnp.transpose(cache_fox_v, (0, 2, 3, 1)),
            jnp.swapaxes(cache_fox_logf, 1, 2))
    y_s, pool_s, fox_s = _trunk(
        x_sample, PAST_LEN, hist_s, past, cache_mem_k.reshape(DEPTH, SB, MEM_ROWS, MEM_HEAD_DIM),
        cache_mem_v.reshape(DEPTH, SB, MEM_ROWS, MEM_HEAD_DIM), W, nb=ROW_BLOCK // SL, tl=SL)

    mem_shape = (DEPTH, B, MEM_TOKENS, MEM_HEADS, MEM_HEAD_DIM)
    return (y_p, y_s, pool_p, fox_p[0], fox_p[1], fox_p[2], mem_k_p.reshape(mem_shape),
            mem_v_p.reshape(mem_shape), pool_s, fox_s[0], fox_s[1], fox_s[2])
```

```python
import functools

import jax
import jax.numpy as jnp
from jax import lax
from jax.experimental import pallas as pl
from jax.experimental.pallas import tpu as pltpu

F32 = jnp.float32
BF16 = jnp.bfloat16

D_MODEL = 1024
DEPTH = 4
N_A = DEPTH // 2
PAST_LEN = 1024
POOL_WINDOWS = (2, 4, 8, 16)
POOL_GROUPS = len(POOL_WINDOWS)
POOL_WIDTH = D_MODEL // 2
POOL_GROUP_DIM = POOL_WIDTH // POOL_GROUPS
POOL_HIST = max(POOL_WINDOWS) - 1
HIST_ROWS = POOL_HIST + 1
FOX_HEAD_DIM = 64
FOX_WIDTH = D_MODEL // 2
FOX_HEADS = FOX_WIDTH // FOX_HEAD_DIM
MEM_TOKENS = 256
MEM_HEADS = 4
MEM_WIDTH = D_MODEL // 2
MEM_HEAD_DIM = MEM_WIDTH // MEM_HEADS
MIX_WIDTH = POOL_WIDTH + MEM_WIDTH
D_FF = ((8 * D_MODEL // 3 + 255) // 256) * 256
EPS = 1e-6
FOX_SCALE = FOX_HEAD_DIM ** -0.5
MEM_SCALE = MEM_HEAD_DIM ** -0.5
LOG2E = 1.4426950408889634
AUG_ROWS = 16

LANES = 128
ROW_BLOCK = 512
KV_ROWS = 2 * FOX_WIDTH + 16
FFN_CHUNKS = 2
VMEM_LIMIT = 56 * 1024 * 1024


def _dot(a, b):
    return jnp.dot(a, b, preferred_element_type=F32)


def _dot_nt(a, b):
    return lax.dot_general(a, b, (((1,), (1,)), ((), ())), preferred_element_type=F32)


def _rms(x, g):
    ms = jnp.mean(x * x, axis=-1, keepdims=True)
    return (x * lax.rsqrt(ms + EPS)) * g


def _rms_head64(x, g_pair):
    lo = lax.broadcasted_iota(jnp.int32, (1, LANES), 1) < FOX_HEAD_DIM
    outs = []
    for c in range(x.shape[-1] // LANES):
        xc = x[:, c * LANES:(c + 1) * LANES]
        sq = xc * xc
        s_lo = jnp.sum(jnp.where(lo, sq, 0.0), axis=-1, keepdims=True)
        s_hi = jnp.sum(jnp.where(lo, 0.0, sq), axis=-1, keepdims=True)
        ms = jnp.where(lo, s_lo, s_hi) * (1.0 / FOX_HEAD_DIM)
        outs.append((xc * lax.rsqrt(ms + EPS)) * g_pair)
    return outs


def _const_spec(shape):
    return pl.BlockSpec(shape, lambda *_: (0,) * len(shape), pipeline_mode=pl.Buffered(1))


def _layer_spec(layer, shape):
    return pl.BlockSpec((1,) + shape, lambda *_: (layer,) + (0,) * len(shape),
                        pipeline_mode=pl.Buffered(1))


MEM_ROWS = MEM_TOKENS * MEM_HEADS


def _mem_spec(layer, nb):
    return pl.BlockSpec((1, nb, MEM_ROWS, MEM_HEAD_DIM), lambda b, *_: (layer, b, 0, 0))


def _head_rows(h):
    return pl.ds(h, MEM_TOKENS, stride=MEM_HEADS)


def _params(n_grid, flags=None):
    return pltpu.CompilerParams(
        dimension_semantics=("arbitrary",) * n_grid, vmem_limit_bytes=VMEM_LIMIT, flags=flags)


def _mem_kv_kernel(mem_ref, g_ref, w_ref, kn_ref, k_ref, v_ref):
    nb = mem_ref.shape[0]
    x = mem_ref[...].reshape(nb * MEM_TOKENS, D_MODEL)
    kv = _dot(_rms(x, g_ref[0]).astype(BF16), w_ref[0])
    for h in range(MEM_HEADS):
        ks = slice(h * MEM_HEAD_DIM, (h + 1) * MEM_HEAD_DIM)
        vs = slice(MEM_WIDTH + h * MEM_HEAD_DIM, MEM_WIDTH + (h + 1) * MEM_HEAD_DIM)
        k_ref[0, :, _head_rows(h), :] = _rms(kv[:, ks], kn_ref[0]).reshape(nb, MEM_TOKENS, MEM_HEAD_DIM)
        v_ref[0, :, _head_rows(h), :] = kv[:, vs].reshape(nb, MEM_TOKENS, MEM_HEAD_DIM)


def _mem_kv(mem, g_mem, w_mem_kv, k_norm_mem):
    B = mem.shape[0]
    nb = 4
    out = jax.ShapeDtypeStruct((DEPTH, B, MEM_ROWS, MEM_HEAD_DIM), F32)
    out_spec = pl.BlockSpec((1, nb, MEM_ROWS, MEM_HEAD_DIM), lambda i, b: (i, b, 0, 0))
    return pl.pallas_call(
        _mem_kv_kernel,
        grid=(DEPTH, B // nb),
        in_specs=[
            pl.BlockSpec((nb, MEM_TOKENS, D_MODEL), lambda i, b: (b, 0, 0)),
            pl.BlockSpec((1, 1, D_MODEL), lambda i, b: (i, 0, 0)),
            pl.BlockSpec((1, D_MODEL, 2 * MEM_WIDTH), lambda i, b: (i, 0, 0)),
            pl.BlockSpec((1, 1, MEM_HEAD_DIM), lambda i, b: (i, 0, 0)),
        ],
        out_specs=[out_spec, out_spec],
        out_shape=[out, out],
        compiler_params=_params(2),
        name="mem_kv",
    )(mem, g_mem, w_mem_kv, k_norm_mem)


def _in_proj(x_ref, g_ref, w_ref):
    nb, tl, _ = x_ref.shape
    x = x_ref[...].reshape(nb * tl, D_MODEL)
    return _dot(_rms(x, g_ref[0]).astype(BF16), w_ref[0])


def _mem_attend(zq, qn, mk_ref, mv_ref, i, cat_ref, col0):
    for h in range(MEM_HEADS):
        sl = slice(h * MEM_HEAD_DIM, (h + 1) * MEM_HEAD_DIM)
        q = _rms(zq[:, sl], qn).astype(BF16)
        s = _dot_nt(q, mk_ref[0, i, _head_rows(h), :].astype(BF16)) * (MEM_SCALE * LOG2E)
        p = jnp.exp2(s - jnp.max(s, axis=-1, keepdims=True)).astype(BF16)
        v = mv_ref[0, i, _head_rows(h), :].astype(BF16)
        o = _dot(p, jnp.concatenate([v, jnp.ones_like(v)], axis=-1))
        o = o[:, :MEM_HEAD_DIM] / o[:, MEM_HEAD_DIM:]
        cat_ref[i, :, col0 + h * MEM_HEAD_DIM:col0 + (h + 1) * MEM_HEAD_DIM] = o.astype(cat_ref.dtype)


def _mixer_pool_kernel(x_ref, g_ref, w_in_ref, qn_ref, mk_ref, mv_ref, hist_ref, wp_ref, ps_ref,
                       cat_ref, state_ref, ubuf, *, pos0):
    nb, tl, _ = x_ref.shape
    j = pl.program_id(1)
    z = _in_proj(x_ref, g_ref, w_in_ref)

    @pl.when(j == 0)
    def _():
        ubuf[:, 0:HIST_ROWS, :] = hist_ref[0]

    pos = pos0 + j * tl + lax.broadcasted_iota(jnp.int32, (tl, 1), 0)
    for i in range(nb):
        zi = z[i * tl:(i + 1) * tl]
        u = zi[:, :POOL_WIDTH]
        ubuf[i, HIST_ROWS:HIST_ROWS + tl, :] = u
        for g, w in enumerate(POOL_WINDOWS):
            sl = slice(g * POOL_GROUP_DIM, (g + 1) * POOL_GROUP_DIM)
            ug = u[:, sl]
            acc = ug
            for k in range(1, w):
                acc = acc + ubuf[i, HIST_ROWS - k:HIST_ROWS - k + tl, sl]
            cnt = jnp.minimum(pos + 1, w).astype(F32)
            d = acc / cnt - ug
            y = _dot(d.astype(BF16), wp_ref[0, g]) * ps_ref[0, :, sl]
            cat_ref[i, :, sl] = y.astype(cat_ref.dtype)
        _mem_attend(zi[:, POOL_WIDTH:], qn_ref[0], mk_ref, mv_ref, i, cat_ref, POOL_WIDTH)
        tail = ubuf[i, tl:tl + HIST_ROWS, :]
        state_ref[0, i] = tail
        ubuf[i, 0:HIST_ROWS, :] = tail


def _mixer_pool(layer, x, hist, mem_k, mem_v, W, *, pos0, nb, tl):
    B, L, _ = x.shape
    assert tl >= HIST_ROWS and L % tl == 0 and B % nb == 0
    hist_spec = pl.BlockSpec((1, nb, HIST_ROWS, POOL_WIDTH), lambda b, j: (layer, b, 0, 0))
    return pl.pallas_call(
        functools.partial(_mixer_pool_kernel, pos0=pos0),
        grid=(B // nb, L // tl),
        in_specs=[
            pl.BlockSpec((nb, tl, D_MODEL), lambda b, j: (b, j, 0)),
            _layer_spec(layer, (1, D_MODEL)),
            _layer_spec(layer, (D_MODEL, MIX_WIDTH)),
            _layer_spec(layer, (1, MEM_HEAD_DIM)),
            _mem_spec(layer, nb),
            _mem_spec(layer, nb),
            hist_spec,
            _layer_spec(layer, (POOL_GROUPS, POOL_GROUP_DIM, POOL_GROUP_DIM)),
            _layer_spec(layer, (1, POOL_WIDTH)),
        ],
        out_specs=[
            pl.BlockSpec((nb, tl, MIX_WIDTH), lambda b, j: (b, j, 0)),
            pl.BlockSpec((1, nb, HIST_ROWS, POOL_WIDTH), lambda b, j: (0, b, 0, 0)),
        ],
        out_shape=[
            jax.ShapeDtypeStruct((B, L, MIX_WIDTH), BF16),
            jax.ShapeDtypeStruct((1, B, HIST_ROWS, POOL_WIDTH), F32),
        ],
        scratch_shapes=[pltpu.VMEM((nb, HIST_ROWS + tl, POOL_WIDTH), F32)],
        compiler_params=_params(2),
        name="mixer_pool",
    )(x, W["g_mix"], W["w_in"], W["q_norm_mem"], mem_k, mem_v, hist, W["w_pool"], W["pool_scale"])


def _kv_proj_kernel(x_ref, g_ref, w_ref, wv_ref, kn_ref, bf_ref, k_ref, v_ref, lf_ref, kb_ref, vb_ref):
    nb, tl, _ = x_ref.shape
    rows = nb * tl
    x = x_ref[...].reshape(rows, D_MODEL)
    xn = _rms(x, g_ref[...]).astype(BF16)
    zt = _dot_nt(w_ref[...], xn)
    zv = _dot(xn, wv_ref[...])
    lo = lax.broadcasted_iota(jnp.int32, (1, LANES), 1) < FOX_HEAD_DIM
    for h in range(FOX_HEADS):
        pair = zv[:, (h // 2) * LANES:(h // 2 + 1) * LANES]
        base = pair if h % 2 == 0 else pltpu.roll(pair, FOX_HEAD_DIM, 1)
        vb_ref[:, h] = jnp.where(lo, base, 1.0).astype(BF16).reshape(nb, tl, LANES)
    k3 = zt[:FOX_WIDTH].reshape(FOX_HEADS, FOX_HEAD_DIM, rows)
    ms = jnp.mean(k3 * k3, axis=1, keepdims=True)
    k3 = (k3 * lax.rsqrt(ms + EPS)) * kn_ref[...]
    v3 = zt[FOX_WIDTH:2 * FOX_WIDTH].reshape(FOX_HEADS, FOX_HEAD_DIM, rows)
    t = -(zt[2 * FOX_WIDTH:2 * FOX_WIDTH + FOX_HEADS] + bf_ref[...])
    lf = -(jnp.maximum(t, 0.0) + jnp.log1p(jnp.exp(-jnp.abs(t))))
    for i in range(nb):
        cols = slice(i * tl, (i + 1) * tl)
        k_ref[i] = k3[:, :, cols]
        v_ref[i] = v3[:, :, cols]
        lf_ref[i] = lf[:, cols]
        kb_ref[i] = k3[:, :, cols].astype(BF16)


def _kv_proj(x, W, *, nb, tl):
    B, L, _ = x.shape
    hd = pl.BlockSpec((nb, FOX_HEADS, FOX_HEAD_DIM, tl), lambda b, j: (b, 0, 0, j))
    heads = jax.ShapeDtypeStruct((B, FOX_HEADS, FOX_HEAD_DIM, L), F32)
    return pl.pallas_call(
        _kv_proj_kernel,
        grid=(B // nb, L // tl),
        in_specs=[
            pl.BlockSpec((nb, tl, D_MODEL), lambda b, j: (b, j, 0)),
            _const_spec((1, D_MODEL)),
            _const_spec((KV_ROWS, D_MODEL)),
            _const_spec((D_MODEL, FOX_WIDTH)),
            _const_spec((1, FOX_HEAD_DIM, 1)),
            _const_spec((FOX_HEADS, 1)),
        ],
        out_specs=[hd, hd, pl.BlockSpec((nb, FOX_HEADS, tl), lambda b, j: (b, 0, j)), hd,
                   pl.BlockSpec((nb, FOX_HEADS, tl, LANES), lambda b, j: (b, 0, j, 0))],
        out_shape=[heads, heads, jax.ShapeDtypeStruct((B, FOX_HEADS, L), F32),
                   jax.ShapeDtypeStruct((B, FOX_HEADS, FOX_HEAD_DIM, L), BF16),
                   jax.ShapeDtypeStruct((B, FOX_HEADS, L, LANES), BF16)],
        compiler_params=_params(2),
        name="kv_proj",
    )(x, W["g_kv"], W["w_kv_t"], W["w_v"], W["kn_col"], W["b_f"])


def _cumsum_kernel(lf_ref, f_ref):
    rows, n = lf_ref.shape
    r = lax.broadcasted_iota(jnp.int32, (LANES, LANES), 0)
    c = lax.broadcasted_iota(jnp.int32, (LANES, LANES), 1)
    tri = jnp.where(r <= c, 1.0, 0.0).astype(BF16)
    carry = jnp.zeros((rows, 1), F32)
    for ch in range(n // LANES):
        x = lf_ref[:, ch * LANES:(ch + 1) * LANES]
        hi = x.astype(BF16)
        r1 = x - hi.astype(F32)
        mid = r1.astype(BF16)
        low = (r1 - mid.astype(F32)).astype(BF16)
        y = (_dot(hi, tri) + _dot(mid, tri)) + _dot(low, tri) + carry
        f_ref[:, ch * LANES:(ch + 1) * LANES] = y
        carry = y[:, LANES - 1:LANES]


def _cumsum_lanes(lf_t):
    return pl.pallas_call(
        _cumsum_kernel,
        out_shape=jax.ShapeDtypeStruct(lf_t.shape, F32),
        name="logf_cumsum",
    )(lf_t)


def _split3(x):
    hi = x.astype(BF16).astype(F32)
    r1 = x - hi
    mid = r1.astype(BF16).astype(F32)
    low = (r1 - mid).astype(BF16).astype(F32)
    return hi, mid, low


def _mixer_fox_prompt_kernel(x_ref, g_ref, w_in_ref, qn_ref, mk_ref, mv_ref, qnf_ref,
                             kt_ref, v_ref, ft_ref, fq_ref, cat_ref, q_sc, m_sc, acc_sc, mask_sc):
    _, tl, _ = x_ref.shape
    tk = kt_ref.shape[-1]
    j = pl.program_id(1)
    kb = pl.program_id(2)
    lane = lax.broadcasted_iota(jnp.int32, (1, LANES), 1)
    row = lax.broadcasted_iota(jnp.int32, (AUG_ROWS, 1), 0)

    @pl.when(kb == 0)
    def _():
        z = _in_proj(x_ref, g_ref, w_in_ref)
        _mem_attend(z[:, FOX_WIDTH:], qn_ref[0], mk_ref, mv_ref, 0, cat_ref, FOX_WIDTH)
        qs = _rms_head64(z[:, :FOX_WIDTH], qnf_ref[0])
        for h in range(FOX_HEADS):
            base = qs[h // 2] if h % 2 == 0 else pltpu.roll(qs[h // 2], FOX_HEAD_DIM, 1)
            hi, mid, low = _split3(fq_ref[0, :, h:h + 1] * LOG2E)
            tail = jnp.where(lane < FOX_HEAD_DIM + 3, 1.0,
                             jnp.where(lane == FOX_HEAD_DIM + 3, hi,
                                       jnp.where(lane == FOX_HEAD_DIM + 4, mid,
                                                 jnp.where(lane == FOX_HEAD_DIM + 5, low, 0.0))))
            q_sc[h] = jnp.where(lane < FOX_HEAD_DIM, base * (FOX_SCALE * LOG2E), tail).astype(BF16)
        m_sc[...] = jnp.full(m_sc.shape, -jnp.inf, F32)
        acc_sc[...] = jnp.zeros(acc_sc.shape, F32)
        causal = (lax.broadcasted_iota(jnp.int32, (tl, 1), 0)
                  >= lax.broadcasted_iota(jnp.int32, (1, tk), 1))
        mask_sc[...] = jnp.where(causal, 0.0, -jnp.inf)

    def k_aug(h):
        hi, mid, low = _split3(ft_ref[0, 0, h:h + 1, :] * (-LOG2E))
        aug = jnp.where(row == 0, hi, jnp.where(row == 1, mid, jnp.where(row == 2, low,
                        jnp.where(row < 6, 1.0, 0.0)))).astype(BF16)
        pad = jnp.zeros((FOX_HEAD_DIM - AUG_ROWS, tk), BF16)
        return jnp.concatenate([kt_ref[0, h], aug, pad], axis=0)

    def attend(masked):
        zk = jnp.zeros((LANES, tk), BF16)
        zv = jnp.zeros((tk, LANES), BF16)
        for c in range(FOX_HEADS // 2):
            h0, h1 = 2 * c, 2 * c + 1
            k_pair = jnp.concatenate([jnp.concatenate([k_aug(h0), zk], axis=1),
                                      jnp.concatenate([zk, k_aug(h1)], axis=1)], axis=0)
            v_pair = jnp.concatenate([jnp.concatenate([v_ref[0, h0], zv], axis=1),
                                      jnp.concatenate([zv, v_ref[0, h1]], axis=1)], axis=0)
            q_pair = jnp.concatenate([q_sc[h0], q_sc[h1]], axis=1)
            s = _dot(q_pair, k_pair)
            ps, alphas = [], []
            for h, sh in ((h0, s[:, :tk]), (h1, s[:, tk:])):
                if masked:
                    sh = sh + mask_sc[...]
                m_old = m_sc[h]
                m_new = jnp.maximum(m_old, jnp.max(sh, axis=-1, keepdims=True))
                ps.append(jnp.exp2(sh - m_new[:, 0:1]).astype(BF16))
                alphas.append(jnp.exp2(m_old - m_new))
                m_sc[h] = m_new
            o = _dot(jnp.concatenate(ps, axis=1), v_pair)
            acc_sc[h0] = alphas[0] * acc_sc[h0] + o[:, :LANES]
            acc_sc[h1] = alphas[1] * acc_sc[h1] + o[:, LANES:]

    @pl.when(kb < j)
    def _():
        attend(False)

    @pl.when(kb == j)
    def _():
        attend(True)
        for c in range(FOX_HEADS // 2):
            a0 = acc_sc[2 * c]
            a1 = acc_sc[2 * c + 1]
            o0 = a0 / pltpu.roll(a0, FOX_HEAD_DIM, 1)
            o1 = pltpu.roll(a1, FOX_HEAD_DIM, 1) / a1
            cat_ref[0, :, c * LANES:(c + 1) * LANES] = (
                jnp.where(lane < FOX_HEAD_DIM, o0, o1).astype(cat_ref.dtype))


def _mixer_fox_prompt(layer, x, kt_b, vt_b, f_t, f_q, mem_k, mem_v, W, *, tl):
    B, L, _ = x.shape
    nkb = L // tl
    ft4 = jnp.swapaxes(f_t.reshape(B, FOX_HEADS, nkb, tl), 1, 2)
    kt_spec = pl.BlockSpec((1, FOX_HEADS, FOX_HEAD_DIM, tl),
                           lambda b, j, kb: (b, 0, 0, jnp.minimum(kb, j)))
    v_spec = pl.BlockSpec((1, FOX_HEADS, tl, LANES),
                          lambda b, j, kb: (b, 0, jnp.minimum(kb, j), 0))
    return pl.pallas_call(
        _mixer_fox_prompt_kernel,
        grid=(B, nkb, nkb),
        in_specs=[
            pl.BlockSpec((1, tl, D_MODEL), lambda b, j, kb: (b, j, 0)),
            _layer_spec(layer, (1, D_MODEL)),
            _layer_spec(layer, (D_MODEL, MIX_WIDTH)),
            _layer_spec(layer, (1, MEM_HEAD_DIM)),
            _mem_spec(layer, 1),
            _mem_spec(layer, 1),
            _layer_spec(layer - N_A, (1, LANES)),
            kt_spec,
            v_spec,
            pl.BlockSpec((1, 1, FOX_HEADS, tl), lambda b, j, kb: (b, jnp.minimum(kb, j), 0, 0)),
            pl.BlockSpec((1, tl, FOX_HEADS), lambda b, j, kb: (b, j, 0)),
        ],
        out_specs=pl.BlockSpec((1, tl, MIX_WIDTH), lambda b, j, kb: (b, j, 0)),
        out_shape=jax.ShapeDtypeStruct((B, L, MIX_WIDTH), BF16),
        scratch_shapes=[
            pltpu.VMEM((FOX_HEADS, tl, LANES), BF16),
            pltpu.VMEM((FOX_HEADS, tl, LANES), F32),
            pltpu.VMEM((FOX_HEADS, tl, LANES), F32),
            pltpu.VMEM((tl, tl), F32),
        ],
        compiler_params=_params(3),
        name="mixer_fox_prompt",
    )(x, W["g_mix"], W["w_in"], W["q_norm_mem"], mem_k, mem_v, W["qnf_pair"], kt_b, vt_b, ft4, f_q)


def _mixer_fox_sample_kernel(x_ref, g_ref, w_in_ref, qn_ref, mk_ref, mv_ref, qnf_ref,
                             ktp_ref, vtp_ref, ktn_ref, vtn_ref, ft_ref, fq_ref, cat_ref):
    nb, tl, _ = x_ref.shape
    past = ktp_ref.shape[-1]
    z = _in_proj(x_ref, g_ref, w_in_ref)
    causal = (lax.broadcasted_iota(jnp.int32, (tl, 1), 0)
              >= lax.broadcasted_iota(jnp.int32, (1, tl), 1))
    for i in range(nb):
        zi = z[i * tl:(i + 1) * tl]
        qs = _rms_head64(zi[:, :FOX_WIDTH], qnf_ref[0])
        for h in range(FOX_HEADS):
            half = h % 2
            q = (qs[h // 2][:, half * FOX_HEAD_DIM:(half + 1) * FOX_HEAD_DIM] * FOX_SCALE).astype(BF16)
            fq = fq_ref[i, :, h:h + 1]
            s_p = (_dot(q, ktp_ref[i, h].astype(BF16)) + fq) - ft_ref[i, h:h + 1, 0:past]
            s_n = (_dot(q, ktn_ref[i, h].astype(BF16)) + fq) - ft_ref[i, h:h + 1, past:past + tl]
            s_n = jnp.where(causal, s_n, -jnp.inf)
            m = jnp.maximum(jnp.max(s_p, axis=-1, keepdims=True), jnp.max(s_n, axis=-1, keepdims=True))
            p_p = jnp.exp(s_p - m)
            p_n = jnp.exp(s_n - m)
            l = jnp.sum(p_p, axis=-1, keepdims=True) + jnp.sum(p_n, axis=-1, keepdims=True)
            o = (_dot_nt(p_p.astype(BF16), vtp_ref[i, h].astype(BF16))
                 + _dot_nt(p_n.astype(BF16), vtn_ref[i, h].astype(BF16)))
            cat_ref[i, :, h * FOX_HEAD_DIM:(h + 1) * FOX_HEAD_DIM] = (o * (1.0 / l)).astype(cat_ref.dtype)
        _mem_attend(zi[:, FOX_WIDTH:], qn_ref[0], mk_ref, mv_ref, i, cat_ref, FOX_WIDTH)


def _mixer_fox_sample(layer, x, kt_past, vt_past, kt_new, vt_new, f_t, f_q, mem_k, mem_v, W, *, nb):
    B, L, _ = x.shape
    past = kt_past.shape[-1]
    lk_pad = f_t.shape[-1]
    per_b = lambda *tail: pl.BlockSpec((nb,) + tail, lambda b: (b,) + (0,) * len(tail))
    return pl.pallas_call(
        _mixer_fox_sample_kernel,
        grid=(B // nb,),
        in_specs=[
            per_b(L, D_MODEL),
            _layer_spec(layer, (1, D_MODEL)),
            _layer_spec(layer, (D_MODEL, MIX_WIDTH)),
            _layer_spec(layer, (1, MEM_HEAD_DIM)),
            _mem_spec(layer, nb),
            _mem_spec(layer, nb),
            _layer_spec(layer - N_A, (1, LANES)),
            per_b(FOX_HEADS, FOX_HEAD_DIM, past),
            per_b(FOX_HEADS, FOX_HEAD_DIM, past),
            per_b(FOX_HEADS, FOX_HEAD_DIM, L),
            per_b(FOX_HEADS, FOX_HEAD_DIM, L),
            per_b(FOX_HEADS, lk_pad),
            per_b(L, FOX_HEADS),
        ],
        out_specs=per_b(L, MIX_WIDTH),
        out_shape=jax.ShapeDtypeStruct((B, L, MIX_WIDTH), BF16),
        compiler_params=_params(1),
        name="mixer_fox_sample",
    )(x, W["g_mix"], W["w_in"], W["q_norm_mem"], mem_k, mem_v, W["qnf_pair"],
      kt_past, vt_past, kt_new, vt_new, f_t, f_q)


def _out_ffn_kernel(x_ref, cat_ref, w_out_ref, g_ref, w_gu_ref, w_down_ref, y_ref):
    x1 = x_ref[...] + _dot(cat_ref[...], w_out_ref[0])
    xn = _rms(x1, g_ref[0]).astype(BF16)
    cw = D_FF // FFN_CHUNKS
    acc = x1
    for c in range(FFN_CHUNKS):
        gate = _dot(xn, w_gu_ref[0, :, c * cw:(c + 1) * cw])
        up = _dot(xn, w_gu_ref[0, :, D_FF + c * cw:D_FF + (c + 1) * cw])
        h = (gate * (1.0 / (1.0 + jnp.exp(-gate)))) * up
        acc = acc + _dot(h.astype(BF16), w_down_ref[0, c * cw:(c + 1) * cw, :])
    y_ref[...] = acc


def _out_ffn(layer, x2, cat2, W):
    T = x2.shape[0]
    return pl.pallas_call(
        _out_ffn_kernel,
        grid=(T // ROW_BLOCK,),
        in_specs=[
            pl.BlockSpec((ROW_BLOCK, D_MODEL), lambda r: (r, 0)),
            pl.BlockSpec((ROW_BLOCK, MIX_WIDTH), lambda r: (r, 0)),
            _layer_spec(layer, (MIX_WIDTH, D_MODEL)),
            _layer_spec(layer, (1, D_MODEL)),
            _layer_spec(layer, (D_MODEL, 2 * D_FF)),
            _layer_spec(layer, (D_FF, D_MODEL)),
        ],
        out_specs=pl.BlockSpec((ROW_BLOCK, D_MODEL), lambda r: (r, 0)),
        out_shape=jax.ShapeDtypeStruct((T, D_MODEL), F32),
        compiler_params=_params(1),
        name="out_ffn",
    )(x2, cat2, W["w_out"], W["g_ffn"], W["w_gu"], W["w_down"])


def _trunk(x, pos0, pool_hist, fox_past, mem_k, mem_v, W, *, nb, tl):
    B, L, _ = x.shape
    pool_states = []
    for i in range(DEPTH):
        if i < N_A:
            cat, state = _mixer_pool(i, x, pool_hist, mem_k, mem_v, W, pos0=pos0, nb=nb, tl=tl)
            pool_states.append(state[:, :, 1:, :])
        else:
            if i == N_A:
                kt_new, vt_new, lft_new, kt_b, vt_b = _kv_proj(x, W, nb=nb, tl=tl)
                if fox_past is None:
                    lft_all = lft_new
                else:
                    lft_all = jnp.concatenate([fox_past[2], lft_new], axis=2)
                lk = lft_all.shape[2]
                lk_pad = -(-lk // LANES) * LANES
                lf_t = jnp.pad(lft_all.reshape(B * FOX_HEADS, lk), ((0, 0), (0, lk_pad - lk)))
                f_t = _cumsum_lanes(lf_t).reshape(B, FOX_HEADS, lk_pad)
                f_q = jnp.swapaxes(f_t[:, :, lk - L:lk], 1, 2)
            if fox_past is None:
                cat = _mixer_fox_prompt(i, x, kt_b, vt_b, f_t, f_q, mem_k, mem_v, W, tl=tl)
            else:
                cat = _mixer_fox_sample(i, x, fox_past[0], fox_past[1], kt_new, vt_new, f_t, f_q,
                                        mem_k, mem_v, W, nb=2)
        x = _out_ffn(i, x.reshape(B * L, D_MODEL), cat.reshape(B * L, MIX_WIDTH), W
                     ).reshape(B, L, D_MODEL)
    fox_new = (jnp.transpose(kt_new, (0, 3, 1, 2)), jnp.transpose(vt_new, (0, 3, 1, 2)),
               jnp.swapaxes(lft_new, 1, 2))
    return x, jnp.concatenate(pool_states, axis=0), fox_new


def kernel(x_prompt, x_sample, state_pool, cache_fox_k, cache_fox_v, cache_fox_logf, cache_mem_k,
           cache_mem_v, mem_prompt, g_mix, w_in, w_out, q_norm_mem, g_mem, w_mem_kv, k_norm_mem,
           w_pool, pool_scale, q_norm_fox, g_kv, w_kv, k_norm_fox, b_f, g_ffn, w_gu, w_down):
    B, L, _ = x_prompt.shape
    SB, SL, _ = x_sample.shape
    W = dict(
        g_mix=g_mix.reshape(DEPTH, 1, D_MODEL), w_in=w_in.astype(BF16), w_out=w_out.astype(BF16),
        q_norm_mem=q_norm_mem.reshape(DEPTH, 1, MEM_HEAD_DIM), w_pool=w_pool.astype(BF16),
        pool_scale=pool_scale.reshape(N_A, 1, POOL_WIDTH),
        qnf_pair=jnp.tile(q_norm_fox, (1, 2)).reshape(DEPTH - N_A, 1, LANES),
        g_kv=g_kv.reshape(1, D_MODEL),
        w_kv_t=jnp.pad(w_kv.T, ((0, KV_ROWS - w_kv.shape[1]), (0, 0))).astype(BF16),
        w_v=w_kv[:, FOX_WIDTH:2 * FOX_WIDTH].astype(BF16),
        kn_col=k_norm_fox.reshape(1, FOX_HEAD_DIM, 1), b_f=b_f.reshape(FOX_HEADS, 1),
        g_ffn=g_ffn.reshape(DEPTH, 1, D_MODEL), w_gu=w_gu.astype(BF16), w_down=w_down.astype(BF16))

    mem_k_p, mem_v_p = _mem_kv(mem_prompt, g_mem.reshape(DEPTH, 1, D_MODEL), w_mem_kv.astype(BF16),
                               k_norm_mem.reshape(DEPTH, 1, MEM_HEAD_DIM))
    hist_p = jnp.zeros((N_A, B, HIST_ROWS, POOL_WIDTH), F32)
    y_p, pool_p, fox_p = _trunk(x_prompt, 0, hist_p, None, mem_k_p, mem_v_p, W, nb=1, tl=ROW_BLOCK)

    hist_s = jnp.pad(state_pool, ((0, 0), (0, 0), (1, 0), (0, 0)))
    past = (jnp.transpose(cache_fox_k, (0, 2, 3, 1)), jnp.transpose(cache_fox_v, (0, 2, 3, 1)),
            jnp.swapaxes(cache_fox_logf, 1, 2))
    y_s, pool_s, fox_s = _trunk(
        x_sample, PAST_LEN, hist_s, past, cache_mem_k.reshape(DEPTH, SB, MEM_ROWS, MEM_HEAD_DIM),
        cache_mem_v.reshape(DEPTH, SB, MEM_ROWS, MEM_HEAD_DIM), W, nb=ROW_BLOCK // SL, tl=SL)

    mem_shape = (DEPTH, B, MEM_TOKENS, MEM_HEADS, MEM_HEAD_DIM)
    return (y_p, y_s, pool_p, fox_p[0], fox_p[1], fox_p[2], mem_k_p.reshape(mem_shape),
            mem_v_p.reshape(mem_shape), pool_s, fox_s[0], fox_s[1], fox_s[2])
```

```python
import functools

import jax
import jax.numpy as jnp
from jax import lax
from jax.experimental import pallas as pl
from jax.experimental.pallas import tpu as pltpu

F32 = jnp.float32
BF16 = jnp.bfloat16

D_MODEL = 1024
DEPTH = 4
N_A = DEPTH // 2
PAST_LEN = 1024
POOL_WINDOWS = (2, 4, 8, 16)
POOL_GROUPS = len(POOL_WINDOWS)
POOL_WIDTH = D_MODEL // 2
POOL_GROUP_DIM = POOL_WIDTH // POOL_GROUPS
POOL_HIST = max(POOL_WINDOWS) - 1
HIST_ROWS = POOL_HIST + 1
FOX_HEAD_DIM = 64
FOX_WIDTH = D_MODEL // 2
FOX_HEADS = FOX_WIDTH // FOX_HEAD_DIM
MEM_TOKENS = 256
MEM_HEADS = 4
MEM_WIDTH = D_MODEL // 2
MEM_HEAD_DIM = MEM_WIDTH // MEM_HEADS
MIX_WIDTH = POOL_WIDTH + MEM_WIDTH
D_FF = ((8 * D_MODEL // 3 + 255) // 256) * 256
EPS = 1e-6
FOX_SCALE = FOX_HEAD_DIM ** -0.5
MEM_SCALE = MEM_HEAD_DIM ** -0.5
LOG2E = 1.4426950408889634
AUG_ROWS = 16

LANES = 128
ROW_BLOCK = 512
KV_ROWS = 2 * FOX_WIDTH + 16
FFN_CHUNKS = 2
VMEM_LIMIT = 56 * 1024 * 1024


def _dot(a, b):
    return jnp.dot(a, b, preferred_element_type=F32)


def _dot_nt(a, b):
    return lax.dot_general(a, b, (((1,), (1,)), ((), ())), preferred_element_type=F32)


def _rms(x, g):
    ms = jnp.mean(x * x, axis=-1, keepdims=True)
    return (x * lax.rsqrt(ms + EPS)) * g


def _rms_head64(x, g_pair):
    lo = lax.broadcasted_iota(jnp.int32, (1, LANES), 1) < FOX_HEAD_DIM
    outs = []
    for c in range(x.shape[-1] // LANES):
        xc = x[:, c * LANES:(c + 1) * LANES]
        sq = xc * xc
        s_lo = jnp.sum(jnp.where(lo, sq, 0.0), axis=-1, keepdims=True)
        s_hi = jnp.sum(jnp.where(lo, 0.0, sq), axis=-1, keepdims=True)
        ms = jnp.where(lo, s_lo, s_hi) * (1.0 / FOX_HEAD_DIM)
        outs.append((xc * lax.rsqrt(ms + EPS)) * g_pair)
    return outs


def _const_spec(shape):
    return pl.BlockSpec(shape, lambda *_: (0,) * len(shape), pipeline_mode=pl.Buffered(1))


def _layer_spec(layer, shape):
    return pl.BlockSpec((1,) + shape, lambda *_: (layer,) + (0,) * len(shape),
                        pipeline_mode=pl.Buffered(1))


MEM_ROWS = MEM_TOKENS * MEM_HEADS


def _mem_spec(layer, nb):
    return pl.BlockSpec((1, nb, MEM_ROWS, MEM_HEAD_DIM), lambda b, *_: (layer, b, 0, 0))


def _head_rows(h):
    return pl.ds(h, MEM_TOKENS, stride=MEM_HEADS)


def _params(n_grid, flags=None):
    return pltpu.CompilerParams(
        dimension_semantics=("arbitrary",) * n_grid, vmem_limit_bytes=VMEM_LIMIT, flags=flags)


def _mem_kv_kernel(mem_ref, g_ref, w_ref, kn_ref, k_ref, v_ref):
    nb = mem_ref.shape[0]
    x = mem_ref[...].reshape(nb * MEM_TOKENS, D_MODEL)
    kv = _dot(_rms(x, g_ref[0]).astype(BF16), w_ref[0])
    for h in range(MEM_HEADS):
        ks = slice(h * MEM_HEAD_DIM, (h + 1) * MEM_HEAD_DIM)
        vs = slice(MEM_WIDTH + h * MEM_HEAD_DIM, MEM_WIDTH + (h + 1) * MEM_HEAD_DIM)
        k_ref[0, :, _head_rows(h), :] = _rms(kv[:, ks], kn_ref[0]).reshape(nb, MEM_TOKENS, MEM_HEAD_DIM)
        v_ref[0, :, _head_rows(h), :] = kv[:, vs].reshape(nb, MEM_TOKENS, MEM_HEAD_DIM)


def _mem_kv(mem, g_mem, w_mem_kv, k_norm_mem):
    B = mem.shape[0]
    nb = 4
    out = jax.ShapeDtypeStruct((DEPTH, B, MEM_ROWS, MEM_HEAD_DIM), F32)
    out_spec = pl.BlockSpec((1, nb, MEM_ROWS, MEM_HEAD_DIM), lambda i, b: (i, b, 0, 0))
    return pl.pallas_call(
        _mem_kv_kernel,
        grid=(DEPTH, B // nb),
        in_specs=[
            pl.BlockSpec((nb, MEM_TOKENS, D_MODEL), lambda i, b: (b, 0, 0)),
            pl.BlockSpec((1, 1, D_MODEL), lambda i, b: (i, 0, 0)),
            pl.BlockSpec((1, D_MODEL, 2 * MEM_WIDTH), lambda i, b: (i, 0, 0)),
            pl.BlockSpec((1, 1, MEM_HEAD_DIM), lambda i, b: (i, 0, 0)),
        ],
        out_specs=[out_spec, out_spec],
        out_shape=[out, out],
        compiler_params=_params(2),
        name="mem_kv",
    )(mem, g_mem, w_mem_kv, k_norm_mem)


def _in_proj(x_ref, g_ref, w_ref):
    nb, tl, _ = x_ref.shape
    x = x_ref[...].reshape(nb * tl, D_MODEL)
    return _dot(_rms(x, g_ref[0]).astype(BF16), w_ref[0])


def _mem_attend(zq, qn, mk_ref, mv_ref, i, cat_ref, col0):
    for h in range(MEM_HEADS):
        sl = slice(h * MEM_HEAD_DIM, (h + 1) * MEM_HEAD_DIM)
        q = _rms(zq[:, sl], qn).astype(BF16)
        s = _dot_nt(q, mk_ref[0, i, _head_rows(h), :].astype(BF16)) * (MEM_SCALE * LOG2E)
        p = jnp.exp2(s - jnp.max(s, axis=-1, keepdims=True)).astype(BF16)
        v = mv_ref[0, i, _head_rows(h), :].astype(BF16)
        o = _dot(p, jnp.concatenate([v, jnp.ones_like(v)], axis=-1))
        o = o[:, :MEM_HEAD_DIM] / o[:, MEM_HEAD_DIM:]
        cat_ref[i, :, col0 + h * MEM_HEAD_DIM:col0 + (h + 1) * MEM_HEAD_DIM] = o.astype(cat_ref.dtype)


def _mem_attend_paired(zq, qn, mk_ref, mv_ref, i, cat_ref, col0):
    z = jnp.zeros((MEM_TOKENS, MEM_HEAD_DIM), BF16)
    one = jnp.ones((MEM_TOKENS, MEM_HEAD_DIM), BF16)
    for c in range(MEM_HEADS // 2):
        hs = (2 * c, 2 * c + 1)
        q = jnp.concatenate(
            [_rms(zq[:, h * MEM_HEAD_DIM:(h + 1) * MEM_HEAD_DIM], qn) for h in hs], axis=1)
        k0, k1 = (mk_ref[0, i, _head_rows(h), :].astype(BF16) for h in hs)
        v0, v1 = (mv_ref[0, i, _head_rows(h), :].astype(BF16) for h in hs)
        k_pair = jnp.concatenate([jnp.concatenate([k0, z], axis=1),
                                  jnp.concatenate([z, k1], axis=1)], axis=0)
        v_pair = jnp.concatenate([jnp.concatenate([v0, one, z, z], axis=1),
                                  jnp.concatenate([z, z, v1, one], axis=1)], axis=0)
        s = _dot_nt(q.astype(BF16), k_pair) * (MEM_SCALE * LOG2E)
        p = jnp.concatenate(
            [jnp.exp2(sh - jnp.max(sh, axis=-1, keepdims=True))
             for sh in (s[:, :MEM_TOKENS], s[:, MEM_TOKENS:])], axis=1).astype(BF16)
        o = _dot(p, v_pair)
        for n, h in enumerate(hs):
            oh = o[:, 2 * n * MEM_HEAD_DIM:(2 * n + 1) * MEM_HEAD_DIM]
            lh = o[:, (2 * n + 1) * MEM_HEAD_DIM:(2 * n + 2) * MEM_HEAD_DIM]
            cat_ref[i, :, col0 + h * MEM_HEAD_DIM:col0 + (h + 1) * MEM_HEAD_DIM] = (
                (oh / lh).astype(cat_ref.dtype))


def _mixer_pool_kernel(x_ref, g_ref, w_in_ref, qn_ref, mk_ref, mv_ref, hist_ref, wp_ref, ps_ref,
                       cat_ref, state_ref, ubuf, *, pos0):
    nb, tl, _ = x_ref.shape
    j = pl.program_id(1)
    z = _in_proj(x_ref, g_ref, w_in_ref)

    @pl.when(j == 0)
    def _():
        ubuf[:, 0:HIST_ROWS, :] = hist_ref[0]

    pos = pos0 + j * tl + lax.broadcasted_iota(jnp.int32, (tl, 1), 0)
    for i in range(nb):
        zi = z[i * tl:(i + 1) * tl]
        u = zi[:, :POOL_WIDTH]
        ubuf[i, HIST_ROWS:HIST_ROWS + tl, :] = u
        for g, w in enumerate(POOL_WINDOWS):
            sl = slice(g * POOL_GROUP_DIM, (g + 1) * POOL_GROUP_DIM)
            ug = u[:, sl]
            acc = ug
            for k in range(1, w):
                acc = acc + ubuf[i, HIST_ROWS - k:HIST_ROWS - k + tl, sl]
            cnt = jnp.minimum(pos + 1, w).astype(F32)
            d = acc / cnt - ug
            y = _dot(d.astype(BF16), wp_ref[0, g]) * ps_ref[0, :, sl]
            cat_ref[i, :, sl] = y.astype(cat_ref.dtype)
        _mem_attend(zi[:, POOL_WIDTH:], qn_ref[0], mk_ref, mv_ref, i, cat_ref, POOL_WIDTH)
        tail = ubuf[i, tl:tl + HIST_ROWS, :]
        state_ref[0, i] = tail
        ubuf[i, 0:HIST_ROWS, :] = tail


def _mixer_pool(layer, x, hist, mem_k, mem_v, W, *, pos0, nb, tl):
    B, L, _ = x.shape
    assert tl >= HIST_ROWS and L % tl == 0 and B % nb == 0
    hist_spec = pl.BlockSpec((1, nb, HIST_ROWS, POOL_WIDTH), lambda b, j: (layer, b, 0, 0))
    return pl.pallas_call(
        functools.partial(_mixer_pool_kernel, pos0=pos0),
        grid=(B // nb, L // tl),
        in_specs=[
            pl.BlockSpec((nb, tl, D_MODEL), lambda b, j: (b, j, 0)),
            _layer_spec(layer, (1, D_MODEL)),
            _layer_spec(layer, (D_MODEL, MIX_WIDTH)),
            _layer_spec(layer, (1, MEM_HEAD_DIM)),
            _mem_spec(layer, nb),
            _mem_spec(layer, nb),
            hist_spec,
            _layer_spec(layer, (POOL_GROUPS, POOL_GROUP_DIM, POOL_GROUP_DIM)),
            _layer_spec(layer, (1, POOL_WIDTH)),
        ],
        out_specs=[
            pl.BlockSpec((nb, tl, MIX_WIDTH), lambda b, j: (b, j, 0)),
            pl.BlockSpec((1, nb, HIST_ROWS, POOL_WIDTH), lambda b, j: (0, b, 0, 0)),
        ],
        out_shape=[
            jax.ShapeDtypeStruct((B, L, MIX_WIDTH), BF16),
            jax.ShapeDtypeStruct((1, B, HIST_ROWS, POOL_WIDTH), F32),
        ],
        scratch_shapes=[pltpu.VMEM((nb, HIST_ROWS + tl, POOL_WIDTH), F32)],
        compiler_params=_params(2),
        name="mixer_pool",
    )(x, W["g_mix"], W["w_in"], W["q_norm_mem"], mem_k, mem_v, hist, W["w_pool"], W["pool_scale"])


def _kv_proj_kernel(x_ref, g_ref, w_ref, wv_ref, kn_ref, bf_ref, k_ref, v_ref, lf_ref, kb_ref, vb_ref):
    nb, tl, _ = x_ref.shape
    rows = nb * tl
    x = x_ref[...].reshape(rows, D_MODEL)
    xn = _rms(x, g_ref[...]).astype(BF16)
    zt = _dot_nt(w_ref[...], xn)
    zv = _dot(xn, wv_ref[...])
    lo = lax.broadcasted_iota(jnp.int32, (1, LANES), 1) < FOX_HEAD_DIM
    for h in range(FOX_HEADS):
        pair = zv[:, (h // 2) * LANES:(h // 2 + 1) * LANES]
        base = pair if h % 2 == 0 else pltpu.roll(pair, FOX_HEAD_DIM, 1)
        vb_ref[:, h] = jnp.where(lo, base, 1.0).astype(BF16).reshape(nb, tl, LANES)
    k3 = zt[:FOX_WIDTH].reshape(FOX_HEADS, FOX_HEAD_DIM, rows)
    ms = jnp.mean(k3 * k3, axis=1, keepdims=True)
    k3 = (k3 * lax.rsqrt(ms + EPS)) * kn_ref[...]
    v3 = zt[FOX_WIDTH:2 * FOX_WIDTH].reshape(FOX_HEADS, FOX_HEAD_DIM, rows)
    t = -(zt[2 * FOX_WIDTH:2 * FOX_WIDTH + FOX_HEADS] + bf_ref[...])
    lf = -(jnp.maximum(t, 0.0) + jnp.log1p(jnp.exp(-jnp.abs(t))))
    for i in range(nb):
        cols = slice(i * tl, (i + 1) * tl)
        k_ref[i] = k3[:, :, cols]
        v_ref[i] = v3[:, :, cols]
        lf_ref[i] = lf[:, cols]
        kb_ref[i] = k3[:, :, cols].astype(BF16)


def _kv_proj(x, W, *, nb, tl):
    B, L, _ = x.shape
    hd = pl.BlockSpec((nb, FOX_HEADS, FOX_HEAD_DIM, tl), lambda b, j: (b, 0, 0, j))
    heads = jax.ShapeDtypeStruct((B, FOX_HEADS, FOX_HEAD_DIM, L), F32)
    return pl.pallas_call(
        _kv_proj_kernel,
        grid=(B // nb, L // tl),
        in_specs=[
            pl.BlockSpec((nb, tl, D_MODEL), lambda b, j: (b, j, 0)),
            _const_spec((1, D_MODEL)),
            _const_spec((KV_ROWS, D_MODEL)),
            _const_spec((D_MODEL, FOX_WIDTH)),
            _const_spec((1, FOX_HEAD_DIM, 1)),
            _const_spec((FOX_HEADS, 1)),
        ],
        out_specs=[hd, hd, pl.BlockSpec((nb, FOX_HEADS, tl), lambda b, j: (b, 0, j)), hd,
                   pl.BlockSpec((nb, FOX_HEADS, tl, LANES), lambda b, j: (b, 0, j, 0))],
        out_shape=[heads, heads, jax.ShapeDtypeStruct((B, FOX_HEADS, L), F32),
                   jax.ShapeDtypeStruct((B, FOX_HEADS, FOX_HEAD_DIM, L), BF16),
                   jax.ShapeDtypeStruct((B, FOX_HEADS, L, LANES), BF16)],
        compiler_params=_params(2),
        name="kv_proj",
    )(x, W["g_kv"], W["w_kv_t"], W["w_v"], W["kn_col"], W["b_f"])


def _cumsum_kernel(lf_ref, f_ref):
    rows, n = lf_ref.shape
    r = lax.broadcasted_iota(jnp.int32, (LANES, LANES), 0)
    c = lax.broadcasted_iota(jnp.int32, (LANES, LANES), 1)
    tri = jnp.where(r <= c, 1.0, 0.0).astype(BF16)
    carry = jnp.zeros((rows, 1), F32)
    for ch in range(n // LANES):
        x = lf_ref[:, ch * LANES:(ch + 1) * LANES]
        hi = x.astype(BF16)
        r1 = x - hi.astype(F32)
        mid = r1.astype(BF16)
        low = (r1 - mid.astype(F32)).astype(BF16)
        y = (_dot(hi, tri) + _dot(mid, tri)) + _dot(low, tri) + carry
        f_ref[:, ch * LANES:(ch + 1) * LANES] = y
        carry = y[:, LANES - 1:LANES]


def _cumsum_lanes(lf_t):
    return pl.pallas_call(
        _cumsum_kernel,
        out_shape=jax.ShapeDtypeStruct(lf_t.shape, F32),
        name="logf_cumsum",
    )(lf_t)


def _split3(x):
    hi = x.astype(BF16).astype(F32)
    r1 = x - hi
    mid = r1.astype(BF16).astype(F32)
    low = (r1 - mid).astype(BF16).astype(F32)
    return hi, mid, low


def _mixer_fox_prompt_kernel(x_ref, g_ref, w_in_ref, qn_ref, mk_ref, mv_ref, qnf_ref,
                             kt_ref, v_ref, ft_ref, fq_ref, cat_ref, q_sc, m_sc, acc_sc, mask_sc):
    _, tl, _ = x_ref.shape
    tk = kt_ref.shape[-1]
    j = pl.program_id(1)
    kb = pl.program_id(2)
    lane = lax.broadcasted_iota(jnp.int32, (1, LANES), 1)
    row = lax.broadcasted_iota(jnp.int32, (AUG_ROWS, 1), 0)

    @pl.when(kb == 0)
    def _():
        z = _in_proj(x_ref, g_ref, w_in_ref)
        _mem_attend_paired(z[:, FOX_WIDTH:], qn_ref[0], mk_ref, mv_ref, 0, cat_ref, FOX_WIDTH)
        qs = _rms_head64(z[:, :FOX_WIDTH], qnf_ref[0])
        for h in range(FOX_HEADS):
            base = qs[h // 2] if h % 2 == 0 else pltpu.roll(qs[h // 2], FOX_HEAD_DIM, 1)
            hi, mid, low = _split3(fq_ref[0, :, h:h + 1] * LOG2E)
            tail = jnp.where(lane < FOX_HEAD_DIM + 3, 1.0,
                             jnp.where(lane == FOX_HEAD_DIM + 3, hi,
                                       jnp.where(lane == FOX_HEAD_DIM + 4, mid,
                                                 jnp.where(lane == FOX_HEAD_DIM + 5, low, 0.0))))
            q_sc[h] = jnp.where(lane < FOX_HEAD_DIM, base * (FOX_SCALE * LOG2E), tail).astype(BF16)
        m_sc[...] = jnp.full(m_sc.shape, -jnp.inf, F32)
        acc_sc[...] = jnp.zeros(acc_sc.shape, F32)
        causal = (lax.broadcasted_iota(jnp.int32, (tl, 1), 0)
                  >= lax.broadcasted_iota(jnp.int32, (1, tk), 1))
        mask_sc[...] = jnp.where(causal, 0.0, -jnp.inf)

    def k_aug(h):
        hi, mid, low = _split3(ft_ref[0, 0, h:h + 1, :] * (-LOG2E))
        aug = jnp.where(row == 0, hi, jnp.where(row == 1, mid, jnp.where(row == 2, low,
                        jnp.where(row < 6, 1.0, 0.0)))).astype(BF16)
        pad = jnp.zeros((FOX_HEAD_DIM - AUG_ROWS, tk), BF16)
        return jnp.concatenate([kt_ref[0, h], aug, pad], axis=0)

    def attend(masked):
        zk = jnp.zeros((LANES, tk), BF16)
        zv = jnp.zeros((tk, LANES), BF16)
        for c in range(FOX_HEADS // 2):
            h0, h1 = 2 * c, 2 * c + 1
            k_pair = jnp.concatenate([jnp.concatenate([k_aug(h0), zk], axis=1),
                                      jnp.concatenate([zk, k_aug(h1)], axis=1)], axis=0)
            v_pair = jnp.concatenate([jnp.concatenate([v_ref[0, h0], zv], axis=1),
                                      jnp.concatenate([zv, v_ref[0, h1]], axis=1)], axis=0)
            q_pair = jnp.concatenate([q_sc[h0], q_sc[h1]], axis=1)
            s = _dot(q_pair, k_pair)
            ps, alphas = [], []
            for h, sh in ((h0, s[:, :tk]), (h1, s[:, tk:])):
                if masked:
                    sh = sh + mask_sc[...]
                m_old = m_sc[h]
                m_new = jnp.maximum(m_old, jnp.max(sh, axis=-1, keepdims=True))
                ps.append(jnp.exp2(sh - m_new[:, 0:1]).astype(BF16))
                alphas.append(jnp.exp2(m_old - m_new))
                m_sc[h] = m_new
            o = _dot(jnp.concatenate(ps, axis=1), v_pair)
            acc_sc[h0] = alphas[0] * acc_sc[h0] + o[:, :LANES]
            acc_sc[h1] = alphas[1] * acc_sc[h1] + o[:, LANES:]

    @pl.when(kb < j)
    def _():
        attend(False)

    @pl.when(kb == j)
    def _():
        attend(True)
        for c in range(FOX_HEADS // 2):
            a0 = acc_sc[2 * c]
            a1 = acc_sc[2 * c + 1]
            o0 = a0 / pltpu.roll(a0, FOX_HEAD_DIM, 1)
            o1 = pltpu.roll(a1, FOX_HEAD_DIM, 1) / a1
            cat_ref[0, :, c * LANES:(c + 1) * LANES] = (
                jnp.where(lane < FOX_HEAD_DIM, o0, o1).astype(cat_ref.dtype))


def _mixer_fox_prompt(layer, x, kt_b, vt_b, f_t, f_q, mem_k, mem_v, W, *, tl):
    B, L, _ = x.shape
    nkb = L // tl
    ft4 = jnp.swapaxes(f_t.reshape(B, FOX_HEADS, nkb, tl), 1, 2)
    kt_spec = pl.BlockSpec((1, FOX_HEADS, FOX_HEAD_DIM, tl),
                           lambda b, j, kb: (b, 0, 0, jnp.minimum(kb, j)))
    v_spec = pl.BlockSpec((1, FOX_HEADS, tl, LANES),
                          lambda b, j, kb: (b, 0, jnp.minimum(kb, j), 0))
    return pl.pallas_call(
        _mixer_fox_prompt_kernel,
        grid=(B, nkb, nkb),
        in_specs=[
            pl.BlockSpec((1, tl, D_MODEL), lambda b, j, kb: (b, j, 0)),
            _layer_spec(layer, (1, D_MODEL)),
            _layer_spec(layer, (D_MODEL, MIX_WIDTH)),
            _layer_spec(layer, (1, MEM_HEAD_DIM)),
            _mem_spec(layer, 1),
            _mem_spec(layer, 1),
            _layer_spec(layer - N_A, (1, LANES)),
            kt_spec,
            v_spec,
            pl.BlockSpec((1, 1, FOX_HEADS, tl), lambda b, j, kb: (b, jnp.minimum(kb, j), 0, 0)),
            pl.BlockSpec((1, tl, FOX_HEADS), lambda b, j, kb: (b, j, 0)),
        ],
        out_specs=pl.BlockSpec((1, tl, MIX_WIDTH), lambda b, j, kb: (b, j, 0)),
        out_shape=jax.ShapeDtypeStruct((B, L, MIX_WIDTH), BF16),
        scratch_shapes=[
            pltpu.VMEM((FOX_HEADS, tl, LANES), BF16),
            pltpu.VMEM((FOX_HEADS, tl, LANES), F32),
            pltpu.VMEM((FOX_HEADS, tl, LANES), F32),
            pltpu.VMEM((tl, tl), F32),
        ],
        compiler_params=_params(3),
        name="mixer_fox_prompt",
    )(x, W["g_mix"], W["w_in"], W["q_norm_mem"], mem_k, mem_v, W["qnf_pair"], kt_b, vt_b, ft4, f_q)


def _mixer_fox_sample_kernel(x_ref, g_ref, w_in_ref, qn_ref, mk_ref, mv_ref, qnf_ref,
                             ktp_ref, vtp_ref, ktn_ref, vtn_ref, ft_ref, fq_ref, cat_ref):
    nb, tl, _ = x_ref.shape
    past = ktp_ref.shape[-1]
    z = _in_proj(x_ref, g_ref, w_in_ref)
    causal = (lax.broadcasted_iota(jnp.int32, (tl, 1), 0)
              >= lax.broadcasted_iota(jnp.int32, (1, tl), 1))
    for i in range(nb):
        zi = z[i * tl:(i + 1) * tl]
        qs = _rms_head64(zi[:, :FOX_WIDTH], qnf_ref[0])
        for h in range(FOX_HEADS):
            half = h % 2
            q = (qs[h // 2][:, half * FOX_HEAD_DIM:(half + 1) * FOX_HEAD_DIM] * FOX_SCALE).astype(BF16)
            fq = fq_ref[i, :, h:h + 1]
            s_p = (_dot(q, ktp_ref[i, h].astype(BF16)) + fq) - ft_ref[i, h:h + 1, 0:past]
            s_n = (_dot(q, ktn_ref[i, h].astype(BF16)) + fq) - ft_ref[i, h:h + 1, past:past + tl]
            s_n = jnp.where(causal, s_n, -jnp.inf)
            m = jnp.maximum(jnp.max(s_p, axis=-1, keepdims=True), jnp.max(s_n, axis=-1, keepdims=True))
            p_p = jnp.exp(s_p - m)
            p_n = jnp.exp(s_n - m)
            l = jnp.sum(p_p, axis=-1, keepdims=True) + jnp.sum(p_n, axis=-1, keepdims=True)
            o = (_dot_nt(p_p.astype(BF16), vtp_ref[i, h].astype(BF16))
                 + _dot_nt(p_n.astype(BF16), vtn_ref[i, h].astype(BF16)))
            cat_ref[i, :, h * FOX_HEAD_DIM:(h + 1) * FOX_HEAD_DIM] = (o * (1.0 / l)).astype(cat_ref.dtype)
        _mem_attend(zi[:, FOX_WIDTH:], qn_ref[0], mk_ref, mv_ref, i, cat_ref, FOX_WIDTH)


def _mixer_fox_sample(layer, x, kt_past, vt_past, kt_new, vt_new, f_t, f_q, mem_k, mem_v, W, *, nb):
    B, L, _ = x.shape
    past = kt_past.shape[-1]
    lk_pad = f_t.shape[-1]
    per_b = lambda *tail: pl.BlockSpec((nb,) + tail, lambda b: (b,) + (0,) * len(tail))
    return pl.pallas_call(
        _mixer_fox_sample_kernel,
        grid=(B // nb,),
        in_specs=[
            per_b(L, D_MODEL),
            _layer_spec(layer, (1, D_MODEL)),
            _layer_spec(layer, (D_MODEL, MIX_WIDTH)),
            _layer_spec(layer, (1, MEM_HEAD_DIM)),
            _mem_spec(layer, nb),
            _mem_spec(layer, nb),
            _layer_spec(layer - N_A, (1, LANES)),
            per_b(FOX_HEADS, FOX_HEAD_DIM, past),
            per_b(FOX_HEADS, FOX_HEAD_DIM, past),
            per_b(FOX_HEADS, FOX_HEAD_DIM, L),
            per_b(FOX_HEADS, FOX_HEAD_DIM, L),
            per_b(FOX_HEADS, lk_pad),
            per_b(L, FOX_HEADS),
        ],
        out_specs=per_b(L, MIX_WIDTH),
        out_shape=jax.ShapeDtypeStruct((B, L, MIX_WIDTH), BF16),
        compiler_params=_params(1),
        name="mixer_fox_sample",
    )(x, W["g_mix"], W["w_in"], W["q_norm_mem"], mem_k, mem_v, W["qnf_pair"],
      kt_past, vt_past, kt_new, vt_new, f_t, f_q)


def _out_ffn_kernel(x_ref, cat_ref, w_out_ref, g_ref, w_gu_ref, w_down_ref, y_ref):
    x1 = x_ref[...] + _dot(cat_ref[...], w_out_ref[0])
    xn = _rms(x1, g_ref[0]).astype(BF16)
    cw = D_FF // FFN_CHUNKS
    acc = x1
    for c in range(FFN_CHUNKS):
        gate = _dot(xn, w_gu_ref[0, :, c * cw:(c + 1) * cw])
        up = _dot(xn, w_gu_ref[0, :, D_FF + c * cw:D_FF + (c + 1) * cw])
        h = (gate * (1.0 / (1.0 + jnp.exp(-gate)))) * up
        acc = acc + _dot(h.astype(BF16), w_down_ref[0, c * cw:(c + 1) * cw, :])
    y_ref[...] = acc


def _out_ffn(layer, x2, cat2, W):
    T = x2.shape[0]
    return pl.pallas_call(
        _out_ffn_kernel,
        grid=(T // ROW_BLOCK,),
        in_specs=[
            pl.BlockSpec((ROW_BLOCK, D_MODEL), lambda r: (r, 0)),
            pl.BlockSpec((ROW_BLOCK, MIX_WIDTH), lambda r: (r, 0)),
            _layer_spec(layer, (MIX_WIDTH, D_MODEL)),
            _layer_spec(layer, (1, D_MODEL)),
            _layer_spec(layer, (D_MODEL, 2 * D_FF)),
            _layer_spec(layer, (D_FF, D_MODEL)),
        ],
        out_specs=pl.BlockSpec((ROW_BLOCK, D_MODEL), lambda r: (r, 0)),
        out_shape=jax.ShapeDtypeStruct((T, D_MODEL), F32),
        compiler_params=_params(1),
        name="out_ffn",
    )(x2, cat2, W["w_out"], W["g_ffn"], W["w_gu"], W["w_down"])


def _trunk(x, pos0, pool_hist, fox_past, mem_k, mem_v, W, *, nb, tl):
    B, L, _ = x.shape
    pool_states = []
    for i in range(DEPTH):
        if i < N_A:
            cat, state = _mixer_pool(i, x, pool_hist, mem_k, mem_v, W, pos0=pos0, nb=nb, tl=tl)
            pool_states.append(state[:, :, 1:, :])
        else:
            if i == N_A:
                kt_new, vt_new, lft_new, kt_b, vt_b = _kv_proj(x, W, nb=nb, tl=tl)
                if fox_past is None:
                    lft_all = lft_new
                else:
                    lft_all = jnp.concatenate([fox_past[2], lft_new], axis=2)
                lk = lft_all.shape[2]
                lk_pad = -(-lk // LANES) * LANES
                lf_t = jnp.pad(lft_all.reshape(B * FOX_HEADS, lk), ((0, 0), (0, lk_pad - lk)))
                f_t = _cumsum_lanes(lf_t).reshape(B, FOX_HEADS, lk_pad)
                f_q = jnp.swapaxes(f_t[:, :, lk - L:lk], 1, 2)
            if fox_past is None:
                cat = _mixer_fox_prompt(i, x, kt_b, vt_b, f_t, f_q, mem_k, mem_v, W, tl=tl)
            else:
                cat = _mixer_fox_sample(i, x, fox_past[0], fox_past[1], kt_new, vt_new, f_t, f_q,
                                        mem_k, mem_v, W, nb=2)
        x = _out_ffn(i, x.reshape(B * L, D_MODEL), cat.reshape(B * L, MIX_WIDTH), W
                     ).reshape(B, L, D_MODEL)
    fox_new = (jnp.transpose(kt_new, (0, 3, 1, 2)), jnp.transpose(vt_new, (0, 3, 1, 2)),
               jnp.swapaxes(lft_new, 1, 2))
    return x, jnp.concatenate(pool_states, axis=0), fox_new


def kernel(x_prompt, x_sample, state_pool, cache_fox_k, cache_fox_v, cache_fox_logf, cache_mem_k,
           cache_mem_v, mem_prompt, g_mix, w_in, w_out, q_norm_mem, g_mem, w_mem_kv, k_norm_mem,
           w_pool, pool_scale, q_norm_fox, g_kv, w_kv, k_norm_fox, b_f, g_ffn, w_gu, w_down):
    B, L, _ = x_prompt.shape
    SB, SL, _ = x_sample.shape
    W = dict(
        g_mix=g_mix.reshape(DEPTH, 1, D_MODEL), w_in=w_in.astype(BF16), w_out=w_out.astype(BF16),
        q_norm_mem=q_norm_mem.reshape(DEPTH, 1, MEM_HEAD_DIM), w_pool=w_pool.astype(BF16),
        pool_scale=pool_scale.reshape(N_A, 1, POOL_WIDTH),
        qnf_pair=jnp.tile(q_norm_fox, (1, 2)).reshape(DEPTH - N_A, 1, LANES),
        g_kv=g_kv.reshape(1, D_MODEL),
        w_kv_t=jnp.pad(w_kv.T, ((0, KV_ROWS - w_kv.shape[1]), (0, 0))).astype(BF16),
        w_v=w_kv[:, FOX_WIDTH:2 * FOX_WIDTH].astype(BF16),
        kn_col=k_norm_fox.reshape(1, FOX_HEAD_DIM, 1), b_f=b_f.reshape(FOX_HEADS, 1),
        g_ffn=g_ffn.reshape(DEPTH, 1, D_MODEL), w_gu=w_gu.astype(BF16), w_down=w_down.astype(BF16))

    mem_k_p, mem_v_p = _mem_kv(mem_prompt, g_mem.reshape(DEPTH, 1, D_MODEL), w_mem_kv.astype(BF16),
                               k_norm_mem.reshape(DEPTH, 1, MEM_HEAD_DIM))
    hist_p = jnp.zeros((N_A, B, HIST_ROWS, POOL_WIDTH), F32)
    y_p, pool_p, fox_p = _trunk(x_prompt, 0, hist_p, None, mem_k_p, mem_v_p, W, nb=1, tl=ROW_BLOCK)

    hist_s = jnp.pad(state_pool, ((0, 0), (0, 0), (1, 0), (0, 0)))
    past = (jnp.transpose(cache_fox_k, (0, 2, 3, 1)), jnp.transpose(cache_fox_v, (0, 2, 3, 1)),
            jnp.swapaxes(cache_fox_logf, 1, 2))
    y_s, pool_s, fox_s = _trunk(
        x_sample, PAST_LEN, hist_s, past, cache_mem_k.reshape(DEPTH, SB, MEM_ROWS, MEM_HEAD_DIM),
        cache_mem_v.reshape(DEPTH, SB, MEM_ROWS, MEM_HEAD_DIM), W, nb=ROW_BLOCK // SL, tl=SL)

    mem_shape = (DEPTH, B, MEM_TOKENS, MEM_HEADS, MEM_HEAD_DIM)
    return (y_p, y_s, pool_p, fox_p[0], fox_p[1], fox_p[2], mem_k_p.reshape(mem_shape),
            mem_v_p.reshape(mem_shape), pool_s, fox_s[0], fox_s[1], fox_s[2])
```

```python
import functools

import jax
import jax.numpy as jnp
from jax import lax
from jax.experimental import pallas as pl
from jax.experimental.pallas import tpu as pltpu

F32 = jnp.float32
BF16 = jnp.bfloat16

D_MODEL = 1024
DEPTH = 4
N_A = DEPTH // 2
PAST_LEN = 1024
POOL_WINDOWS = (2, 4, 8, 16)
POOL_GROUPS = len(POOL_WINDOWS)
POOL_WIDTH = D_MODEL // 2
POOL_GROUP_DIM = POOL_WIDTH // POOL_GROUPS
POOL_HIST = max(POOL_WINDOWS) - 1
HIST_ROWS = POOL_HIST + 1
FOX_HEAD_DIM = 64
FOX_WIDTH = D_MODEL // 2
FOX_HEADS = FOX_WIDTH // FOX_HEAD_DIM
MEM_TOKENS = 256
MEM_HEADS = 4
MEM_WIDTH = D_MODEL // 2
MEM_HEAD_DIM = MEM_WIDTH // MEM_HEADS
MIX_WIDTH = POOL_WIDTH + MEM_WIDTH
D_FF = ((8 * D_MODEL // 3 + 255) // 256) * 256
EPS = 1e-6
FOX_SCALE = FOX_HEAD_DIM ** -0.5
MEM_SCALE = MEM_HEAD_DIM ** -0.5
LOG2E = 1.4426950408889634
AUG_ROWS = 16

LANES = 128
ROW_BLOCK = 512
KV_ROWS = 2 * FOX_WIDTH + 16
MXU_DIM = 256
FFN_SPLITS = ((0, 6 * MXU_DIM), (6 * MXU_DIM, D_FF))
VMEM_LIMIT = 56 * 1024 * 1024


def _dot(a, b):
    return jnp.dot(a, b, preferred_element_type=F32)


def _dot_nt(a, b):
    return lax.dot_general(a, b, (((1,), (1,)), ((), ())), preferred_element_type=F32)


def _rms(x, g):
    ms = jnp.mean(x * x, axis=-1, keepdims=True)
    return (x * lax.rsqrt(ms + EPS)) * g


def _rms_head64(x, g_pair):
    lo = lax.broadcasted_iota(jnp.int32, (1, LANES), 1) < FOX_HEAD_DIM
    outs = []
    for c in range(x.shape[-1] // LANES):
        xc = x[:, c * LANES:(c + 1) * LANES]
        sq = xc * xc
        s_lo = jnp.sum(jnp.where(lo, sq, 0.0), axis=-1, keepdims=True)
        s_hi = jnp.sum(jnp.where(lo, 0.0, sq), axis=-1, keepdims=True)
        ms = jnp.where(lo, s_lo, s_hi) * (1.0 / FOX_HEAD_DIM)
        outs.append((xc * lax.rsqrt(ms + EPS)) * g_pair)
    return outs


def _const_spec(shape):
    return pl.BlockSpec(shape, lambda *_: (0,) * len(shape), pipeline_mode=pl.Buffered(1))


def _layer_spec(layer, shape):
    return pl.BlockSpec((1,) + shape, lambda *_: (layer,) + (0,) * len(shape),
                        pipeline_mode=pl.Buffered(1))


MEM_ROWS = MEM_TOKENS * MEM_HEADS


def _mem_spec(layer, nb):
    return pl.BlockSpec((1, nb, MEM_ROWS, MEM_HEAD_DIM), lambda b, *_: (layer, b, 0, 0))


def _head_rows(h):
    return pl.ds(h, MEM_TOKENS, stride=MEM_HEADS)


def _params(n_grid, flags=None):
    return pltpu.CompilerParams(
        dimension_semantics=("arbitrary",) * n_grid, vmem_limit_bytes=VMEM_LIMIT, flags=flags)


def _mem_kv_kernel(mem_ref, g_ref, w_ref, kn_ref, k_ref, v_ref):
    nb = mem_ref.shape[0]
    x = mem_ref[...].reshape(nb * MEM_TOKENS, D_MODEL)
    kv = _dot(_rms(x, g_ref[0]).astype(BF16), w_ref[0])
    for h in range(MEM_HEADS):
        ks = slice(h * MEM_HEAD_DIM, (h + 1) * MEM_HEAD_DIM)
        vs = slice(MEM_WIDTH + h * MEM_HEAD_DIM, MEM_WIDTH + (h + 1) * MEM_HEAD_DIM)
        k_ref[0, :, _head_rows(h), :] = _rms(kv[:, ks], kn_ref[0]).reshape(nb, MEM_TOKENS, MEM_HEAD_DIM)
        v_ref[0, :, _head_rows(h), :] = kv[:, vs].reshape(nb, MEM_TOKENS, MEM_HEAD_DIM)


def _mem_kv(mem, g_mem, w_mem_kv, k_norm_mem):
    B = mem.shape[0]
    nb = 4
    out = jax.ShapeDtypeStruct((DEPTH, B, MEM_ROWS, MEM_HEAD_DIM), F32)
    out_spec = pl.BlockSpec((1, nb, MEM_ROWS, MEM_HEAD_DIM), lambda i, b: (i, b, 0, 0))
    return pl.pallas_call(
        _mem_kv_kernel,
        grid=(DEPTH, B // nb),
        in_specs=[
            pl.BlockSpec((nb, MEM_TOKENS, D_MODEL), lambda i, b: (b, 0, 0)),
            pl.BlockSpec((1, 1, D_MODEL), lambda i, b: (i, 0, 0)),
            pl.BlockSpec((1, D_MODEL, 2 * MEM_WIDTH), lambda i, b: (i, 0, 0)),
            pl.BlockSpec((1, 1, MEM_HEAD_DIM), lambda i, b: (i, 0, 0)),
        ],
        out_specs=[out_spec, out_spec],
        out_shape=[out, out],
        compiler_params=_params(2),
        name="mem_kv",
    )(mem, g_mem, w_mem_kv, k_norm_mem)


def _in_proj(x_ref, g_ref, w_ref):
    nb, tl, _ = x_ref.shape
    x = x_ref[...].reshape(nb * tl, D_MODEL)
    return _dot(_rms(x, g_ref[0]).astype(BF16), w_ref[0])


def _mem_attend(zq, qn, mk_ref, mv_ref, i, cat_ref, col0):
    for h in range(MEM_HEADS):
        sl = slice(h * MEM_HEAD_DIM, (h + 1) * MEM_HEAD_DIM)
        q = _rms(zq[:, sl], qn).astype(BF16)
        s = _dot_nt(q, mk_ref[0, i, _head_rows(h), :].astype(BF16)) * (MEM_SCALE * LOG2E)
        p = jnp.exp2(s - jnp.max(s, axis=-1, keepdims=True)).astype(BF16)
        v = mv_ref[0, i, _head_rows(h), :].astype(BF16)
        o = _dot(p, jnp.concatenate([v, jnp.ones_like(v)], axis=-1))
        o = o[:, :MEM_HEAD_DIM] / o[:, MEM_HEAD_DIM:]
        cat_ref[i, :, col0 + h * MEM_HEAD_DIM:col0 + (h + 1) * MEM_HEAD_DIM] = o.astype(cat_ref.dtype)


def _mem_attend_paired(zq, qn, mk_ref, mv_ref, i, cat_ref, col0):
    z = jnp.zeros((MEM_TOKENS, MEM_HEAD_DIM), BF16)
    one = jnp.ones((MEM_TOKENS, MEM_HEAD_DIM), BF16)
    for c in range(MEM_HEADS // 2):
        hs = (2 * c, 2 * c + 1)
        q = jnp.concatenate(
            [_rms(zq[:, h * MEM_HEAD_DIM:(h + 1) * MEM_HEAD_DIM], qn) for h in hs], axis=1)
        k0, k1 = (mk_ref[0, i, _head_rows(h), :].astype(BF16) for h in hs)
        v0, v1 = (mv_ref[0, i, _head_rows(h), :].astype(BF16) for h in hs)
        k_pair = jnp.concatenate([jnp.concatenate([k0, z], axis=1),
                                  jnp.concatenate([z, k1], axis=1)], axis=0)
        v_pair = jnp.concatenate([jnp.concatenate([v0, one, z, z], axis=1),
                                  jnp.concatenate([z, z, v1, one], axis=1)], axis=0)
        s = _dot_nt(q.astype(BF16), k_pair) * (MEM_SCALE * LOG2E)
        p = jnp.concatenate(
            [jnp.exp2(sh - jnp.max(sh, axis=-1, keepdims=True))
             for sh in (s[:, :MEM_TOKENS], s[:, MEM_TOKENS:])], axis=1).astype(BF16)
        o = _dot(p, v_pair)
        for n, h in enumerate(hs):
            oh = o[:, 2 * n * MEM_HEAD_DIM:(2 * n + 1) * MEM_HEAD_DIM]
            lh = o[:, (2 * n + 1) * MEM_HEAD_DIM:(2 * n + 2) * MEM_HEAD_DIM]
            cat_ref[i, :, col0 + h * MEM_HEAD_DIM:col0 + (h + 1) * MEM_HEAD_DIM] = (
                (oh / lh).astype(cat_ref.dtype))


def _mixer_pool_kernel(x_ref, g_ref, w_in_ref, qn_ref, mk_ref, mv_ref, hist_ref, wp_ref, ps_ref,
                       cat_ref, state_ref, ubuf, *, pos0):
    nb, tl, _ = x_ref.shape
    j = pl.program_id(1)
    z = _in_proj(x_ref, g_ref, w_in_ref)

    @pl.when(j == 0)
    def _():
        ubuf[:, 0:HIST_ROWS, :] = hist_ref[0]

    pos = pos0 + j * tl + lax.broadcasted_iota(jnp.int32, (tl, 1), 0)
    for i in range(nb):
        zi = z[i * tl:(i + 1) * tl]
        u = zi[:, :POOL_WIDTH]
        ubuf[i, HIST_ROWS:HIST_ROWS + tl, :] = u
        for g, w in enumerate(POOL_WINDOWS):
            sl = slice(g * POOL_GROUP_DIM, (g + 1) * POOL_GROUP_DIM)
            ug = u[:, sl]
            acc = ug
            for k in range(1, w):
                acc = acc + ubuf[i, HIST_ROWS - k:HIST_ROWS - k + tl, sl]
            cnt = jnp.minimum(pos + 1, w).astype(F32)
            d = acc / cnt - ug
            y = _dot(d.astype(BF16), wp_ref[0, g]) * ps_ref[0, :, sl]
            cat_ref[i, :, sl] = y.astype(cat_ref.dtype)
        _mem_attend(zi[:, POOL_WIDTH:], qn_ref[0], mk_ref, mv_ref, i, cat_ref, POOL_WIDTH)
        tail = ubuf[i, tl:tl + HIST_ROWS, :]
        state_ref[0, i] = tail
        ubuf[i, 0:HIST_ROWS, :] = tail


def _mixer_pool(layer, x, hist, mem_k, mem_v, W, *, pos0, nb, tl):
    B, L, _ = x.shape
    assert tl >= HIST_ROWS and L % tl == 0 and B % nb == 0
    hist_spec = pl.BlockSpec((1, nb, HIST_ROWS, POOL_WIDTH), lambda b, j: (layer, b, 0, 0))
    return pl.pallas_call(
        functools.partial(_mixer_pool_kernel, pos0=pos0),
        grid=(B // nb, L // tl),
        in_specs=[
            pl.BlockSpec((nb, tl, D_MODEL), lambda b, j: (b, j, 0)),
            _layer_spec(layer, (1, D_MODEL)),
            _layer_spec(layer, (D_MODEL, MIX_WIDTH)),
            _layer_spec(layer, (1, MEM_HEAD_DIM)),
            _mem_spec(layer, nb),
            _mem_spec(layer, nb),
            hist_spec,
            _layer_spec(layer, (POOL_GROUPS, POOL_GROUP_DIM, POOL_GROUP_DIM)),
            _layer_spec(layer, (1, POOL_WIDTH)),
        ],
        out_specs=[
            pl.BlockSpec((nb, tl, MIX_WIDTH), lambda b, j: (b, j, 0)),
            pl.BlockSpec((1, nb, HIST_ROWS, POOL_WIDTH), lambda b, j: (0, b, 0, 0)),
        ],
        out_shape=[
            jax.ShapeDtypeStruct((B, L, MIX_WIDTH), BF16),
            jax.ShapeDtypeStruct((1, B, HIST_ROWS, POOL_WIDTH), F32),
        ],
        scratch_shapes=[pltpu.VMEM((nb, HIST_ROWS + tl, POOL_WIDTH), F32)],
        compiler_params=_params(2),
        name="mixer_pool",
    )(x, W["g_mix"], W["w_in"], W["q_norm_mem"], mem_k, mem_v, hist, W["w_pool"], W["pool_scale"])


def _kv_proj_kernel(x_ref, g_ref, w_ref, wv_ref, kn_ref, bf_ref, k_ref, v_ref, lf_ref, kb_ref, vb_ref):
    nb, tl, _ = x_ref.shape
    rows = nb * tl
    x = x_ref[...].reshape(rows, D_MODEL)
    xn = _rms(x, g_ref[...]).astype(BF16)
    zt = _dot_nt(w_ref[...], xn)
    zv = _dot(xn, wv_ref[...])
    lo = lax.broadcasted_iota(jnp.int32, (1, LANES), 1) < FOX_HEAD_DIM
    for h in range(FOX_HEADS):
        pair = zv[:, (h // 2) * LANES:(h // 2 + 1) * LANES]
        base = pair if h % 2 == 0 else pltpu.roll(pair, FOX_HEAD_DIM, 1)
        vb_ref[:, h] = jnp.where(lo, base, 1.0).astype(BF16).reshape(nb, tl, LANES)
    k3 = zt[:FOX_WIDTH].reshape(FOX_HEADS, FOX_HEAD_DIM, rows)
    ms = jnp.mean(k3 * k3, axis=1, keepdims=True)
    k3 = (k3 * lax.rsqrt(ms + EPS)) * kn_ref[...]
    v3 = zt[FOX_WIDTH:2 * FOX_WIDTH].reshape(FOX_HEADS, FOX_HEAD_DIM, rows)
    t = -(zt[2 * FOX_WIDTH:2 * FOX_WIDTH + FOX_HEADS] + bf_ref[...])
    lf = -(jnp.maximum(t, 0.0) + jnp.log1p(jnp.exp(-jnp.abs(t))))
    for i in range(nb):
        cols = slice(i * tl, (i + 1) * tl)
        k_ref[i] = k3[:, :, cols]
        v_ref[i] = v3[:, :, cols]
        lf_ref[i] = lf[:, cols]
        kb_ref[i] = k3[:, :, cols].astype(BF16)


def _kv_proj(x, W, *, nb, tl):
    B, L, _ = x.shape
    hd = pl.BlockSpec((nb, FOX_HEADS, FOX_HEAD_DIM, tl), lambda b, j: (b, 0, 0, j))
    heads = jax.ShapeDtypeStruct((B, FOX_HEADS, FOX_HEAD_DIM, L), F32)
    return pl.pallas_call(
        _kv_proj_kernel,
        grid=(B // nb, L // tl),
        in_specs=[
            pl.BlockSpec((nb, tl, D_MODEL), lambda b, j: (b, j, 0)),
            _const_spec((1, D_MODEL)),
            _const_spec((KV_ROWS, D_MODEL)),
            _const_spec((D_MODEL, FOX_WIDTH)),
            _const_spec((1, FOX_HEAD_DIM, 1)),
            _const_spec((FOX_HEADS, 1)),
        ],
        out_specs=[hd, hd, pl.BlockSpec((nb, FOX_HEADS, tl), lambda b, j: (b, 0, j)), hd,
                   pl.BlockSpec((nb, FOX_HEADS, tl, LANES), lambda b, j: (b, 0, j, 0))],
        out_shape=[heads, heads, jax.ShapeDtypeStruct((B, FOX_HEADS, L), F32),
                   jax.ShapeDtypeStruct((B, FOX_HEADS, FOX_HEAD_DIM, L), BF16),
                   jax.ShapeDtypeStruct((B, FOX_HEADS, L, LANES), BF16)],
        compiler_params=_params(2),
        name="kv_proj",
    )(x, W["g_kv"], W["w_kv_t"], W["w_v"], W["kn_col"], W["b_f"])


def _cumsum_kernel(lf_ref, f_ref):
    rows, n = lf_ref.shape
    r = lax.broadcasted_iota(jnp.int32, (LANES, LANES), 0)
    c = lax.broadcasted_iota(jnp.int32, (LANES, LANES), 1)
    tri = jnp.where(r <= c, 1.0, 0.0).astype(BF16)
    carry = jnp.zeros((rows, 1), F32)
    for ch in range(n // LANES):
        x = lf_ref[:, ch * LANES:(ch + 1) * LANES]
        hi = x.astype(BF16)
        r1 = x - hi.astype(F32)
        mid = r1.astype(BF16)
        low = (r1 - mid.astype(F32)).astype(BF16)
        y = (_dot(hi, tri) + _dot(mid, tri)) + _dot(low, tri) + carry
        f_ref[:, ch * LANES:(ch + 1) * LANES] = y
        carry = y[:, LANES - 1:LANES]


def _cumsum_lanes(lf_t):
    return pl.pallas_call(
        _cumsum_kernel,
        out_shape=jax.ShapeDtypeStruct(lf_t.shape, F32),
        name="logf_cumsum",
    )(lf_t)


def _tri_unrank(t, n):
    row = sum((t >= k * (k + 1) // 2).astype(jnp.int32) for k in range(1, n))
    return row, t - row * (row + 1) // 2


def _split3(x):
    hi = x.astype(BF16).astype(F32)
    r1 = x - hi
    mid = r1.astype(BF16).astype(F32)
    low = (r1 - mid).astype(BF16).astype(F32)
    return hi, mid, low


def _mixer_fox_prompt_kernel(x_ref, g_ref, w_in_ref, qn_ref, mk_ref, mv_ref, qnf_ref,
                             kt_ref, v_ref, ft_ref, fq_ref, cat_ref, q_sc, m_sc, acc_sc, mask_sc,
                             *, nkb):
    _, tl, _ = x_ref.shape
    tk = kt_ref.shape[-1]
    j, kb = _tri_unrank(pl.program_id(1), nkb)
    lane = lax.broadcasted_iota(jnp.int32, (1, LANES), 1)
    row = lax.broadcasted_iota(jnp.int32, (AUG_ROWS, 1), 0)

    @pl.when(kb == 0)
    def _():
        z = _in_proj(x_ref, g_ref, w_in_ref)
        _mem_attend_paired(z[:, FOX_WIDTH:], qn_ref[0], mk_ref, mv_ref, 0, cat_ref, FOX_WIDTH)
        qs = _rms_head64(z[:, :FOX_WIDTH], qnf_ref[0])
        for h in range(FOX_HEADS):
            base = qs[h // 2] if h % 2 == 0 else pltpu.roll(qs[h // 2], FOX_HEAD_DIM, 1)
            hi, mid, low = _split3(fq_ref[0, :, h:h + 1] * LOG2E)
            tail = jnp.where(lane < FOX_HEAD_DIM + 3, 1.0,
                             jnp.where(lane == FOX_HEAD_DIM + 3, hi,
                                       jnp.where(lane == FOX_HEAD_DIM + 4, mid,
                                                 jnp.where(lane == FOX_HEAD_DIM + 5, low, 0.0))))
            q_sc[h] = jnp.where(lane < FOX_HEAD_DIM, base * (FOX_SCALE * LOG2E), tail).astype(BF16)
        m_sc[...] = jnp.full(m_sc.shape, -jnp.inf, F32)
        acc_sc[...] = jnp.zeros(acc_sc.shape, F32)
        causal = (lax.broadcasted_iota(jnp.int32, (tl, 1), 0)
                  >= lax.broadcasted_iota(jnp.int32, (1, tk), 1))
        mask_sc[...] = jnp.where(causal, 0.0, -jnp.inf)

    def k_aug(h):
        hi, mid, low = _split3(ft_ref[0, 0, h:h + 1, :] * (-LOG2E))
        aug = jnp.where(row == 0, hi, jnp.where(row == 1, mid, jnp.where(row == 2, low,
                        jnp.where(row < 6, 1.0, 0.0)))).astype(BF16)
        pad = jnp.zeros((FOX_HEAD_DIM - AUG_ROWS, tk), BF16)
        return jnp.concatenate([kt_ref[0, h], aug, pad], axis=0)

    def attend(masked):
        zk = jnp.zeros((LANES, tk), BF16)
        zv = jnp.zeros((tk, LANES), BF16)
        for c in range(FOX_HEADS // 2):
            h0, h1 = 2 * c, 2 * c + 1
            k_pair = jnp.concatenate([jnp.concatenate([k_aug(h0), zk], axis=1),
                                      jnp.concatenate([zk, k_aug(h1)], axis=1)], axis=0)
            v_pair = jnp.concatenate([jnp.concatenate([v_ref[0, h0], zv], axis=1),
                                      jnp.concatenate([zv, v_ref[0, h1]], axis=1)], axis=0)
            q_pair = jnp.concatenate([q_sc[h0], q_sc[h1]], axis=1)
            s = _dot(q_pair, k_pair)
            ps, alphas = [], []
            for h, sh in ((h0, s[:, :tk]), (h1, s[:, tk:])):
                if masked:
                    sh = sh + mask_sc[...]
                m_old = m_sc[h]
                m_new = jnp.maximum(m_old, jnp.max(sh, axis=-1, keepdims=True))
                ps.append(jnp.exp2(sh - m_new[:, 0:1]).astype(BF16))
                alphas.append(jnp.exp2(m_old - m_new))
                m_sc[h] = m_new
            o = _dot(jnp.concatenate(ps, axis=1), v_pair)
            acc_sc[h0] = alphas[0] * acc_sc[h0] + o[:, :LANES]
            acc_sc[h1] = alphas[1] * acc_sc[h1] + o[:, LANES:]

    @pl.when(kb < j)
    def _():
        attend(False)

    @pl.when(kb == j)
    def _():
        attend(True)
        for c in range(FOX_HEADS // 2):
            a0 = acc_sc[2 * c]
            a1 = acc_sc[2 * c + 1]
            o0 = a0 / pltpu.roll(a0, FOX_HEAD_DIM, 1)
            o1 = pltpu.roll(a1, FOX_HEAD_DIM, 1) / a1
            cat_ref[0, :, c * LANES:(c + 1) * LANES] = (
                jnp.where(lane < FOX_HEAD_DIM, o0, o1).astype(cat_ref.dtype))


def _mixer_fox_prompt(layer, x, kt_b, vt_b, f_t, f_q, mem_k, mem_v, W, *, tl):
    B, L, _ = x.shape
    nkb = L // tl
    ft4 = jnp.swapaxes(f_t.reshape(B, FOX_HEADS, nkb, tl), 1, 2)
    jj = lambda t: _tri_unrank(t, nkb)[0]
    kk = lambda t: _tri_unrank(t, nkb)[1]
    kt_spec = pl.BlockSpec((1, FOX_HEADS, FOX_HEAD_DIM, tl), lambda b, t: (b, 0, 0, kk(t)))
    v_spec = pl.BlockSpec((1, FOX_HEADS, tl, LANES), lambda b, t: (b, 0, kk(t), 0))
    return pl.pallas_call(
        functools.partial(_mixer_fox_prompt_kernel, nkb=nkb),
        grid=(B, nkb * (nkb + 1) // 2),
        in_specs=[
            pl.BlockSpec((1, tl, D_MODEL), lambda b, t: (b, jj(t), 0)),
            _layer_spec(layer, (1, D_MODEL)),
            _layer_spec(layer, (D_MODEL, MIX_WIDTH)),
            _layer_spec(layer, (1, MEM_HEAD_DIM)),
            _mem_spec(layer, 1),
            _mem_spec(layer, 1),
            _layer_spec(layer - N_A, (1, LANES)),
            kt_spec,
            v_spec,
            pl.BlockSpec((1, 1, FOX_HEADS, tl), lambda b, t: (b, kk(t), 0, 0)),
            pl.BlockSpec((1, tl, FOX_HEADS), lambda b, t: (b, jj(t), 0)),
        ],
        out_specs=pl.BlockSpec((1, tl, MIX_WIDTH), lambda b, t: (b, jj(t), 0)),
        out_shape=jax.ShapeDtypeStruct((B, L, MIX_WIDTH), BF16),
        scratch_shapes=[
            pltpu.VMEM((FOX_HEADS, tl, LANES), BF16),
            pltpu.VMEM((FOX_HEADS, tl, LANES), F32),
            pltpu.VMEM((FOX_HEADS, tl, LANES), F32),
            pltpu.VMEM((tl, tl), F32),
        ],
        compiler_params=_params(2),
        name="mixer_fox_prompt",
    )(x, W["g_mix"], W["w_in"], W["q_norm_mem"], mem_k, mem_v, W["qnf_pair"], kt_b, vt_b, ft4, f_q)


def _block_diag(blocks):
    a, b = blocks
    z = jnp.zeros_like(a)
    return jnp.concatenate([jnp.concatenate([a, z], axis=1),
                            jnp.concatenate([z, b], axis=1)], axis=0)


def _mixer_fox_sample_kernel(x_ref, g_ref, w_in_ref, qn_ref, mk_ref, mv_ref, qnf_ref,
                             ktp_ref, vtp_ref, ktn_ref, vtn_ref, ft_ref, fq_ref, cat_ref):
    nb, tl, _ = x_ref.shape
    past = ktp_ref.shape[-1]
    z = _in_proj(x_ref, g_ref, w_in_ref)
    causal = (lax.broadcasted_iota(jnp.int32, (tl, 1), 0)
              >= lax.broadcasted_iota(jnp.int32, (1, tl), 1))
    lo = lax.broadcasted_iota(jnp.int32, (1, LANES), 1) < FOX_HEAD_DIM
    for i in range(nb):
        zi = z[i * tl:(i + 1) * tl]
        qs = _rms_head64(zi[:, :FOX_WIDTH], qnf_ref[0])
        for c in range(FOX_HEADS // 2):
            hs = (2 * c, 2 * c + 1)
            q = (qs[c] * FOX_SCALE).astype(BF16)
            s_p = _dot(q, _block_diag([ktp_ref[i, h].astype(BF16) for h in hs]))
            s_n = _dot(q, _block_diag([ktn_ref[i, h].astype(BF16) for h in hs]))
            pp, pn, inv_l = [], [], []
            for n, h in enumerate(hs):
                fq = fq_ref[i, :, h:h + 1]
                sp = (s_p[:, n * past:(n + 1) * past] + fq) - ft_ref[i, h:h + 1, 0:past]
                sn = (s_n[:, n * tl:(n + 1) * tl] + fq) - ft_ref[i, h:h + 1, past:past + tl]
                sn = jnp.where(causal, sn, -jnp.inf)
                m = jnp.maximum(jnp.max(sp, axis=-1, keepdims=True),
                                jnp.max(sn, axis=-1, keepdims=True))
                ep = jnp.exp(sp - m)
                en = jnp.exp(sn - m)
                inv_l.append(1.0 / (jnp.sum(ep, axis=-1, keepdims=True)
                                    + jnp.sum(en, axis=-1, keepdims=True)))
                pp.append(ep.astype(BF16))
                pn.append(en.astype(BF16))
            o = (_dot_nt(jnp.concatenate(pp, axis=1),
                         _block_diag([vtp_ref[i, h].astype(BF16) for h in hs]))
                 + _dot_nt(jnp.concatenate(pn, axis=1),
                           _block_diag([vtn_ref[i, h].astype(BF16) for h in hs])))
            scale = jnp.where(lo, inv_l[0], inv_l[1])
            cat_ref[i, :, c * LANES:(c + 1) * LANES] = (o * scale).astype(cat_ref.dtype)
        _mem_attend(zi[:, FOX_WIDTH:], qn_ref[0], mk_ref, mv_ref, i, cat_ref, FOX_WIDTH)


def _mixer_fox_sample(layer, x, kt_past, vt_past, kt_new, vt_new, f_t, f_q, mem_k, mem_v, W, *, nb):
    B, L, _ = x.shape
    past = kt_past.shape[-1]
    lk_pad = f_t.shape[-1]
    per_b = lambda *tail: pl.BlockSpec((nb,) + tail, lambda b: (b,) + (0,) * len(tail))
    return pl.pallas_call(
        _mixer_fox_sample_kernel,
        grid=(B // nb,),
        in_specs=[
            per_b(L, D_MODEL),
            _layer_spec(layer, (1, D_MODEL)),
            _layer_spec(layer, (D_MODEL, MIX_WIDTH)),
            _layer_spec(layer, (1, MEM_HEAD_DIM)),
            _mem_spec(layer, nb),
            _mem_spec(layer, nb),
            _layer_spec(layer - N_A, (1, LANES)),
            per_b(FOX_HEADS, FOX_HEAD_DIM, past),
            per_b(FOX_HEADS, FOX_HEAD_DIM, past),
            per_b(FOX_HEADS, FOX_HEAD_DIM, L),
            per_b(FOX_HEADS, FOX_HEAD_DIM, L),
            per_b(FOX_HEADS, lk_pad),
            per_b(L, FOX_HEADS),
        ],
        out_specs=per_b(L, MIX_WIDTH),
        out_shape=jax.ShapeDtypeStruct((B, L, MIX_WIDTH), BF16),
        compiler_params=_params(1),
        name="mixer_fox_sample",
    )(x, W["g_mix"], W["w_in"], W["q_norm_mem"], mem_k, mem_v, W["qnf_pair"],
      kt_past, vt_past, kt_new, vt_new, f_t, f_q)


def _out_ffn_kernel(x_ref, cat_ref, w_out_ref, g_ref, w_gu_ref, w_down_ref, y_ref):
    x1 = x_ref[...] + _dot(cat_ref[...], w_out_ref[0])
    xn = _rms(x1, g_ref[0]).astype(BF16)
    acc = x1
    for lo, hi in FFN_SPLITS:
        gate = _dot(xn, w_gu_ref[0, :, lo:hi])
        up = _dot(xn, w_gu_ref[0, :, D_FF + lo:D_FF + hi])
        h = (gate * (1.0 / (1.0 + jnp.exp(-gate)))) * up
        acc = acc + _dot(h.astype(BF16), w_down_ref[0, lo:hi, :])
    y_ref[...] = acc


def _out_ffn(layer, x2, cat2, W):
    T = x2.shape[0]
    return pl.pallas_call(
        _out_ffn_kernel,
        grid=(T // ROW_BLOCK,),
        in_specs=[
            pl.BlockSpec((ROW_BLOCK, D_MODEL), lambda r: (r, 0)),
            pl.BlockSpec((ROW_BLOCK, MIX_WIDTH), lambda r: (r, 0)),
            _layer_spec(layer, (MIX_WIDTH, D_MODEL)),
            _layer_spec(layer, (1, D_MODEL)),
            _layer_spec(layer, (D_MODEL, 2 * D_FF)),
            _layer_spec(layer, (D_FF, D_MODEL)),
        ],
        out_specs=pl.BlockSpec((ROW_BLOCK, D_MODEL), lambda r: (r, 0)),
        out_shape=jax.ShapeDtypeStruct((T, D_MODEL), F32),
        compiler_params=_params(1),
        name="out_ffn",
    )(x2, cat2, W["w_out"], W["g_ffn"], W["w_gu"], W["w_down"])


def _trunk(x, pos0, pool_hist, fox_past, mem_k, mem_v, W, *, nb, tl):
    B, L, _ = x.shape
    pool_states = []
    for i in range(DEPTH):
        if i < N_A:
            cat, state = _mixer_pool(i, x, pool_hist, mem_k, mem_v, W, pos0=pos0, nb=nb, tl=tl)
            pool_states.append(state[:, :, 1:, :])
        else:
            if i == N_A:
                kt_new, vt_new, lft_new, kt_b, vt_b = _kv_proj(x, W, nb=nb, tl=tl)
                if fox_past is None:
                    lft_all = lft_new
                else:
                    lft_all = jnp.concatenate([fox_past[2], lft_new], axis=2)
                lk = lft_all.shape[2]
                lk_pad = -(-lk // LANES) * LANES
                lf_t = jnp.pad(lft_all.reshape(B * FOX_HEADS, lk), ((0, 0), (0, lk_pad - lk)))
                f_t = _cumsum_lanes(lf_t).reshape(B, FOX_HEADS, lk_pad)
                f_q = jnp.swapaxes(f_t[:, :, lk - L:lk], 1, 2)
            if fox_past is None:
                cat = _mixer_fox_prompt(i, x, kt_b, vt_b, f_t, f_q, mem_k, mem_v, W, tl=tl)
            else:
                cat = _mixer_fox_sample(i, x, fox_past[0], fox_past[1], kt_new, vt_new, f_t, f_q,
                                        mem_k, mem_v, W, nb=2)
        x = _out_ffn(i, x.reshape(B * L, D_MODEL), cat.reshape(B * L, MIX_WIDTH), W
                     ).reshape(B, L, D_MODEL)
    fox_new = (jnp.transpose(kt_new, (0, 3, 1, 2)), jnp.transpose(vt_new, (0, 3, 1, 2)),
               jnp.swapaxes(lft_new, 1, 2))
    return x, jnp.concatenate(pool_states, axis=0), fox_new


def kernel(x_prompt, x_sample, state_pool, cache_fox_k, cache_fox_v, cache_fox_logf, cache_mem_k,
           cache_mem_v, mem_prompt, g_mix, w_in, w_out, q_norm_mem, g_mem, w_mem_kv, k_norm_mem,
           w_pool, pool_scale, q_norm_fox, g_kv, w_kv, k_norm_fox, b_f, g_ffn, w_gu, w_down):
    B, L, _ = x_prompt.shape
    SB, SL, _ = x_sample.shape
    W = dict(
        g_mix=g_mix.reshape(DEPTH, 1, D_MODEL), w_in=w_in.astype(BF16), w_out=w_out.astype(BF16),
        q_norm_mem=q_norm_mem.reshape(DEPTH, 1, MEM_HEAD_DIM), w_pool=w_pool.astype(BF16),
        pool_scale=pool_scale.reshape(N_A, 1, POOL_WIDTH),
        qnf_pair=jnp.tile(q_norm_fox, (1, 2)).reshape(DEPTH - N_A, 1, LANES),
        g_kv=g_kv.reshape(1, D_MODEL),
        w_kv_t=jnp.pad(w_kv.T, ((0, KV_ROWS - w_kv.shape[1]), (0, 0))).astype(BF16),
        w_v=w_kv[:, FOX_WIDTH:2 * FOX_WIDTH].astype(BF16),
        kn_col=k_norm_fox.reshape(1, FOX_HEAD_DIM, 1), b_f=b_f.reshape(FOX_HEADS, 1),
        g_ffn=g_ffn.reshape(DEPTH, 1, D_MODEL), w_gu=w_gu.astype(BF16), w_down=w_down.astype(BF16))

    mem_k_p, mem_v_p = _mem_kv(mem_prompt, g_mem.reshape(DEPTH, 1, D_MODEL), w_mem_kv.astype(BF16),
                               k_norm_mem.reshape(DEPTH, 1, MEM_HEAD_DIM))
    hist_p = jnp.zeros((N_A, B, HIST_ROWS, POOL_WIDTH), F32)
    y_p, pool_p, fox_p = _trunk(x_prompt, 0, hist_p, None, mem_k_p, mem_v_p, W, nb=1, tl=ROW_BLOCK)

    hist_s = jnp.pad(state_pool, ((0, 0), (0, 0), (1, 0), (0, 0)))
    past = (jnp.transpose(cache_fox_k, (0, 2, 3, 1)), jnp.transpose(cache_fox_v, (0, 2, 3, 1)),
            jnp.swapaxes(cache_fox_logf, 1, 2))
    y_s, pool_s, fox_s = _trunk(
        x_sample, PAST_LEN, hist_s, past, cache_mem_k.reshape(DEPTH, SB, MEM_ROWS, MEM_HEAD_DIM),
        cache_mem_v.reshape(DEPTH, SB, MEM_ROWS, MEM_HEAD_DIM), W, nb=ROW_BLOCK // SL, tl=SL)

    mem_shape = (DEPTH, B, MEM_TOKENS, MEM_HEADS, MEM_HEAD_DIM)
    return (y_p, y_s, pool_p, fox_p[0], fox_p[1], fox_p[2], mem_k_p.reshape(mem_shape),
            mem_v_p.reshape(mem_shape), pool_s, fox_s[0], fox_s[1], fox_s[2])
```

```python
import functools

import jax
import jax.numpy as jnp
from jax import lax
from jax.experimental import pallas as pl
from jax.experimental.pallas import tpu as pltpu

F32 = jnp.float32
BF16 = jnp.bfloat16

D_MODEL = 1024
DEPTH = 4
N_A = DEPTH // 2
PAST_LEN = 1024
POOL_WINDOWS = (2, 4, 8, 16)
POOL_GROUPS = len(POOL_WINDOWS)
POOL_WIDTH = D_MODEL // 2
POOL_GROUP_DIM = POOL_WIDTH // POOL_GROUPS
POOL_HIST = max(POOL_WINDOWS) - 1
HIST_ROWS = POOL_HIST + 1
FOX_HEAD_DIM = 64
FOX_WIDTH = D_MODEL // 2
FOX_HEADS = FOX_WIDTH // FOX_HEAD_DIM
MEM_TOKENS = 256
MEM_HEADS = 4
MEM_WIDTH = D_MODEL // 2
MEM_HEAD_DIM = MEM_WIDTH // MEM_HEADS
MIX_WIDTH = POOL_WIDTH + MEM_WIDTH
D_FF = ((8 * D_MODEL // 3 + 255) // 256) * 256
EPS = 1e-6
FOX_SCALE = FOX_HEAD_DIM ** -0.5
MEM_SCALE = MEM_HEAD_DIM ** -0.5
LOG2E = 1.4426950408889634
AUG_ROWS = 16

LANES = 128
ROW_BLOCK = 512
FOX_BLOCK = 512
POOL_BLOCK = 1024
KV_ROWS = 2 * FOX_WIDTH + 16
MXU_DIM = 256
FFN_SPLITS = ((0, 6 * MXU_DIM), (6 * MXU_DIM, D_FF))
VMEM_LIMIT = 56 * 1024 * 1024


def _dot(a, b):
    return jnp.dot(a, b, preferred_element_type=F32)


def _dot_nt(a, b):
    return lax.dot_general(a, b, (((1,), (1,)), ((), ())), preferred_element_type=F32)


def _rms(x, g):
    ms = jnp.mean(x * x, axis=-1, keepdims=True)
    return (x * lax.rsqrt(ms + EPS)) * g


def _rms_head64(x, g_pair):
    lo = lax.broadcasted_iota(jnp.int32, (1, LANES), 1) < FOX_HEAD_DIM
    outs = []
    for c in range(x.shape[-1] // LANES):
        xc = x[:, c * LANES:(c + 1) * LANES]
        sq = xc * xc
        s_lo = jnp.sum(jnp.where(lo, sq, 0.0), axis=-1, keepdims=True)
        s_hi = jnp.sum(jnp.where(lo, 0.0, sq), axis=-1, keepdims=True)
        ms = jnp.where(lo, s_lo, s_hi) * (1.0 / FOX_HEAD_DIM)
        outs.append((xc * lax.rsqrt(ms + EPS)) * g_pair)
    return outs


def _const_spec(shape):
    return pl.BlockSpec(shape, lambda *_: (0,) * len(shape), pipeline_mode=pl.Buffered(1))


def _layer_spec(layer, shape):
    return pl.BlockSpec((1,) + shape, lambda *_: (layer,) + (0,) * len(shape),
                        pipeline_mode=pl.Buffered(1))


MEM_ROWS = MEM_TOKENS * MEM_HEADS


def _mem_spec(layer, nb):
    return pl.BlockSpec((1, nb, MEM_ROWS, MEM_HEAD_DIM), lambda b, *_: (layer, b, 0, 0))


def _head_rows(h):
    return pl.ds(h, MEM_TOKENS, stride=MEM_HEADS)


def _params(n_grid, flags=None):
    return pltpu.CompilerParams(
        dimension_semantics=("arbitrary",) * n_grid, vmem_limit_bytes=VMEM_LIMIT, flags=flags)


def _mem_kv_kernel(mem_ref, g_ref, w_ref, kn_ref, k_ref, v_ref):
    nb = mem_ref.shape[0]
    x = mem_ref[...].reshape(nb * MEM_TOKENS, D_MODEL)
    kv = _dot(_rms(x, g_ref[0]).astype(BF16), w_ref[0])
    for h in range(MEM_HEADS):
        ks = slice(h * MEM_HEAD_DIM, (h + 1) * MEM_HEAD_DIM)
        vs = slice(MEM_WIDTH + h * MEM_HEAD_DIM, MEM_WIDTH + (h + 1) * MEM_HEAD_DIM)
        k_ref[0, :, _head_rows(h), :] = _rms(kv[:, ks], kn_ref[0]).reshape(nb, MEM_TOKENS, MEM_HEAD_DIM)
        v_ref[0, :, _head_rows(h), :] = kv[:, vs].reshape(nb, MEM_TOKENS, MEM_HEAD_DIM)


def _mem_kv(mem, g_mem, w_mem_kv, k_norm_mem):
    B = mem.shape[0]
    nb = 4
    out = jax.ShapeDtypeStruct((DEPTH, B, MEM_ROWS, MEM_HEAD_DIM), F32)
    out_spec = pl.BlockSpec((1, nb, MEM_ROWS, MEM_HEAD_DIM), lambda i, b: (i, b, 0, 0))
    return pl.pallas_call(
        _mem_kv_kernel,
        grid=(DEPTH, B // nb),
        in_specs=[
            pl.BlockSpec((nb, MEM_TOKENS, D_MODEL), lambda i, b: (b, 0, 0)),
            pl.BlockSpec((1, 1, D_MODEL), lambda i, b: (i, 0, 0)),
            pl.BlockSpec((1, D_MODEL, 2 * MEM_WIDTH), lambda i, b: (i, 0, 0)),
            pl.BlockSpec((1, 1, MEM_HEAD_DIM), lambda i, b: (i, 0, 0)),
        ],
        out_specs=[out_spec, out_spec],
        out_shape=[out, out],
        compiler_params=_params(2),
        name="mem_kv",
    )(mem, g_mem, w_mem_kv, k_norm_mem)


def _in_proj(x_ref, g_ref, w_ref):
    nb, tl, _ = x_ref.shape
    x = x_ref[...].reshape(nb * tl, D_MODEL)
    return _dot(_rms(x, g_ref[0]).astype(BF16), w_ref[0])


def _mem_attend(zq, qn, mk_ref, mv_ref, i, cat_ref, col0):
    for h in range(MEM_HEADS):
        sl = slice(h * MEM_HEAD_DIM, (h + 1) * MEM_HEAD_DIM)
        q = _rms(zq[:, sl], qn).astype(BF16)
        s = _dot_nt(q, mk_ref[0, i, _head_rows(h), :].astype(BF16)) * (MEM_SCALE * LOG2E)
        p = jnp.exp2(s - jnp.max(s, axis=-1, keepdims=True)).astype(BF16)
        v = mv_ref[0, i, _head_rows(h), :].astype(BF16)
        o = _dot(p, jnp.concatenate([v, jnp.ones_like(v)], axis=-1))
        o = o[:, :MEM_HEAD_DIM] / o[:, MEM_HEAD_DIM:]
        cat_ref[i, :, col0 + h * MEM_HEAD_DIM:col0 + (h + 1) * MEM_HEAD_DIM] = o.astype(cat_ref.dtype)


def _mem_attend_paired(zq, qn, mk_ref, mv_ref, i, cat_ref, col0):
    z = jnp.zeros((MEM_TOKENS, MEM_HEAD_DIM), BF16)
    one = jnp.ones((MEM_TOKENS, MEM_HEAD_DIM), BF16)
    for c in range(MEM_HEADS // 2):
        hs = (2 * c, 2 * c + 1)
        q = jnp.concatenate(
            [_rms(zq[:, h * MEM_HEAD_DIM:(h + 1) * MEM_HEAD_DIM], qn) for h in hs], axis=1)
        k0, k1 = (mk_ref[0, i, _head_rows(h), :].astype(BF16) for h in hs)
        v0, v1 = (mv_ref[0, i, _head_rows(h), :].astype(BF16) for h in hs)
        k_pair = jnp.concatenate([jnp.concatenate([k0, z], axis=1),
                                  jnp.concatenate([z, k1], axis=1)], axis=0)
        v_pair = jnp.concatenate([jnp.concatenate([v0, one, z, z], axis=1),
                                  jnp.concatenate([z, z, v1, one], axis=1)], axis=0)
        s = _dot_nt(q.astype(BF16), k_pair) * (MEM_SCALE * LOG2E)
        p = jnp.concatenate(
            [jnp.exp2(sh - jnp.max(sh, axis=-1, keepdims=True))
             for sh in (s[:, :MEM_TOKENS], s[:, MEM_TOKENS:])], axis=1).astype(BF16)
        o = _dot(p, v_pair)
        for n, h in enumerate(hs):
            oh = o[:, 2 * n * MEM_HEAD_DIM:(2 * n + 1) * MEM_HEAD_DIM]
            lh = o[:, (2 * n + 1) * MEM_HEAD_DIM:(2 * n + 2) * MEM_HEAD_DIM]
            cat_ref[i, :, col0 + h * MEM_HEAD_DIM:col0 + (h + 1) * MEM_HEAD_DIM] = (
                (oh / lh).astype(cat_ref.dtype))


def _mixer_pool_kernel(x_ref, g_ref, w_in_ref, qn_ref, mk_ref, mv_ref, hist_ref, wp_ref, ps_ref,
                       cat_ref, state_ref, ubuf, *, pos0):
    nb, tl, _ = x_ref.shape
    j = pl.program_id(1)
    z = _in_proj(x_ref, g_ref, w_in_ref)

    @pl.when(j == 0)
    def _():
        ubuf[:, 0:HIST_ROWS, :] = hist_ref[0]

    pos = pos0 + j * tl + lax.broadcasted_iota(jnp.int32, (tl, 1), 0)
    for i in range(nb):
        zi = z[i * tl:(i + 1) * tl]
        u = zi[:, :POOL_WIDTH]
        ubuf[i, HIST_ROWS:HIST_ROWS + tl, :] = u
        for g, w in enumerate(POOL_WINDOWS):
            sl = slice(g * POOL_GROUP_DIM, (g + 1) * POOL_GROUP_DIM)
            ug = u[:, sl]
            acc = ug
            for k in range(1, w):
                acc = acc + ubuf[i, HIST_ROWS - k:HIST_ROWS - k + tl, sl]
            cnt = jnp.minimum(pos + 1, w).astype(F32)
            d = acc / cnt - ug
            y = _dot(d.astype(BF16), wp_ref[0, g]) * ps_ref[0, :, sl]
            cat_ref[i, :, sl] = y.astype(cat_ref.dtype)
        _mem_attend(zi[:, POOL_WIDTH:], qn_ref[0], mk_ref, mv_ref, i, cat_ref, POOL_WIDTH)
        tail = ubuf[i, tl:tl + HIST_ROWS, :]
        state_ref[0, i] = tail
        ubuf[i, 0:HIST_ROWS, :] = tail


def _mixer_pool(layer, x, hist, mem_k, mem_v, W, *, pos0, nb, tl):
    B, L, _ = x.shape
    assert tl >= HIST_ROWS and L % tl == 0 and B % nb == 0
    hist_spec = pl.BlockSpec((1, nb, HIST_ROWS, POOL_WIDTH), lambda b, j: (layer, b, 0, 0))
    return pl.pallas_call(
        functools.partial(_mixer_pool_kernel, pos0=pos0),
        grid=(B // nb, L // tl),
        in_specs=[
            pl.BlockSpec((nb, tl, D_MODEL), lambda b, j: (b, j, 0)),
            _layer_spec(layer, (1, D_MODEL)),
            _layer_spec(layer, (D_MODEL, MIX_WIDTH)),
            _layer_spec(layer, (1, MEM_HEAD_DIM)),
            _mem_spec(layer, nb),
            _mem_spec(layer, nb),
            hist_spec,
            _layer_spec(layer, (POOL_GROUPS, POOL_GROUP_DIM, POOL_GROUP_DIM)),
            _layer_spec(layer, (1, POOL_WIDTH)),
        ],
        out_specs=[
            pl.BlockSpec((nb, tl, MIX_WIDTH), lambda b, j: (b, j, 0)),
            pl.BlockSpec((1, nb, HIST_ROWS, POOL_WIDTH), lambda b, j: (0, b, 0, 0)),
        ],
        out_shape=[
            jax.ShapeDtypeStruct((B, L, MIX_WIDTH), BF16),
            jax.ShapeDtypeStruct((1, B, HIST_ROWS, POOL_WIDTH), F32),
        ],
        scratch_shapes=[pltpu.VMEM((nb, HIST_ROWS + tl, POOL_WIDTH), F32)],
        compiler_params=_params(2),
        name="mixer_pool",
    )(x, W["g_mix"], W["w_in"], W["q_norm_mem"], mem_k, mem_v, hist, W["w_pool"], W["pool_scale"])


def _kv_proj_kernel(x_ref, g_ref, w_ref, wv_ref, kn_ref, bf_ref, k_ref, v_ref, lf_ref, kb_ref, vb_ref):
    nb, tl, _ = x_ref.shape
    rows = nb * tl
    x = x_ref[...].reshape(rows, D_MODEL)
    xn = _rms(x, g_ref[...]).astype(BF16)
    zt = _dot_nt(w_ref[...], xn)
    zv = _dot(xn, wv_ref[...])
    lo = lax.broadcasted_iota(jnp.int32, (1, LANES), 1) < FOX_HEAD_DIM
    for h in range(FOX_HEADS):
        pair = zv[:, (h // 2) * LANES:(h // 2 + 1) * LANES]
        base = pair if h % 2 == 0 else pltpu.roll(pair, FOX_HEAD_DIM, 1)
        vb_ref[:, h] = jnp.where(lo, base, 1.0).astype(BF16).reshape(nb, tl, LANES)
    k3 = zt[:FOX_WIDTH].reshape(FOX_HEADS, FOX_HEAD_DIM, rows)
    ms = jnp.mean(k3 * k3, axis=1, keepdims=True)
    k3 = (k3 * lax.rsqrt(ms + EPS)) * kn_ref[...]
    v3 = zt[FOX_WIDTH:2 * FOX_WIDTH].reshape(FOX_HEADS, FOX_HEAD_DIM, rows)
    t = -(zt[2 * FOX_WIDTH:2 * FOX_WIDTH + FOX_HEADS] + bf_ref[...])
    lf = -(jnp.maximum(t, 0.0) + jnp.log1p(jnp.exp(-jnp.abs(t))))
    for i in range(nb):
        cols = slice(i * tl, (i + 1) * tl)
        k_ref[i] = k3[:, :, cols]
        v_ref[i] = v3[:, :, cols]
        lf_ref[i] = lf[:, cols]
        kb_ref[i] = k3[:, :, cols].astype(BF16)


def _kv_proj(x, W, *, nb, tl):
    B, L, _ = x.shape
    hd = pl.BlockSpec((nb, FOX_HEADS, FOX_HEAD_DIM, tl), lambda b, j: (b, 0, 0, j))
    heads = jax.ShapeDtypeStruct((B, FOX_HEADS, FOX_HEAD_DIM, L), F32)
    return pl.pallas_call(
        _kv_proj_kernel,
        grid=(B // nb, L // tl),
        in_specs=[
            pl.BlockSpec((nb, tl, D_MODEL), lambda b, j: (b, j, 0)),
            _const_spec((1, D_MODEL)),
            _const_spec((KV_ROWS, D_MODEL)),
            _const_spec((D_MODEL, FOX_WIDTH)),
            _const_spec((1, FOX_HEAD_DIM, 1)),
            _const_spec((FOX_HEADS, 1)),
        ],
        out_specs=[hd, hd, pl.BlockSpec((nb, FOX_HEADS, tl), lambda b, j: (b, 0, j)), hd,
                   pl.BlockSpec((nb, FOX_HEADS, tl, LANES), lambda b, j: (b, 0, j, 0))],
        out_shape=[heads, heads, jax.ShapeDtypeStruct((B, FOX_HEADS, L), F32),
                   jax.ShapeDtypeStruct((B, FOX_HEADS, FOX_HEAD_DIM, L), BF16),
                   jax.ShapeDtypeStruct((B, FOX_HEADS, L, LANES), BF16)],
        compiler_params=_params(2),
        name="kv_proj",
    )(x, W["g_kv"], W["w_kv_t"], W["w_v"], W["kn_col"], W["b_f"])


def _cumsum_kernel(lf_ref, f_ref):
    rows, n = lf_ref.shape
    r = lax.broadcasted_iota(jnp.int32, (LANES, LANES), 0)
    c = lax.broadcasted_iota(jnp.int32, (LANES, LANES), 1)
    tri = jnp.where(r <= c, 1.0, 0.0).astype(BF16)
    carry = jnp.zeros((rows, 1), F32)
    for ch in range(n // LANES):
        x = lf_ref[:, ch * LANES:(ch + 1) * LANES]
        hi = x.astype(BF16)
        r1 = x - hi.astype(F32)
        mid = r1.astype(BF16)
        low = (r1 - mid.astype(F32)).astype(BF16)
        y = (_dot(hi, tri) + _dot(mid, tri)) + _dot(low, tri) + carry
        f_ref[:, ch * LANES:(ch + 1) * LANES] = y
        carry = y[:, LANES - 1:LANES]


def _cumsum_lanes(lf_t):
    return pl.pallas_call(
        _cumsum_kernel,
        out_shape=jax.ShapeDtypeStruct(lf_t.shape, F32),
        name="logf_cumsum",
    )(lf_t)


def _tri_unrank(t, n):
    row = sum((t >= k * (k + 1) // 2).astype(jnp.int32) for k in range(1, n))
    return row, t - row * (row + 1) // 2


def _split3(x):
    hi = x.astype(BF16).astype(F32)
    r1 = x - hi
    mid = r1.astype(BF16).astype(F32)
    low = (r1 - mid).astype(BF16).astype(F32)
    return hi, mid, low


def _mixer_fox_prompt_kernel(x_ref, g_ref, w_in_ref, qn_ref, mk_ref, mv_ref, qnf_ref,
                             kt_ref, v_ref, ft_ref, fq_ref, cat_ref, q_sc, m_sc, acc_sc, mask_sc,
                             *, nkb):
    _, tl, _ = x_ref.shape
    tk = kt_ref.shape[-1]
    j, kb = _tri_unrank(pl.program_id(1), nkb)
    lane = lax.broadcasted_iota(jnp.int32, (1, LANES), 1)
    row = lax.broadcasted_iota(jnp.int32, (AUG_ROWS, 1), 0)

    @pl.when(kb == 0)
    def _():
        z = _in_proj(x_ref, g_ref, w_in_ref)
        _mem_attend_paired(z[:, FOX_WIDTH:], qn_ref[0], mk_ref, mv_ref, 0, cat_ref, FOX_WIDTH)
        qs = _rms_head64(z[:, :FOX_WIDTH], qnf_ref[0])
        for h in range(FOX_HEADS):
            base = qs[h // 2] if h % 2 == 0 else pltpu.roll(qs[h // 2], FOX_HEAD_DIM, 1)
            hi, mid, low = _split3(fq_ref[0, :, h:h + 1] * LOG2E)
            tail = jnp.where(lane < FOX_HEAD_DIM + 3, 1.0,
                             jnp.where(lane == FOX_HEAD_DIM + 3, hi,
                                       jnp.where(lane == FOX_HEAD_DIM + 4, mid,
                                                 jnp.where(lane == FOX_HEAD_DIM + 5, low, 0.0))))
            q_sc[h] = jnp.where(lane < FOX_HEAD_DIM, base * (FOX_SCALE * LOG2E), tail).astype(BF16)
        m_sc[...] = jnp.full(m_sc.shape, -jnp.inf, F32)
        acc_sc[...] = jnp.zeros(acc_sc.shape, F32)
        causal = (lax.broadcasted_iota(jnp.int32, (tl, 1), 0)
                  >= lax.broadcasted_iota(jnp.int32, (1, tk), 1))
        mask_sc[...] = jnp.where(causal, 0.0, -jnp.inf)

    def k_aug(h):
        hi, mid, low = _split3(ft_ref[0, 0, h:h + 1, :] * (-LOG2E))
        aug = jnp.where(row == 0, hi, jnp.where(row == 1, mid, jnp.where(row == 2, low,
                        jnp.where(row < 6, 1.0, 0.0)))).astype(BF16)
        pad = jnp.zeros((FOX_HEAD_DIM - AUG_ROWS, tk), BF16)
        return jnp.concatenate([kt_ref[0, h], aug, pad], axis=0)

    def attend(masked):
        zk = jnp.zeros((LANES, tk), BF16)
        zv = jnp.zeros((tk, LANES), BF16)
        for c in range(FOX_HEADS // 2):
            h0, h1 = 2 * c, 2 * c + 1
            k_pair = jnp.concatenate([jnp.concatenate([k_aug(h0), zk], axis=1),
                                      jnp.concatenate([zk, k_aug(h1)], axis=1)], axis=0)
            v_pair = jnp.concatenate([jnp.concatenate([v_ref[0, h0], zv], axis=1),
                                      jnp.concatenate([zv, v_ref[0, h1]], axis=1)], axis=0)
            q_pair = jnp.concatenate([q_sc[h0], q_sc[h1]], axis=1)
            s = _dot(q_pair, k_pair)
            ps, alphas = [], []
            for h, sh in ((h0, s[:, :tk]), (h1, s[:, tk:])):
                if masked:
                    sh = sh + mask_sc[...]
                m_old = m_sc[h]
                m_new = jnp.maximum(m_old, jnp.max(sh, axis=-1, keepdims=True))
                ps.append(jnp.exp2(sh - m_new[:, 0:1]).astype(BF16))
                alphas.append(jnp.exp2(m_old - m_new))
                m_sc[h] = m_new
            o = _dot(jnp.concatenate(ps, axis=1), v_pair)
            acc_sc[h0] = alphas[0] * acc_sc[h0] + o[:, :LANES]
            acc_sc[h1] = alphas[1] * acc_sc[h1] + o[:, LANES:]

    @pl.when(kb < j)
    def _():
        attend(False)

    @pl.when(kb == j)
    def _():
        attend(True)
        for c in range(FOX_HEADS // 2):
            a0 = acc_sc[2 * c]
            a1 = acc_sc[2 * c + 1]
            o0 = a0 / pltpu.roll(a0, FOX_HEAD_DIM, 1)
            o1 = pltpu.roll(a1, FOX_HEAD_DIM, 1) / a1
            cat_ref[0, :, c * LANES:(c + 1) * LANES] = (
                jnp.where(lane < FOX_HEAD_DIM, o0, o1).astype(cat_ref.dtype))


def _mixer_fox_prompt(layer, x, kt_b, vt_b, f_t, f_q, mem_k, mem_v, W, *, tl):
    B, L, _ = x.shape
    nkb = L // tl
    ft4 = jnp.swapaxes(f_t.reshape(B, FOX_HEADS, nkb, tl), 1, 2)
    jj = lambda t: _tri_unrank(t, nkb)[0]
    kk = lambda t: _tri_unrank(t, nkb)[1]
    kt_spec = pl.BlockSpec((1, FOX_HEADS, FOX_HEAD_DIM, tl), lambda b, t: (b, 0, 0, kk(t)))
    v_spec = pl.BlockSpec((1, FOX_HEADS, tl, LANES), lambda b, t: (b, 0, kk(t), 0))
    return pl.pallas_call(
        functools.partial(_mixer_fox_prompt_kernel, nkb=nkb),
        grid=(B, nkb * (nkb + 1) // 2),
        in_specs=[
            pl.BlockSpec((1, tl, D_MODEL), lambda b, t: (b, jj(t), 0)),
            _layer_spec(layer, (1, D_MODEL)),
            _layer_spec(layer, (D_MODEL, MIX_WIDTH)),
            _layer_spec(layer, (1, MEM_HEAD_DIM)),
            _mem_spec(layer, 1),
            _mem_spec(layer, 1),
            _layer_spec(layer - N_A, (1, LANES)),
            kt_spec,
            v_spec,
            pl.BlockSpec((1, 1, FOX_HEADS, tl), lambda b, t: (b, kk(t), 0, 0)),
            pl.BlockSpec((1, tl, FOX_HEADS), lambda b, t: (b, jj(t), 0)),
        ],
        out_specs=pl.BlockSpec((1, tl, MIX_WIDTH), lambda b, t: (b, jj(t), 0)),
        out_shape=jax.ShapeDtypeStruct((B, L, MIX_WIDTH), BF16),
        scratch_shapes=[
            pltpu.VMEM((FOX_HEADS, tl, LANES), BF16),
            pltpu.VMEM((FOX_HEADS, tl, LANES), F32),
            pltpu.VMEM((FOX_HEADS, tl, LANES), F32),
            pltpu.VMEM((tl, tl), F32),
        ],
        compiler_params=_params(2),
        name="mixer_fox_prompt",
    )(x, W["g_mix"], W["w_in"], W["q_norm_mem"], mem_k, mem_v, W["qnf_pair"], kt_b, vt_b, ft4, f_q)


def _block_diag(blocks):
    a, b = blocks
    z = jnp.zeros_like(a)
    return jnp.concatenate([jnp.concatenate([a, z], axis=1),
                            jnp.concatenate([z, b], axis=1)], axis=0)


def _mixer_fox_sample_kernel(x_ref, g_ref, w_in_ref, qn_ref, mk_ref, mv_ref, qnf_ref,
                             ktp_ref, vtp_ref, ktn_ref, vtn_ref, ft_ref, fq_ref, cat_ref):
    nb, tl, _ = x_ref.shape
    past = ktp_ref.shape[-1]
    z = _in_proj(x_ref, g_ref, w_in_ref)
    causal = (lax.broadcasted_iota(jnp.int32, (tl, 1), 0)
              >= lax.broadcasted_iota(jnp.int32, (1, tl), 1))
    lo = lax.broadcasted_iota(jnp.int32, (1, LANES), 1) < FOX_HEAD_DIM
    for i in range(nb):
        zi = z[i * tl:(i + 1) * tl]
        qs = _rms_head64(zi[:, :FOX_WIDTH], qnf_ref[0])
        for c in range(FOX_HEADS // 2):
            hs = (2 * c, 2 * c + 1)
            q = (qs[c] * FOX_SCALE).astype(BF16)
            s_p = _dot(q, _block_diag([ktp_ref[i, h].astype(BF16) for h in hs]))
            s_n = _dot(q, _block_diag([ktn_ref[i, h].astype(BF16) for h in hs]))
            pp, pn, inv_l = [], [], []
            for n, h in enumerate(hs):
                fq = fq_ref[i, :, h:h + 1]
                sp = (s_p[:, n * past:(n + 1) * past] + fq) - ft_ref[i, h:h + 1, 0:past]
                sn = (s_n[:, n * tl:(n + 1) * tl] + fq) - ft_ref[i, h:h + 1, past:past + tl]
                sn = jnp.where(causal, sn, -jnp.inf)
                m = jnp.maximum(jnp.max(sp, axis=-1, keepdims=True),
                                jnp.max(sn, axis=-1, keepdims=True))
                ep = jnp.exp(sp - m)
                en = jnp.exp(sn - m)
                inv_l.append(1.0 / (jnp.sum(ep, axis=-1, keepdims=True)
                                    + jnp.sum(en, axis=-1, keepdims=True)))
                pp.append(ep.astype(BF16))
                pn.append(en.astype(BF16))
            o = (_dot_nt(jnp.concatenate(pp, axis=1),
                         _block_diag([vtp_ref[i, h].astype(BF16) for h in hs]))
                 + _dot_nt(jnp.concatenate(pn, axis=1),
                           _block_diag([vtn_ref[i, h].astype(BF16) for h in hs])))
            scale = jnp.where(lo, inv_l[0], inv_l[1])
            cat_ref[i, :, c * LANES:(c + 1) * LANES] = (o * scale).astype(cat_ref.dtype)
        _mem_attend(zi[:, FOX_WIDTH:], qn_ref[0], mk_ref, mv_ref, i, cat_ref, FOX_WIDTH)


def _mixer_fox_sample(layer, x, kt_past, vt_past, kt_new, vt_new, f_t, f_q, mem_k, mem_v, W, *, nb):
    B, L, _ = x.shape
    past = kt_past.shape[-1]
    lk_pad = f_t.shape[-1]
    per_b = lambda *tail: pl.BlockSpec((nb,) + tail, lambda b: (b,) + (0,) * len(tail))
    return pl.pallas_call(
        _mixer_fox_sample_kernel,
        grid=(B // nb,),
        in_specs=[
            per_b(L, D_MODEL),
            _layer_spec(layer, (1, D_MODEL)),
            _layer_spec(layer, (D_MODEL, MIX_WIDTH)),
            _layer_spec(layer, (1, MEM_HEAD_DIM)),
            _mem_spec(layer, nb),
            _mem_spec(layer, nb),
            _layer_spec(layer - N_A, (1, LANES)),
            per_b(FOX_HEADS, FOX_HEAD_DIM, past),
            per_b(FOX_HEADS, FOX_HEAD_DIM, past),
            per_b(FOX_HEADS, FOX_HEAD_DIM, L),
            per_b(FOX_HEADS, FOX_HEAD_DIM, L),
            per_b(FOX_HEADS, lk_pad),
            per_b(L, FOX_HEADS),
        ],
        out_specs=per_b(L, MIX_WIDTH),
        out_shape=jax.ShapeDtypeStruct((B, L, MIX_WIDTH), BF16),
        compiler_params=_params(1),
        name="mixer_fox_sample",
    )(x, W["g_mix"], W["w_in"], W["q_norm_mem"], mem_k, mem_v, W["qnf_pair"],
      kt_past, vt_past, kt_new, vt_new, f_t, f_q)


def _out_ffn_kernel(xa_ref, cata_ref, xb_ref, catb_ref, w_out_ref, g_ref, w_gu_ref, w_down_ref,
                    ya_ref, yb_ref, *, na):
    def body(x_ref, cat_ref, y_ref):
        x1 = x_ref[...] + _dot(cat_ref[...], w_out_ref[0])
        xn = _rms(x1, g_ref[0]).astype(BF16)
        acc = x1
        for lo, hi in FFN_SPLITS:
            gate = _dot(xn, w_gu_ref[0, :, lo:hi])
            up = _dot(xn, w_gu_ref[0, :, D_FF + lo:D_FF + hi])
            h = (gate * (1.0 / (1.0 + jnp.exp(-gate)))) * up
            acc = acc + _dot(h.astype(BF16), w_down_ref[0, lo:hi, :])
        y_ref[...] = acc

    @pl.when(pl.program_id(0) < na)
    def _():
        body(xa_ref, cata_ref, ya_ref)

    @pl.when(pl.program_id(0) >= na)
    def _():
        body(xb_ref, catb_ref, yb_ref)


def _out_ffn(layer, xa, cata, xb, catb, W):
    na = xa.shape[0] // ROW_BLOCK
    nb = xb.shape[0] // ROW_BLOCK
    a_map = lambda r: (jnp.minimum(r, na - 1), 0)
    b_map = lambda r: (jnp.maximum(r - na, 0), 0)
    return pl.pallas_call(
        functools.partial(_out_ffn_kernel, na=na),
        grid=(na + nb,),
        in_specs=[
            pl.BlockSpec((ROW_BLOCK, D_MODEL), a_map),
            pl.BlockSpec((ROW_BLOCK, MIX_WIDTH), a_map),
            pl.BlockSpec((ROW_BLOCK, D_MODEL), b_map),
            pl.BlockSpec((ROW_BLOCK, MIX_WIDTH), b_map),
            _layer_spec(layer, (MIX_WIDTH, D_MODEL)),
            _layer_spec(layer, (1, D_MODEL)),
            _layer_spec(layer, (D_MODEL, 2 * D_FF)),
            _layer_spec(layer, (D_FF, D_MODEL)),
        ],
        out_specs=[pl.BlockSpec((ROW_BLOCK, D_MODEL), a_map),
                   pl.BlockSpec((ROW_BLOCK, D_MODEL), b_map)],
        out_shape=[jax.ShapeDtypeStruct(xa.shape, F32), jax.ShapeDtypeStruct(xb.shape, F32)],
        compiler_params=_params(1),
        name="out_ffn",
    )(xa, cata, xb, catb, W["w_out"], W["g_ffn"], W["w_gu"], W["w_down"])


def _mixer(i, x, st, W):
    B, L, _ = x.shape
    if i < N_A:
        cat, state = _mixer_pool(i, x, st["hist"], st["mem_k"], st["mem_v"], W, pos0=st["pos0"],
                                 nb=st["nb"], tl=min(L, POOL_BLOCK))
        st["pool_states"].append(state[:, :, 1:, :])
        return cat
    past = st["past"]
    if i == N_A:
        kt_new, vt_new, lft_new, kt_b, v_b = _kv_proj(x, W, nb=st["nb"], tl=st["tl"])
        lft_all = lft_new if past is None else jnp.concatenate([past[2], lft_new], axis=2)
        lk = lft_all.shape[2]
        lk_pad = -(-lk // LANES) * LANES
        lf_t = jnp.pad(lft_all.reshape(B * FOX_HEADS, lk), ((0, 0), (0, lk_pad - lk)))
        f_t = _cumsum_lanes(lf_t).reshape(B, FOX_HEADS, lk_pad)
        st.update(kt_new=kt_new, vt_new=vt_new, lft_new=lft_new, kt_b=kt_b, v_b=v_b, f_t=f_t,
                  f_q=jnp.swapaxes(f_t[:, :, lk - L:lk], 1, 2))
    if past is None:
        return _mixer_fox_prompt(i, x, st["kt_b"], st["v_b"], st["f_t"], st["f_q"], st["mem_k"],
                                 st["mem_v"], W, tl=FOX_BLOCK)
    return _mixer_fox_sample(i, x, past[0], past[1], st["kt_new"], st["vt_new"], st["f_t"],
                             st["f_q"], st["mem_k"], st["mem_v"], W, nb=2)


def _stream_outputs(st):
    return (jnp.concatenate(st["pool_states"], axis=0), jnp.transpose(st["kt_new"], (0, 3, 1, 2)),
            jnp.transpose(st["vt_new"], (0, 3, 1, 2)), jnp.swapaxes(st["lft_new"], 1, 2))


def kernel(x_prompt, x_sample, state_pool, cache_fox_k, cache_fox_v, cache_fox_logf, cache_mem_k,
           cache_mem_v, mem_prompt, g_mix, w_in, w_out, q_norm_mem, g_mem, w_mem_kv, k_norm_mem,
           w_pool, pool_scale, q_norm_fox, g_kv, w_kv, k_norm_fox, b_f, g_ffn, w_gu, w_down):
    B, L, _ = x_prompt.shape
    SB, SL, _ = x_sample.shape
    W = dict(
        g_mix=g_mix.reshape(DEPTH, 1, D_MODEL), w_in=w_in.astype(BF16), w_out=w_out.astype(BF16),
        q_norm_mem=q_norm_mem.reshape(DEPTH, 1, MEM_HEAD_DIM), w_pool=w_pool.astype(BF16),
        pool_scale=pool_scale.reshape(N_A, 1, POOL_WIDTH),
        qnf_pair=jnp.tile(q_norm_fox, (1, 2)).reshape(DEPTH - N_A, 1, LANES),
        g_kv=g_kv.reshape(1, D_MODEL),
        w_kv_t=jnp.pad(w_kv.T, ((0, KV_ROWS - w_kv.shape[1]), (0, 0))).astype(BF16),
        w_v=w_kv[:, FOX_WIDTH:2 * FOX_WIDTH].astype(BF16),
        kn_col=k_norm_fox.reshape(1, FOX_HEAD_DIM, 1), b_f=b_f.reshape(FOX_HEADS, 1),
        g_ffn=g_ffn.reshape(DEPTH, 1, D_MODEL), w_gu=w_gu.astype(BF16), w_down=w_down.astype(BF16))

    mem_k_p, mem_v_p = _mem_kv(mem_prompt, g_mem.reshape(DEPTH, 1, D_MODEL), w_mem_kv.astype(BF16),
                               k_norm_mem.reshape(DEPTH, 1, MEM_HEAD_DIM))
    prompt = dict(pos0=0, hist=jnp.zeros((N_A, B, HIST_ROWS, POOL_WIDTH), F32), past=None,
                  mem_k=mem_k_p, mem_v=mem_v_p, nb=1, tl=ROW_BLOCK, pool_states=[])
    sample = dict(
        pos0=PAST_LEN, hist=jnp.pad(state_pool, ((0, 0), (0, 0), (1, 0), (0, 0))),
        past=(jnp.transpose(cache_fox_k, (0, 2, 3, 1)), jnp.transpose(cache_fox_v, (0, 2, 3, 1)),
              jnp.swapaxes(cache_fox_logf, 1, 2)),
        mem_k=cache_mem_k.reshape(DEPTH, SB, MEM_ROWS, MEM_HEAD_DIM),
        mem_v=cache_mem_v.reshape(DEPTH, SB, MEM_ROWS, MEM_HEAD_DIM),
        nb=ROW_BLOCK // SL, tl=SL, pool_states=[])

    y_p, y_s = x_prompt, x_sample
    for i in range(DEPTH):
        cat_p = _mixer(i, y_p, prompt, W)
        cat_s = _mixer(i, y_s, sample, W)
        y_p, y_s = _out_ffn(i, y_p.reshape(B * L, D_MODEL), cat_p.reshape(B * L, MIX_WIDTH),
                            y_s.reshape(SB * SL, D_MODEL), cat_s.reshape(SB * SL, MIX_WIDTH), W)
        y_p = y_p.reshape(B, L, D_MODEL)
        y_s = y_s.reshape(SB, SL, D_MODEL)

    pool_p, fox_k_p, fox_v_p, fox_lf_p = _stream_outputs(prompt)
    pool_s, fox_k_s, fox_v_s, fox_lf_s = _stream_outputs(sample)
    mem_shape = (DEPTH, B, MEM_TOKENS, MEM_HEADS, MEM_HEAD_DIM)
    return (y_p, y_s, pool_p, fox_k_p, fox_v_p, fox_lf_p, mem_k_p.reshape(mem_shape),
            mem_v_p.reshape(mem_shape), pool_s, fox_k_s, fox_v_s, fox_lf_s)
```

```python
import functools

import jax
import jax.numpy as jnp
from jax import lax
from jax.experimental import pallas as pl
from jax.experimental.pallas import tpu as pltpu

F32 = jnp.float32
BF16 = jnp.bfloat16

D_MODEL = 1024
DEPTH = 4
N_A = DEPTH // 2
PAST_LEN = 1024
POOL_WINDOWS = (2, 4, 8, 16)
POOL_GROUPS = len(POOL_WINDOWS)
POOL_WIDTH = D_MODEL // 2
POOL_GROUP_DIM = POOL_WIDTH // POOL_GROUPS
POOL_HIST = max(POOL_WINDOWS) - 1
HIST_ROWS = POOL_HIST + 1
FOX_HEAD_DIM = 64
FOX_WIDTH = D_MODEL // 2
FOX_HEADS = FOX_WIDTH // FOX_HEAD_DIM
MEM_TOKENS = 256
MEM_HEADS = 4
MEM_WIDTH = D_MODEL // 2
MEM_HEAD_DIM = MEM_WIDTH // MEM_HEADS
MIX_WIDTH = POOL_WIDTH + MEM_WIDTH
D_FF = ((8 * D_MODEL // 3 + 255) // 256) * 256
EPS = 1e-6
FOX_SCALE = FOX_HEAD_DIM ** -0.5
MEM_SCALE = MEM_HEAD_DIM ** -0.5
LOG2E = 1.4426950408889634
AUG_ROWS = 16

LANES = 128
ROW_BLOCK = 512
FOX_BLOCK = 512
POOL_BLOCK = 1024
KV_ROWS = 2 * FOX_WIDTH + 16
MXU_DIM = 256
FFN_SPLITS = ((0, 6 * MXU_DIM), (6 * MXU_DIM, D_FF))
VMEM_LIMIT = 56 * 1024 * 1024


def _dot(a, b):
    return jnp.dot(a, b, preferred_element_type=F32)


def _dot_nt(a, b):
    return lax.dot_general(a, b, (((1,), (1,)), ((), ())), preferred_element_type=F32)


def _rms(x, g):
    ms = jnp.mean(x * x, axis=-1, keepdims=True)
    return (x * lax.rsqrt(ms + EPS)) * g


def _rms_head64(x, g_pair):
    lo = lax.broadcasted_iota(jnp.int32, (1, LANES), 1) < FOX_HEAD_DIM
    outs = []
    for c in range(x.shape[-1] // LANES):
        xc = x[:, c * LANES:(c + 1) * LANES]
        sq = xc * xc
        s_lo = jnp.sum(jnp.where(lo, sq, 0.0), axis=-1, keepdims=True)
        s_hi = jnp.sum(jnp.where(lo, 0.0, sq), axis=-1, keepdims=True)
        ms = jnp.where(lo, s_lo, s_hi) * (1.0 / FOX_HEAD_DIM)
        outs.append((xc * lax.rsqrt(ms + EPS)) * g_pair)
    return outs


def _const_spec(shape):
    return pl.BlockSpec(shape, lambda *_: (0,) * len(shape), pipeline_mode=pl.Buffered(1))


def _layer_spec(layer, shape):
    return pl.BlockSpec((1,) + shape, lambda *_: (layer,) + (0,) * len(shape),
                        pipeline_mode=pl.Buffered(1))


MEM_ROWS = MEM_TOKENS * MEM_HEADS


def _mem_spec(layer, nb):
    return pl.BlockSpec((1, nb, MEM_ROWS, MEM_HEAD_DIM), lambda b, *_: (layer, b, 0, 0))


def _head_rows(h):
    return pl.ds(h, MEM_TOKENS, stride=MEM_HEADS)


def _params(n_grid, flags=None):
    return pltpu.CompilerParams(
        dimension_semantics=("arbitrary",) * n_grid, vmem_limit_bytes=VMEM_LIMIT, flags=flags)


def _mem_kv_kernel(mem_ref, g_ref, w_ref, kn_ref, k_ref, v_ref):
    nb = mem_ref.shape[0]
    x = mem_ref[...].reshape(nb * MEM_TOKENS, D_MODEL)
    kv = _dot(_rms(x, g_ref[0]).astype(BF16), w_ref[0])
    for h in range(MEM_HEADS):
        ks = slice(h * MEM_HEAD_DIM, (h + 1) * MEM_HEAD_DIM)
        vs = slice(MEM_WIDTH + h * MEM_HEAD_DIM, MEM_WIDTH + (h + 1) * MEM_HEAD_DIM)
        k_ref[0, :, _head_rows(h), :] = _rms(kv[:, ks], kn_ref[0]).reshape(nb, MEM_TOKENS, MEM_HEAD_DIM)
        v_ref[0, :, _head_rows(h), :] = kv[:, vs].reshape(nb, MEM_TOKENS, MEM_HEAD_DIM)


def _mem_kv(mem, g_mem, w_mem_kv, k_norm_mem):
    B = mem.shape[0]
    nb = 4
    out = jax.ShapeDtypeStruct((DEPTH, B, MEM_ROWS, MEM_HEAD_DIM), F32)
    out_spec = pl.BlockSpec((1, nb, MEM_ROWS, MEM_HEAD_DIM), lambda i, b: (i, b, 0, 0))
    return pl.pallas_call(
        _mem_kv_kernel,
        grid=(DEPTH, B // nb),
        in_specs=[
            pl.BlockSpec((nb, MEM_TOKENS, D_MODEL), lambda i, b: (b, 0, 0)),
            pl.BlockSpec((1, 1, D_MODEL), lambda i, b: (i, 0, 0)),
            pl.BlockSpec((1, D_MODEL, 2 * MEM_WIDTH), lambda i, b: (i, 0, 0)),
            pl.BlockSpec((1, 1, MEM_HEAD_DIM), lambda i, b: (i, 0, 0)),
        ],
        out_specs=[out_spec, out_spec],
        out_shape=[out, out],
        compiler_params=_params(2),
        name="mem_kv",
    )(mem, g_mem, w_mem_kv, k_norm_mem)


def _in_proj(x_ref, g_ref, w_ref):
    nb, tl, _ = x_ref.shape
    x = x_ref[...].reshape(nb * tl, D_MODEL)
    return _dot(_rms(x, g_ref[0]).astype(BF16), w_ref[0])


def _mem_attend(zq, qn, mk_ref, mv_ref, i, cat_ref, col0):
    for h in range(MEM_HEADS):
        sl = slice(h * MEM_HEAD_DIM, (h + 1) * MEM_HEAD_DIM)
        q = _rms(zq[:, sl], qn).astype(BF16)
        s = _dot_nt(q, mk_ref[0, i, _head_rows(h), :].astype(BF16)) * (MEM_SCALE * LOG2E)
        p = jnp.exp2(s - jnp.max(s, axis=-1, keepdims=True)).astype(BF16)
        v = mv_ref[0, i, _head_rows(h), :].astype(BF16)
        o = _dot(p, jnp.concatenate([v, jnp.ones_like(v)], axis=-1))
        o = o[:, :MEM_HEAD_DIM] / o[:, MEM_HEAD_DIM:]
        cat_ref[i, :, col0 + h * MEM_HEAD_DIM:col0 + (h + 1) * MEM_HEAD_DIM] = o.astype(cat_ref.dtype)


def _mem_attend_paired(zq, qn, mk_ref, mv_ref, i, cat_ref, col0):
    z = jnp.zeros((MEM_TOKENS, MEM_HEAD_DIM), BF16)
    one = jnp.ones((MEM_TOKENS, MEM_HEAD_DIM), BF16)
    for c in range(MEM_HEADS // 2):
        hs = (2 * c, 2 * c + 1)
        q = jnp.concatenate(
            [_rms(zq[:, h * MEM_HEAD_DIM:(h + 1) * MEM_HEAD_DIM], qn) for h in hs], axis=1)
        k0, k1 = (mk_ref[0, i, _head_rows(h), :].astype(BF16) for h in hs)
        v0, v1 = (mv_ref[0, i, _head_rows(h), :].astype(BF16) for h in hs)
        k_pair = jnp.concatenate([jnp.concatenate([k0, z], axis=1),
                                  jnp.concatenate([z, k1], axis=1)], axis=0)
        v_pair = jnp.concatenate([jnp.concatenate([v0, one, z, z], axis=1),
                                  jnp.concatenate([z, z, v1, one], axis=1)], axis=0)
        s = _dot_nt(q.astype(BF16), k_pair) * (MEM_SCALE * LOG2E)
        p = jnp.concatenate(
            [jnp.exp2(sh - jnp.max(sh, axis=-1, keepdims=True))
             for sh in (s[:, :MEM_TOKENS], s[:, MEM_TOKENS:])], axis=1).astype(BF16)
        o = _dot(p, v_pair)
        for n, h in enumerate(hs):
            oh = o[:, 2 * n * MEM_HEAD_DIM:(2 * n + 1) * MEM_HEAD_DIM]
            lh = o[:, (2 * n + 1) * MEM_HEAD_DIM:(2 * n + 2) * MEM_HEAD_DIM]
            cat_ref[i, :, col0 + h * MEM_HEAD_DIM:col0 + (h + 1) * MEM_HEAD_DIM] = (
                (oh / lh).astype(cat_ref.dtype))


def _mixer_pool_kernel(x_ref, g_ref, w_in_ref, qn_ref, mk_ref, mv_ref, hist_ref, wp_ref, ps_ref,
                       cat_ref, state_ref, ubuf, *, pos0):
    nb, tl, _ = x_ref.shape
    j = pl.program_id(1)
    z = _in_proj(x_ref, g_ref, w_in_ref)

    @pl.when(j == 0)
    def _():
        ubuf[:, 0:HIST_ROWS, :] = hist_ref[0]

    pos = pos0 + j * tl + lax.broadcasted_iota(jnp.int32, (tl, 1), 0)
    for i in range(nb):
        zi = z[i * tl:(i + 1) * tl]
        u = zi[:, :POOL_WIDTH]
        ubuf[i, HIST_ROWS:HIST_ROWS + tl, :] = u
        for g, w in enumerate(POOL_WINDOWS):
            sl = slice(g * POOL_GROUP_DIM, (g + 1) * POOL_GROUP_DIM)
            ug = u[:, sl]
            acc = ug
            for k in range(1, w):
                acc = acc + ubuf[i, HIST_ROWS - k:HIST_ROWS - k + tl, sl]
            cnt = jnp.minimum(pos + 1, w).astype(F32)
            d = acc / cnt - ug
            y = _dot(d.astype(BF16), wp_ref[0, g]) * ps_ref[0, :, sl]
            cat_ref[i, :, sl] = y.astype(cat_ref.dtype)
        _mem_attend(zi[:, POOL_WIDTH:], qn_ref[0], mk_ref, mv_ref, i, cat_ref, POOL_WIDTH)
        tail = ubuf[i, tl:tl + HIST_ROWS, :]
        state_ref[0, i] = tail
        ubuf[i, 0:HIST_ROWS, :] = tail


def _mixer_pool(layer, x, hist, mem_k, mem_v, W, *, pos0, nb, tl):
    B, L, _ = x.shape
    assert tl >= HIST_ROWS and L % tl == 0 and B % nb == 0
    hist_spec = pl.BlockSpec((1, nb, HIST_ROWS, POOL_WIDTH), lambda b, j: (layer, b, 0, 0))
    return pl.pallas_call(
        functools.partial(_mixer_pool_kernel, pos0=pos0),
        grid=(B // nb, L // tl),
        in_specs=[
            pl.BlockSpec((nb, tl, D_MODEL), lambda b, j: (b, j, 0)),
            _layer_spec(layer, (1, D_MODEL)),
            _layer_spec(layer, (D_MODEL, MIX_WIDTH)),
            _layer_spec(layer, (1, MEM_HEAD_DIM)),
            _mem_spec(layer, nb),
            _mem_spec(layer, nb),
            hist_spec,
            _layer_spec(layer, (POOL_GROUPS, POOL_GROUP_DIM, POOL_GROUP_DIM)),
            _layer_spec(layer, (1, POOL_WIDTH)),
        ],
        out_specs=[
            pl.BlockSpec((nb, tl, MIX_WIDTH), lambda b, j: (b, j, 0)),
            pl.BlockSpec((1, nb, HIST_ROWS, POOL_WIDTH), lambda b, j: (0, b, 0, 0)),
        ],
        out_shape=[
            jax.ShapeDtypeStruct((B, L, MIX_WIDTH), BF16),
            jax.ShapeDtypeStruct((1, B, HIST_ROWS, POOL_WIDTH), F32),
        ],
        scratch_shapes=[pltpu.VMEM((nb, HIST_ROWS + tl, POOL_WIDTH), F32)],
        compiler_params=_params(2),
        name="mixer_pool",
    )(x, W["g_mix"], W["w_in"], W["q_norm_mem"], mem_k, mem_v, hist, W["w_pool"], W["pool_scale"])


def _kv_proj_kernel(x_ref, g_ref, w_ref, wv_ref, kn_ref, bf_ref, k_ref, v_ref, lf_ref, kb_ref, vb_ref):
    nb, tl, _ = x_ref.shape
    rows = nb * tl
    x = x_ref[...].reshape(rows, D_MODEL)
    xn = _rms(x, g_ref[...]).astype(BF16)
    zt = _dot_nt(w_ref[...], xn)
    zv = _dot(xn, wv_ref[...])
    lo = lax.broadcasted_iota(jnp.int32, (1, LANES), 1) < FOX_HEAD_DIM
    for h in range(FOX_HEADS):
        pair = zv[:, (h // 2) * LANES:(h // 2 + 1) * LANES]
        base = pair if h % 2 == 0 else pltpu.roll(pair, FOX_HEAD_DIM, 1)
        vb_ref[:, h] = jnp.where(lo, base, 1.0).astype(BF16).reshape(nb, tl, LANES)
    k3 = zt[:FOX_WIDTH].reshape(FOX_HEADS, FOX_HEAD_DIM, rows)
    ms = jnp.mean(k3 * k3, axis=1, keepdims=True)
    k3 = (k3 * lax.rsqrt(ms + EPS)) * kn_ref[...]
    v3 = zt[FOX_WIDTH:2 * FOX_WIDTH].reshape(FOX_HEADS, FOX_HEAD_DIM, rows)
    t = -(zt[2 * FOX_WIDTH:2 * FOX_WIDTH + FOX_HEADS] + bf_ref[...])
    lf = -(jnp.maximum(t, 0.0) + jnp.log1p(jnp.exp(-jnp.abs(t))))
    for i in range(nb):
        cols = slice(i * tl, (i + 1) * tl)
        k_ref[i] = k3[:, :, cols]
        v_ref[i] = v3[:, :, cols]
        lf_ref[i] = lf[:, cols]
        kb_ref[i] = k3[:, :, cols].astype(BF16)


def _kv_proj(x, W, *, nb, tl):
    B, L, _ = x.shape
    hd = pl.BlockSpec((nb, FOX_HEADS, FOX_HEAD_DIM, tl), lambda b, j: (b, 0, 0, j))
    heads = jax.ShapeDtypeStruct((B, FOX_HEADS, FOX_HEAD_DIM, L), F32)
    return pl.pallas_call(
        _kv_proj_kernel,
        grid=(B // nb, L // tl),
        in_specs=[
            pl.BlockSpec((nb, tl, D_MODEL), lambda b, j: (b, j, 0)),
            _const_spec((1, D_MODEL)),
            _const_spec((KV_ROWS, D_MODEL)),
            _const_spec((D_MODEL, FOX_WIDTH)),
            _const_spec((1, FOX_HEAD_DIM, 1)),
            _const_spec((FOX_HEADS, 1)),
        ],
        out_specs=[hd, hd, pl.BlockSpec((nb, FOX_HEADS, tl), lambda b, j: (b, 0, j)), hd,
                   pl.BlockSpec((nb, FOX_HEADS, tl, LANES), lambda b, j: (b, 0, j, 0))],
        out_shape=[heads, heads, jax.ShapeDtypeStruct((B, FOX_HEADS, L), F32),
                   jax.ShapeDtypeStruct((B, FOX_HEADS, FOX_HEAD_DIM, L), BF16),
                   jax.ShapeDtypeStruct((B, FOX_HEADS, L, LANES), BF16)],
        compiler_params=_params(2),
        name="kv_proj",
    )(x, W["g_kv"], W["w_kv_t"], W["w_v"], W["kn_col"], W["b_f"])


def _cumsum_kernel(lf_ref, f_ref):
    rows, n = lf_ref.shape
    r = lax.broadcasted_iota(jnp.int32, (LANES, LANES), 0)
    c = lax.broadcasted_iota(jnp.int32, (LANES, LANES), 1)
    tri = jnp.where(r <= c, 1.0, 0.0).astype(BF16)
    carry = jnp.zeros((rows, 1), F32)
    for ch in range(n // LANES):
        x = lf_ref[:, ch * LANES:(ch + 1) * LANES]
        hi = x.astype(BF16)
        r1 = x - hi.astype(F32)
        mid = r1.astype(BF16)
        low = (r1 - mid.astype(F32)).astype(BF16)
        y = (_dot(hi, tri) + _dot(mid, tri)) + _dot(low, tri) + carry
        f_ref[:, ch * LANES:(ch + 1) * LANES] = y
        carry = y[:, LANES - 1:LANES]


def _cumsum_lanes(lf_t):
    return pl.pallas_call(
        _cumsum_kernel,
        out_shape=jax.ShapeDtypeStruct(lf_t.shape, F32),
        name="logf_cumsum",
    )(lf_t)


def _tri_unrank(t, n):
    row = sum((t >= k * (k + 1) // 2).astype(jnp.int32) for k in range(1, n))
    return row, t - row * (row + 1) // 2


def _split3(x):
    hi = x.astype(BF16).astype(F32)
    r1 = x - hi
    mid = r1.astype(BF16).astype(F32)
    low = (r1 - mid).astype(BF16).astype(F32)
    return hi, mid, low


def _mixer_fox_prompt_kernel(x_ref, g_ref, w_in_ref, qn_ref, mk_ref, mv_ref, qnf_ref,
                             kt_ref, v_ref, ft_ref, fq_ref, cat_ref, q_sc, m_sc, acc_sc, mask_sc,
                             *, nkb):
    _, tl, _ = x_ref.shape
    tk = kt_ref.shape[-1]
    j, kb = _tri_unrank(pl.program_id(1), nkb)
    lane = lax.broadcasted_iota(jnp.int32, (1, LANES), 1)
    row = lax.broadcasted_iota(jnp.int32, (AUG_ROWS, 1), 0)

    @pl.when(kb == 0)
    def _():
        z = _in_proj(x_ref, g_ref, w_in_ref)
        _mem_attend_paired(z[:, FOX_WIDTH:], qn_ref[0], mk_ref, mv_ref, 0, cat_ref, FOX_WIDTH)
        qs = _rms_head64(z[:, :FOX_WIDTH], qnf_ref[0])
        for h in range(FOX_HEADS):
            base = qs[h // 2] if h % 2 == 0 else pltpu.roll(qs[h // 2], FOX_HEAD_DIM, 1)
            hi, mid, low = _split3(fq_ref[0, :, h:h + 1] * LOG2E)
            tail = jnp.where(lane < FOX_HEAD_DIM + 3, 1.0,
                             jnp.where(lane == FOX_HEAD_DIM + 3, hi,
                                       jnp.where(lane == FOX_HEAD_DIM + 4, mid,
                                                 jnp.where(lane == FOX_HEAD_DIM + 5, low, 0.0))))
            q_sc[h] = jnp.where(lane < FOX_HEAD_DIM, base * (FOX_SCALE * LOG2E), tail).astype(BF16)
        m_sc[...] = jnp.full(m_sc.shape, -jnp.inf, F32)
        acc_sc[...] = jnp.zeros(acc_sc.shape, F32)
        causal = (lax.broadcasted_iota(jnp.int32, (tl, 1), 0)
                  >= lax.broadcasted_iota(jnp.int32, (1, tk), 1))
        mask_sc[...] = jnp.where(causal, 0.0, -jnp.inf)

    def k_aug(h):
        hi, mid, low = _split3(ft_ref[0, 0, h:h + 1, :] * (-LOG2E))
        aug = jnp.where(row == 0, hi, jnp.where(row == 1, mid, jnp.where(row == 2, low,
                        jnp.where(row < 6, 1.0, 0.0)))).astype(BF16)
        pad = jnp.zeros((FOX_HEAD_DIM - AUG_ROWS, tk), BF16)
        return jnp.concatenate([kt_ref[0, h], aug, pad], axis=0)

    def attend(masked):
        zk = jnp.zeros((LANES, tk), BF16)
        zv = jnp.zeros((tk, LANES), BF16)
        for c in range(FOX_HEADS // 2):
            h0, h1 = 2 * c, 2 * c + 1
            k_pair = jnp.concatenate([jnp.concatenate([k_aug(h0), zk], axis=1),
                                      jnp.concatenate([zk, k_aug(h1)], axis=1)], axis=0)
            v_pair = jnp.concatenate([jnp.concatenate([v_ref[0, h0], zv], axis=1),
                                      jnp.concatenate([zv, v_ref[0, h1]], axis=1)], axis=0)
            q_pair = jnp.concatenate([q_sc[h0], q_sc[h1]], axis=1)
            s = _dot(q_pair, k_pair)
            ps, alphas = [], []
            for h, sh in ((h0, s[:, :tk]), (h1, s[:, tk:])):
                if masked:
                    sh = sh + mask_sc[...]
                m_old = m_sc[h]
                m_new = jnp.maximum(m_old, jnp.max(sh, axis=-1, keepdims=True))
                ps.append(jnp.exp2(sh - m_new[:, 0:1]).astype(BF16))
                alphas.append(jnp.exp2(m_old - m_new))
                m_sc[h] = m_new
            o = _dot(jnp.concatenate(ps, axis=1), v_pair)
            acc_sc[h0] = alphas[0] * acc_sc[h0] + o[:, :LANES]
            acc_sc[h1] = alphas[1] * acc_sc[h1] + o[:, LANES:]

    @pl.when(kb < j)
    def _():
        attend(False)

    @pl.when(kb == j)
    def _():
        attend(True)
        for c in range(FOX_HEADS // 2):
            a0 = acc_sc[2 * c]
            a1 = acc_sc[2 * c + 1]
            o0 = a0 / pltpu.roll(a0, FOX_HEAD_DIM, 1)
            o1 = pltpu.roll(a1, FOX_HEAD_DIM, 1) / a1
            cat_ref[0, :, c * LANES:(c + 1) * LANES] = (
                jnp.where(lane < FOX_HEAD_DIM, o0, o1).astype(cat_ref.dtype))


def _mixer_fox_prompt(layer, x, kt_b, vt_b, f_t, f_q, mem_k, mem_v, W, *, tl):
    B, L, _ = x.shape
    nkb = L // tl
    ft4 = jnp.swapaxes(f_t.reshape(B, FOX_HEADS, nkb, tl), 1, 2)
    jj = lambda t: _tri_unrank(t, nkb)[0]
    kk = lambda t: _tri_unrank(t, nkb)[1]
    kt_spec = pl.BlockSpec((1, FOX_HEADS, FOX_HEAD_DIM, tl), lambda b, t: (b, 0, 0, kk(t)))
    v_spec = pl.BlockSpec((1, FOX_HEADS, tl, LANES), lambda b, t: (b, 0, kk(t), 0))
    return pl.pallas_call(
        functools.partial(_mixer_fox_prompt_kernel, nkb=nkb),
        grid=(B, nkb * (nkb + 1) // 2),
        in_specs=[
            pl.BlockSpec((1, tl, D_MODEL), lambda b, t: (b, jj(t), 0)),
            _layer_spec(layer, (1, D_MODEL)),
            _layer_spec(layer, (D_MODEL, MIX_WIDTH)),
            _layer_spec(layer, (1, MEM_HEAD_DIM)),
            _mem_spec(layer, 1),
            _mem_spec(layer, 1),
            _layer_spec(layer - N_A, (1, LANES)),
            kt_spec,
            v_spec,
            pl.BlockSpec((1, 1, FOX_HEADS, tl), lambda b, t: (b, kk(t), 0, 0)),
            pl.BlockSpec((1, tl, FOX_HEADS), lambda b, t: (b, jj(t), 0)),
        ],
        out_specs=pl.BlockSpec((1, tl, MIX_WIDTH), lambda b, t: (b, jj(t), 0)),
        out_shape=jax.ShapeDtypeStruct((B, L, MIX_WIDTH), BF16),
        scratch_shapes=[
            pltpu.VMEM((FOX_HEADS, tl, LANES), BF16),
            pltpu.VMEM((FOX_HEADS, tl, LANES), F32),
            pltpu.VMEM((FOX_HEADS, tl, LANES), F32),
            pltpu.VMEM((tl, tl), F32),
        ],
        compiler_params=_params(2),
        name="mixer_fox_prompt",
    )(x, W["g_mix"], W["w_in"], W["q_norm_mem"], mem_k, mem_v, W["qnf_pair"], kt_b, vt_b, ft4, f_q)


def _block_diag(blocks):
    a, b = blocks
    z = jnp.zeros_like(a)
    return jnp.concatenate([jnp.concatenate([a, z], axis=1),
                            jnp.concatenate([z, b], axis=1)], axis=0)


def _fox_sample_pre_kernel(x_ref, g_ref, w_in_ref, qn_ref, mk_ref, mv_ref, qnf_ref, q_ref, cm_ref):
    nb, tl, _ = x_ref.shape
    z = _in_proj(x_ref, g_ref, w_in_ref)
    for i in range(nb):
        zi = z[i * tl:(i + 1) * tl]
        qs = _rms_head64(zi[:, :FOX_WIDTH], qnf_ref[0])
        for c in range(FOX_HEADS // 2):
            q_ref[i, :, c * LANES:(c + 1) * LANES] = (qs[c] * FOX_SCALE).astype(BF16)
        _mem_attend(zi[:, FOX_WIDTH:], qn_ref[0], mk_ref, mv_ref, i, cm_ref, 0)


def _fox_sample_attn_kernel(q_ref, ktp_ref, vtp_ref, ktn_ref, vtn_ref, ft_ref, fq_ref, cat_ref):
    nb, tl, _ = q_ref.shape
    past = ktp_ref.shape[-1]
    causal = (lax.broadcasted_iota(jnp.int32, (tl, 1), 0)
              >= lax.broadcasted_iota(jnp.int32, (1, tl), 1))
    lo = lax.broadcasted_iota(jnp.int32, (1, LANES), 1) < FOX_HEAD_DIM
    for i in range(nb):
        for c in range(FOX_HEADS // 2):
            hs = (2 * c, 2 * c + 1)
            q = q_ref[i, :, c * LANES:(c + 1) * LANES]
            s_p = _dot(q, _block_diag([ktp_ref[i, h].astype(BF16) for h in hs]))
            s_n = _dot(q, _block_diag([ktn_ref[i, h].astype(BF16) for h in hs]))
            pp, pn, inv_l = [], [], []
            for n, h in enumerate(hs):
                fq = fq_ref[i, :, h:h + 1]
                sp = (s_p[:, n * past:(n + 1) * past] + fq) - ft_ref[i, h:h + 1, 0:past]
                sn = (s_n[:, n * tl:(n + 1) * tl] + fq) - ft_ref[i, h:h + 1, past:past + tl]
                sn = jnp.where(causal, sn, -jnp.inf)
                m = jnp.maximum(jnp.max(sp, axis=-1, keepdims=True),
                                jnp.max(sn, axis=-1, keepdims=True))
                ep = jnp.exp(sp - m)
                en = jnp.exp(sn - m)
                inv_l.append(1.0 / (jnp.sum(ep, axis=-1, keepdims=True)
                                    + jnp.sum(en, axis=-1, keepdims=True)))
                pp.append(ep.astype(BF16))
                pn.append(en.astype(BF16))
            o = (_dot_nt(jnp.concatenate(pp, axis=1),
                         _block_diag([vtp_ref[i, h].astype(BF16) for h in hs]))
                 + _dot_nt(jnp.concatenate(pn, axis=1),
                           _block_diag([vtn_ref[i, h].astype(BF16) for h in hs])))
            scale = jnp.where(lo, inv_l[0], inv_l[1])
            cat_ref[i, :, c * LANES:(c + 1) * LANES] = (o * scale).astype(cat_ref.dtype)


def _mixer_fox_sample(layer, x, kt_past, vt_past, kt_new, vt_new, f_t, f_q, mem_k, mem_v, W, *,
                      nb_dense, nb_attn):
    B, L, _ = x.shape
    past = kt_past.shape[-1]
    lk_pad = f_t.shape[-1]
    blk = lambda nb, *tail: pl.BlockSpec((nb,) + tail, lambda b: (b,) + (0,) * len(tail))
    half = jax.ShapeDtypeStruct((B, L, FOX_WIDTH), BF16)
    q, cat_mem = pl.pallas_call(
        _fox_sample_pre_kernel,
        grid=(B // nb_dense,),
        in_specs=[
            blk(nb_dense, L, D_MODEL),
            _layer_spec(layer, (1, D_MODEL)),
            _layer_spec(layer, (D_MODEL, MIX_WIDTH)),
            _layer_spec(layer, (1, MEM_HEAD_DIM)),
            _mem_spec(layer, nb_dense),
            _mem_spec(layer, nb_dense),
            _layer_spec(layer - N_A, (1, LANES)),
        ],
        out_specs=[blk(nb_dense, L, FOX_WIDTH), blk(nb_dense, L, MEM_WIDTH)],
        out_shape=[half, half],
        compiler_params=_params(1),
        name="fox_sample_pre",
    )(x, W["g_mix"], W["w_in"], W["q_norm_mem"], mem_k, mem_v, W["qnf_pair"])
    cat_fox = pl.pallas_call(
        _fox_sample_attn_kernel,
        grid=(B // nb_attn,),
        in_specs=[
            blk(nb_attn, L, FOX_WIDTH),
            blk(nb_attn, FOX_HEADS, FOX_HEAD_DIM, past),
            blk(nb_attn, FOX_HEADS, FOX_HEAD_DIM, past),
            blk(nb_attn, FOX_HEADS, FOX_HEAD_DIM, L),
            blk(nb_attn, FOX_HEADS, FOX_HEAD_DIM, L),
            blk(nb_attn, FOX_HEADS, lk_pad),
            blk(nb_attn, L, FOX_HEADS),
        ],
        out_specs=blk(nb_attn, L, FOX_WIDTH),
        out_shape=half,
        compiler_params=_params(1),
        name="fox_sample_attn",
    )(q, kt_past, vt_past, kt_new, vt_new, f_t, f_q)
    return jnp.concatenate([cat_fox, cat_mem], axis=-1)


def _out_ffn_kernel(xa_ref, cata_ref, xb_ref, catb_ref, w_out_ref, g_ref, w_gu_ref, w_down_ref,
                    ya_ref, yb_ref, *, na):
    def body(x_ref, cat_ref, y_ref):
        x1 = x_ref[...] + _dot(cat_ref[...], w_out_ref[0])
        xn = _rms(x1, g_ref[0]).astype(BF16)
        acc = x1
        for lo, hi in FFN_SPLITS:
            gate = _dot(xn, w_gu_ref[0, :, lo:hi])
            up = _dot(xn, w_gu_ref[0, :, D_FF + lo:D_FF + hi])
            h = (gate * (1.0 / (1.0 + jnp.exp(-gate)))) * up
            acc = acc + _dot(h.astype(BF16), w_down_ref[0, lo:hi, :])
        y_ref[...] = acc

    @pl.when(pl.program_id(0) < na)
    def _():
        body(xa_ref, cata_ref, ya_ref)

    @pl.when(pl.program_id(0) >= na)
    def _():
        body(xb_ref, catb_ref, yb_ref)


def _out_ffn(layer, xa, cata, xb, catb, W):
    na = xa.shape[0] // ROW_BLOCK
    nb = xb.shape[0] // ROW_BLOCK
    a_map = lambda r: (jnp.minimum(r, na - 1), 0)
    b_map = lambda r: (jnp.maximum(r - na, 0), 0)
    return pl.pallas_call(
        functools.partial(_out_ffn_kernel, na=na),
        grid=(na + nb,),
        in_specs=[
            pl.BlockSpec((ROW_BLOCK, D_MODEL), a_map),
            pl.BlockSpec((ROW_BLOCK, MIX_WIDTH), a_map),
            pl.BlockSpec((ROW_BLOCK, D_MODEL), b_map),
            pl.BlockSpec((ROW_BLOCK, MIX_WIDTH), b_map),
            _layer_spec(layer, (MIX_WIDTH, D_MODEL)),
            _layer_spec(layer, (1, D_MODEL)),
            _layer_spec(layer, (D_MODEL, 2 * D_FF)),
            _layer_spec(layer, (D_FF, D_MODEL)),
        ],
        out_specs=[pl.BlockSpec((ROW_BLOCK, D_MODEL), a_map),
                   pl.BlockSpec((ROW_BLOCK, D_MODEL), b_map)],
        out_shape=[jax.ShapeDtypeStruct(xa.shape, F32), jax.ShapeDtypeStruct(xb.shape, F32)],
        compiler_params=_params(1),
        name="out_ffn",
    )(xa, cata, xb, catb, W["w_out"], W["g_ffn"], W["w_gu"], W["w_down"])


def _mixer(i, x, st, W):
    B, L, _ = x.shape
    if i < N_A:
        cat, state = _mixer_pool(i, x, st["hist"], st["mem_k"], st["mem_v"], W, pos0=st["pos0"],
                                 nb=st["nb"], tl=min(L, POOL_BLOCK))
        st["pool_states"].append(state[:, :, 1:, :])
        return cat
    past = st["past"]
    if i == N_A:
        kt_new, vt_new, lft_new, kt_b, v_b = _kv_proj(x, W, nb=st["nb"], tl=st["tl"])
        lft_all = lft_new if past is None else jnp.concatenate([past[2], lft_new], axis=2)
        lk = lft_all.shape[2]
        lk_pad = -(-lk // LANES) * LANES
        lf_t = jnp.pad(lft_all.reshape(B * FOX_HEADS, lk), ((0, 0), (0, lk_pad - lk)))
        f_t = _cumsum_lanes(lf_t).reshape(B, FOX_HEADS, lk_pad)
        st.update(kt_new=kt_new, vt_new=vt_new, lft_new=lft_new, kt_b=kt_b, v_b=v_b, f_t=f_t,
                  f_q=jnp.swapaxes(f_t[:, :, lk - L:lk], 1, 2))
    if past is None:
        return _mixer_fox_prompt(i, x, st["kt_b"], st["v_b"], st["f_t"], st["f_q"], st["mem_k"],
                                 st["mem_v"], W, tl=FOX_BLOCK)
    return _mixer_fox_sample(i, x, past[0], past[1], st["kt_new"], st["vt_new"], st["f_t"],
                             st["f_q"], st["mem_k"], st["mem_v"], W, nb_dense=st["nb"], nb_attn=4)


def _stream_outputs(st):
    return (jnp.concatenate(st["pool_states"], axis=0), jnp.transpose(st["kt_new"], (0, 3, 1, 2)),
            jnp.transpose(st["vt_new"], (0, 3, 1, 2)), jnp.swapaxes(st["lft_new"], 1, 2))


def kernel(x_prompt, x_sample, state_pool, cache_fox_k, cache_fox_v, cache_fox_logf, cache_mem_k,
           cache_mem_v, mem_prompt, g_mix, w_in, w_out, q_norm_mem, g_mem, w_mem_kv, k_norm_mem,
           w_pool, pool_scale, q_norm_fox, g_kv, w_kv, k_norm_fox, b_f, g_ffn, w_gu, w_down):
    B, L, _ = x_prompt.shape
    SB, SL, _ = x_sample.shape
    W = dict(
        g_mix=g_mix.reshape(DEPTH, 1, D_MODEL), w_in=w_in.astype(BF16), w_out=w_out.astype(BF16),
        q_norm_mem=q_norm_mem.reshape(DEPTH, 1, MEM_HEAD_DIM), w_pool=w_pool.astype(BF16),
        pool_scale=pool_scale.reshape(N_A, 1, POOL_WIDTH),
        qnf_pair=jnp.tile(q_norm_fox, (1, 2)).reshape(DEPTH - N_A, 1, LANES),
        g_kv=g_kv.reshape(1, D_MODEL),
        w_kv_t=jnp.pad(w_kv.T, ((0, KV_ROWS - w_kv.shape[1]), (0, 0))).astype(BF16),
        w_v=w_kv[:, FOX_WIDTH:2 * FOX_WIDTH].astype(BF16),
        kn_col=k_norm_fox.reshape(1, FOX_HEAD_DIM, 1), b_f=b_f.reshape(FOX_HEADS, 1),
        g_ffn=g_ffn.reshape(DEPTH, 1, D_MODEL), w_gu=w_gu.astype(BF16), w_down=w_down.astype(BF16))

    mem_k_p, mem_v_p = _mem_kv(mem_prompt, g_mem.reshape(DEPTH, 1, D_MODEL), w_mem_kv.astype(BF16),
                               k_norm_mem.reshape(DEPTH, 1, MEM_HEAD_DIM))
    prompt = dict(pos0=0, hist=jnp.zeros((N_A, B, HIST_ROWS, POOL_WIDTH), F32), past=None,
                  mem_k=mem_k_p, mem_v=mem_v_p, nb=1, tl=ROW_BLOCK, pool_states=[])
    sample = dict(
        pos0=PAST_LEN, hist=jnp.pad(state_pool, ((0, 0), (0, 0), (1, 0), (0, 0))),
        past=(jnp.transpose(cache_fox_k, (0, 2, 3, 1)), jnp.transpose(cache_fox_v, (0, 2, 3, 1)),
              jnp.swapaxes(cache_fox_logf, 1, 2)),
        mem_k=cache_mem_k.reshape(DEPTH, SB, MEM_ROWS, MEM_HEAD_DIM),
        mem_v=cache_mem_v.reshape(DEPTH, SB, MEM_ROWS, MEM_HEAD_DIM),
        nb=ROW_BLOCK // SL, tl=SL, pool_states=[])

    y_p, y_s = x_prompt, x_sample
    for i in range(DEPTH):
        cat_p = _mixer(i, y_p, prompt, W)
        cat_s = _mixer(i, y_s, sample, W)
        y_p, y_s = _out_ffn(i, y_p.reshape(B * L, D_MODEL), cat_p.reshape(B * L, MIX_WIDTH),
                            y_s.reshape(SB * SL, D_MODEL), cat_s.reshape(SB * SL, MIX_WIDTH), W)
        y_p = y_p.reshape(B, L, D_MODEL)
        y_s = y_s.reshape(SB, SL, D_MODEL)

    pool_p, fox_k_p, fox_v_p, fox_lf_p = _stream_outputs(prompt)
    pool_s, fox_k_s, fox_v_s, fox_lf_s = _stream_outputs(sample)
    mem_shape = (DEPTH, B, MEM_TOKENS, MEM_HEADS, MEM_HEAD_DIM)
    return (y_p, y_s, pool_p, fox_k_p, fox_v_p, fox_lf_p, mem_k_p.reshape(mem_shape),
            mem_v_p.reshape(mem_shape), pool_s, fox_k_s, fox_v_s, fox_lf_s)
```

```python
import functools

import jax
import jax.numpy as jnp
from jax import lax
from jax.experimental import pallas as pl
from jax.experimental.pallas import tpu as pltpu

F32 = jnp.float32
BF16 = jnp.bfloat16

D_MODEL = 1024
DEPTH = 4
N_A = DEPTH // 2
PAST_LEN = 1024
POOL_WINDOWS = (2, 4, 8, 16)
POOL_GROUPS = len(POOL_WINDOWS)
POOL_WIDTH = D_MODEL // 2
POOL_GROUP_DIM = POOL_WIDTH // POOL_GROUPS
POOL_HIST = max(POOL_WINDOWS) - 1
HIST_ROWS = POOL_HIST + 1
FOX_HEAD_DIM = 64
FOX_WIDTH = D_MODEL // 2
FOX_HEADS = FOX_WIDTH // FOX_HEAD_DIM
MEM_TOKENS = 256
MEM_HEADS = 4
MEM_WIDTH = D_MODEL // 2
MEM_HEAD_DIM = MEM_WIDTH // MEM_HEADS
MIX_WIDTH = POOL_WIDTH + MEM_WIDTH
D_FF = ((8 * D_MODEL // 3 + 255) // 256) * 256
EPS = 1e-6
FOX_SCALE = FOX_HEAD_DIM ** -0.5
MEM_SCALE = MEM_HEAD_DIM ** -0.5
LOG2E = 1.4426950408889634
AUG_ROWS = 16

LANES = 128
ROW_BLOCK = 512
FOX_BLOCK = 512
POOL_BLOCK = 1024
CAST_STEPS = 16
KV_ROWS = 2 * FOX_WIDTH + 16
MXU_DIM = 256
FFN_SPLITS = ((0, 6 * MXU_DIM), (6 * MXU_DIM, D_FF))
VMEM_LIMIT = 56 * 1024 * 1024


def _dot(a, b):
    return jnp.dot(a, b, preferred_element_type=F32)


def _dot_nt(a, b):
    return lax.dot_general(a, b, (((1,), (1,)), ((), ())), preferred_element_type=F32)


def _rms(x, g):
    ms = jnp.mean(x * x, axis=-1, keepdims=True)
    return (x * lax.rsqrt(ms + EPS)) * g


def _rms_head64(x, g_pair):
    lo = lax.broadcasted_iota(jnp.int32, (1, LANES), 1) < FOX_HEAD_DIM
    outs = []
    for c in range(x.shape[-1] // LANES):
        xc = x[:, c * LANES:(c + 1) * LANES]
        sq = xc * xc
        s_lo = jnp.sum(jnp.where(lo, sq, 0.0), axis=-1, keepdims=True)
        s_hi = jnp.sum(jnp.where(lo, 0.0, sq), axis=-1, keepdims=True)
        ms = jnp.where(lo, s_lo, s_hi) * (1.0 / FOX_HEAD_DIM)
        outs.append((xc * lax.rsqrt(ms + EPS)) * g_pair)
    return outs


def _const_spec(shape):
    return pl.BlockSpec(shape, lambda *_: (0,) * len(shape), pipeline_mode=pl.Buffered(1))


def _layer_spec(layer, shape):
    return pl.BlockSpec((1,) + shape, lambda *_: (layer,) + (0,) * len(shape),
                        pipeline_mode=pl.Buffered(1))


MEM_ROWS = MEM_TOKENS * MEM_HEADS


def _mem_spec(layer, nb):
    return pl.BlockSpec((1, nb, MEM_ROWS, MEM_HEAD_DIM), lambda b, *_: (layer, b, 0, 0))


def _head_rows(h):
    return pl.ds(h, MEM_TOKENS, stride=MEM_HEADS)


def _params(n_grid, flags=None):
    return pltpu.CompilerParams(
        dimension_semantics=("arbitrary",) * n_grid, vmem_limit_bytes=VMEM_LIMIT, flags=flags)


def _mem_kv_kernel(mem_ref, g_ref, w_ref, kn_ref, k_ref, v_ref):
    nb = mem_ref.shape[0]
    x = mem_ref[...].reshape(nb * MEM_TOKENS, D_MODEL)
    kv = _dot(_rms(x, g_ref[0]).astype(BF16), w_ref[0])
    for h in range(MEM_HEADS):
        ks = slice(h * MEM_HEAD_DIM, (h + 1) * MEM_HEAD_DIM)
        vs = slice(MEM_WIDTH + h * MEM_HEAD_DIM, MEM_WIDTH + (h + 1) * MEM_HEAD_DIM)
        k_ref[0, :, _head_rows(h), :] = _rms(kv[:, ks], kn_ref[0]).reshape(nb, MEM_TOKENS, MEM_HEAD_DIM)
        v_ref[0, :, _head_rows(h), :] = kv[:, vs].reshape(nb, MEM_TOKENS, MEM_HEAD_DIM)


def _mem_kv(mem, g_mem, w_mem_kv, k_norm_mem):
    B = mem.shape[0]
    nb = 4
    out = jax.ShapeDtypeStruct((DEPTH, B, MEM_ROWS, MEM_HEAD_DIM), F32)
    out_spec = pl.BlockSpec((1, nb, MEM_ROWS, MEM_HEAD_DIM), lambda i, b: (i, b, 0, 0))
    return pl.pallas_call(
        _mem_kv_kernel,
        grid=(DEPTH, B // nb),
        in_specs=[
            pl.BlockSpec((nb, MEM_TOKENS, D_MODEL), lambda i, b: (b, 0, 0)),
            pl.BlockSpec((1, 1, D_MODEL), lambda i, b: (i, 0, 0)),
            pl.BlockSpec((1, D_MODEL, 2 * MEM_WIDTH), lambda i, b: (i, 0, 0)),
            pl.BlockSpec((1, 1, MEM_HEAD_DIM), lambda i, b: (i, 0, 0)),
        ],
        out_specs=[out_spec, out_spec],
        out_shape=[out, out],
        compiler_params=_params(2),
        name="mem_kv",
    )(mem, g_mem, w_mem_kv, k_norm_mem)


def _in_proj(x_ref, g_ref, w_ref):
    nb, tl, _ = x_ref.shape
    x = x_ref[...].reshape(nb * tl, D_MODEL)
    return _dot(_rms(x, g_ref[0]).astype(BF16), w_ref[0])


def _mem_attend(zq, qn, mk_ref, mv_ref, i, cat_ref, col0):
    for h in range(MEM_HEADS):
        sl = slice(h * MEM_HEAD_DIM, (h + 1) * MEM_HEAD_DIM)
        q = _rms(zq[:, sl], qn).astype(BF16)
        s = _dot_nt(q, mk_ref[0, i, _head_rows(h), :].astype(BF16)) * (MEM_SCALE * LOG2E)
        p = jnp.exp2(s - jnp.max(s, axis=-1, keepdims=True)).astype(BF16)
        v = mv_ref[0, i, _head_rows(h), :].astype(BF16)
        o = _dot(p, jnp.concatenate([v, jnp.ones_like(v)], axis=-1))
        o = o[:, :MEM_HEAD_DIM] / o[:, MEM_HEAD_DIM:]
        cat_ref[i, :, col0 + h * MEM_HEAD_DIM:col0 + (h + 1) * MEM_HEAD_DIM] = o.astype(cat_ref.dtype)


def _mem_attend_paired(zq, qn, mk_ref, mv_ref, i, cat_ref, col0):
    z = jnp.zeros((MEM_TOKENS, MEM_HEAD_DIM), BF16)
    one = jnp.ones((MEM_TOKENS, MEM_HEAD_DIM), BF16)
    for c in range(MEM_HEADS // 2):
        hs = (2 * c, 2 * c + 1)
        q = jnp.concatenate(
            [_rms(zq[:, h * MEM_HEAD_DIM:(h + 1) * MEM_HEAD_DIM], qn) for h in hs], axis=1)
        k0, k1 = (mk_ref[0, i, _head_rows(h), :].astype(BF16) for h in hs)
        v0, v1 = (mv_ref[0, i, _head_rows(h), :].astype(BF16) for h in hs)
        k_pair = jnp.concatenate([jnp.concatenate([k0, z], axis=1),
                                  jnp.concatenate([z, k1], axis=1)], axis=0)
        v_pair = jnp.concatenate([jnp.concatenate([v0, one, z, z], axis=1),
                                  jnp.concatenate([z, z, v1, one], axis=1)], axis=0)
        s = _dot_nt(q.astype(BF16), k_pair) * (MEM_SCALE * LOG2E)
        p = jnp.concatenate(
            [jnp.exp2(sh - jnp.max(sh, axis=-1, keepdims=True))
             for sh in (s[:, :MEM_TOKENS], s[:, MEM_TOKENS:])], axis=1).astype(BF16)
        o = _dot(p, v_pair)
        for n, h in enumerate(hs):
            oh = o[:, 2 * n * MEM_HEAD_DIM:(2 * n + 1) * MEM_HEAD_DIM]
            lh = o[:, (2 * n + 1) * MEM_HEAD_DIM:(2 * n + 2) * MEM_HEAD_DIM]
            cat_ref[i, :, col0 + h * MEM_HEAD_DIM:col0 + (h + 1) * MEM_HEAD_DIM] = (
                (oh / lh).astype(cat_ref.dtype))


def _mixer_pool_kernel(x_ref, g_ref, w_in_ref, qn_ref, mk_ref, mv_ref, hist_ref, wp_ref, ps_ref,
                       cat_ref, state_ref, ubuf, *, pos0):
    nb, tl, _ = x_ref.shape
    j = pl.program_id(1)
    z = _in_proj(x_ref, g_ref, w_in_ref)

    @pl.when(j == 0)
    def _():
        ubuf[:, 0:HIST_ROWS, :] = hist_ref[0]

    pos = pos0 + j * tl + lax.broadcasted_iota(jnp.int32, (tl, 1), 0)
    for i in range(nb):
        zi = z[i * tl:(i + 1) * tl]
        u = zi[:, :POOL_WIDTH]
        ubuf[i, HIST_ROWS:HIST_ROWS + tl, :] = u
        for g, w in enumerate(POOL_WINDOWS):
            sl = slice(g * POOL_GROUP_DIM, (g + 1) * POOL_GROUP_DIM)
            ug = u[:, sl]
            acc = ug
            for k in range(1, w):
                acc = acc + ubuf[i, HIST_ROWS - k:HIST_ROWS - k + tl, sl]
            cnt = jnp.minimum(pos + 1, w).astype(F32)
            d = acc / cnt - ug
            y = _dot(d.astype(BF16), wp_ref[0, g]) * ps_ref[0, :, sl]
            cat_ref[i, :, sl] = y.astype(cat_ref.dtype)
        _mem_attend(zi[:, POOL_WIDTH:], qn_ref[0], mk_ref, mv_ref, i, cat_ref, POOL_WIDTH)
        tail = ubuf[i, tl:tl + HIST_ROWS, :]
        state_ref[0, i] = tail
        ubuf[i, 0:HIST_ROWS, :] = tail


def _mixer_pool(layer, x, hist, mem_k, mem_v, W, *, pos0, nb, tl):
    B, L, _ = x.shape
    assert tl >= HIST_ROWS and L % tl == 0 and B % nb == 0
    hist_spec = pl.BlockSpec((1, nb, HIST_ROWS, POOL_WIDTH), lambda b, j: (layer, b, 0, 0))
    return pl.pallas_call(
        functools.partial(_mixer_pool_kernel, pos0=pos0),
        grid=(B // nb, L // tl),
        in_specs=[
            pl.BlockSpec((nb, tl, D_MODEL), lambda b, j: (b, j, 0)),
            _layer_spec(layer, (1, D_MODEL)),
            _layer_spec(0, (D_MODEL, MIX_WIDTH)),
            _layer_spec(layer, (1, MEM_HEAD_DIM)),
            _mem_spec(layer, nb),
            _mem_spec(layer, nb),
            hist_spec,
            _layer_spec(layer, (POOL_GROUPS, POOL_GROUP_DIM, POOL_GROUP_DIM)),
            _layer_spec(layer, (1, POOL_WIDTH)),
        ],
        out_specs=[
            pl.BlockSpec((nb, tl, MIX_WIDTH), lambda b, j: (b, j, 0)),
            pl.BlockSpec((1, nb, HIST_ROWS, POOL_WIDTH), lambda b, j: (0, b, 0, 0)),
        ],
        out_shape=[
            jax.ShapeDtypeStruct((B, L, MIX_WIDTH), BF16),
            jax.ShapeDtypeStruct((1, B, HIST_ROWS, POOL_WIDTH), F32),
        ],
        scratch_shapes=[pltpu.VMEM((nb, HIST_ROWS + tl, POOL_WIDTH), F32)],
        compiler_params=_params(2),
        name="mixer_pool",
    )(x, W["g_mix"], W["w_in"][layer], W["q_norm_mem"], mem_k, mem_v, hist, W["w_pool"], W["pool_scale"])


def _kv_proj_kernel(x_ref, g_ref, w_ref, wv_ref, kn_ref, bf_ref, k_ref, v_ref, lf_ref, kb_ref, vb_ref):
    nb, tl, _ = x_ref.shape
    rows = nb * tl
    x = x_ref[...].reshape(rows, D_MODEL)
    xn = _rms(x, g_ref[...]).astype(BF16)
    zt = _dot_nt(w_ref[...], xn)
    zv = _dot(xn, wv_ref[...])
    lo = lax.broadcasted_iota(jnp.int32, (1, LANES), 1) < FOX_HEAD_DIM
    for h in range(FOX_HEADS):
        pair = zv[:, (h // 2) * LANES:(h // 2 + 1) * LANES]
        base = pair if h % 2 == 0 else pltpu.roll(pair, FOX_HEAD_DIM, 1)
        vb_ref[:, h] = jnp.where(lo, base, 1.0).astype(BF16).reshape(nb, tl, LANES)
    k3 = zt[:FOX_WIDTH].reshape(FOX_HEADS, FOX_HEAD_DIM, rows)
    ms = jnp.mean(k3 * k3, axis=1, keepdims=True)
    k3 = (k3 * lax.rsqrt(ms + EPS)) * kn_ref[...]
    v3 = zt[FOX_WIDTH:2 * FOX_WIDTH].reshape(FOX_HEADS, FOX_HEAD_DIM, rows)
    t = -(zt[2 * FOX_WIDTH:2 * FOX_WIDTH + FOX_HEADS] + bf_ref[...])
    lf = -(jnp.maximum(t, 0.0) + jnp.log1p(jnp.exp(-jnp.abs(t))))
    for i in range(nb):
        cols = slice(i * tl, (i + 1) * tl)
        k_ref[i] = k3[:, :, cols]
        v_ref[i] = v3[:, :, cols]
        lf_ref[i] = lf[:, cols]
        kb_ref[i] = k3[:, :, cols].astype(BF16)


def _kv_proj(x, W, *, nb, tl):
    B, L, _ = x.shape
    hd = pl.BlockSpec((nb, FOX_HEADS, FOX_HEAD_DIM, tl), lambda b, j: (b, 0, 0, j))
    heads = jax.ShapeDtypeStruct((B, FOX_HEADS, FOX_HEAD_DIM, L), F32)
    return pl.pallas_call(
        _kv_proj_kernel,
        grid=(B // nb, L // tl),
        in_specs=[
            pl.BlockSpec((nb, tl, D_MODEL), lambda b, j: (b, j, 0)),
            _const_spec((1, D_MODEL)),
            _const_spec((KV_ROWS, D_MODEL)),
            _const_spec((D_MODEL, FOX_WIDTH)),
            _const_spec((1, FOX_HEAD_DIM, 1)),
            _const_spec((FOX_HEADS, 1)),
        ],
        out_specs=[hd, hd, pl.BlockSpec((nb, FOX_HEADS, tl), lambda b, j: (b, 0, j)), hd,
                   pl.BlockSpec((nb, FOX_HEADS, tl, LANES), lambda b, j: (b, 0, j, 0))],
        out_shape=[heads, heads, jax.ShapeDtypeStruct((B, FOX_HEADS, L), F32),
                   jax.ShapeDtypeStruct((B, FOX_HEADS, FOX_HEAD_DIM, L), BF16),
                   jax.ShapeDtypeStruct((B, FOX_HEADS, L, LANES), BF16)],
        compiler_params=_params(2),
        name="kv_proj",
    )(x, W["g_kv"], W["w_kv_t"], W["w_v"], W["kn_col"], W["b_f"])


def _cumsum_kernel(lf_ref, f_ref):
    rows, n = lf_ref.shape
    r = lax.broadcasted_iota(jnp.int32, (LANES, LANES), 0)
    c = lax.broadcasted_iota(jnp.int32, (LANES, LANES), 1)
    tri = jnp.where(r <= c, 1.0, 0.0).astype(BF16)
    carry = jnp.zeros((rows, 1), F32)
    for ch in range(n // LANES):
        x = lf_ref[:, ch * LANES:(ch + 1) * LANES]
        hi = x.astype(BF16)
        r1 = x - hi.astype(F32)
        mid = r1.astype(BF16)
        low = (r1 - mid.astype(F32)).astype(BF16)
        y = (_dot(hi, tri) + _dot(mid, tri)) + _dot(low, tri) + carry
        f_ref[:, ch * LANES:(ch + 1) * LANES] = y
        carry = y[:, LANES - 1:LANES]


def _cumsum_lanes(lf_t):
    return pl.pallas_call(
        _cumsum_kernel,
        out_shape=jax.ShapeDtypeStruct(lf_t.shape, F32),
        name="logf_cumsum",
    )(lf_t)


def _tri_unrank(t, n):
    row = sum((t >= k * (k + 1) // 2).astype(jnp.int32) for k in range(1, n))
    return row, t - row * (row + 1) // 2


def _split3(x):
    hi = x.astype(BF16).astype(F32)
    r1 = x - hi
    mid = r1.astype(BF16).astype(F32)
    low = (r1 - mid).astype(BF16).astype(F32)
    return hi, mid, low


def _mixer_fox_prompt_kernel(x_ref, g_ref, w_in_ref, qn_ref, mk_ref, mv_ref, qnf_ref,
                             kt_ref, v_ref, ft_ref, fq_ref, cat_ref, q_sc, m_sc, acc_sc, mask_sc,
                             *, nkb):
    _, tl, _ = x_ref.shape
    tk = kt_ref.shape[-1]
    j, kb = _tri_unrank(pl.program_id(1), nkb)
    lane = lax.broadcasted_iota(jnp.int32, (1, LANES), 1)
    row = lax.broadcasted_iota(jnp.int32, (AUG_ROWS, 1), 0)

    @pl.when(kb == 0)
    def _():
        z = _in_proj(x_ref, g_ref, w_in_ref)
        _mem_attend_paired(z[:, FOX_WIDTH:], qn_ref[0], mk_ref, mv_ref, 0, cat_ref, FOX_WIDTH)
        qs = _rms_head64(z[:, :FOX_WIDTH], qnf_ref[0])
        for h in range(FOX_HEADS):
            base = qs[h // 2] if h % 2 == 0 else pltpu.roll(qs[h // 2], FOX_HEAD_DIM, 1)
            hi, mid, low = _split3(fq_ref[0, :, h:h + 1] * LOG2E)
            tail = jnp.where(lane < FOX_HEAD_DIM + 3, 1.0,
                             jnp.where(lane == FOX_HEAD_DIM + 3, hi,
                                       jnp.where(lane == FOX_HEAD_DIM + 4, mid,
                                                 jnp.where(lane == FOX_HEAD_DIM + 5, low, 0.0))))
            q_sc[h] = jnp.where(lane < FOX_HEAD_DIM, base * (FOX_SCALE * LOG2E), tail).astype(BF16)
        m_sc[...] = jnp.full(m_sc.shape, -jnp.inf, F32)
        acc_sc[...] = jnp.zeros(acc_sc.shape, F32)
        causal = (lax.broadcasted_iota(jnp.int32, (tl, 1), 0)
                  >= lax.broadcasted_iota(jnp.int32, (1, tk), 1))
        mask_sc[...] = jnp.where(causal, 0.0, -jnp.inf)

    def k_aug(h):
        hi, mid, low = _split3(ft_ref[0, 0, h:h + 1, :] * (-LOG2E))
        aug = jnp.where(row == 0, hi, jnp.where(row == 1, mid, jnp.where(row == 2, low,
                        jnp.where(row < 6, 1.0, 0.0)))).astype(BF16)
        pad = jnp.zeros((FOX_HEAD_DIM - AUG_ROWS, tk), BF16)
        return jnp.concatenate([kt_ref[0, h], aug, pad], axis=0)

    def attend(masked):
        zk = jnp.zeros((LANES, tk), BF16)
        zv = jnp.zeros((tk, LANES), BF16)
        for c in range(FOX_HEADS // 2):
            h0, h1 = 2 * c, 2 * c + 1
            k_pair = jnp.concatenate([jnp.concatenate([k_aug(h0), zk], axis=1),
                                      jnp.concatenate([zk, k_aug(h1)], axis=1)], axis=0)
            v_pair = jnp.concatenate([jnp.concatenate([v_ref[0, h0], zv], axis=1),
                                      jnp.concatenate([zv, v_ref[0, h1]], axis=1)], axis=0)
            q_pair = jnp.concatenate([q_sc[h0], q_sc[h1]], axis=1)
            s = _dot(q_pair, k_pair)
            ps, alphas = [], []
            for h, sh in ((h0, s[:, :tk]), (h1, s[:, tk:])):
                if masked:
                    sh = sh + mask_sc[...]
                m_old = m_sc[h]
                m_new = jnp.maximum(m_old, jnp.max(sh, axis=-1, keepdims=True))
                ps.append(jnp.exp2(sh - m_new[:, 0:1]).astype(BF16))
                alphas.append(jnp.exp2(m_old - m_new))
                m_sc[h] = m_new
            o = _dot(jnp.concatenate(ps, axis=1), v_pair)
            acc_sc[h0] = alphas[0] * acc_sc[h0] + o[:, :LANES]
            acc_sc[h1] = alphas[1] * acc_sc[h1] + o[:, LANES:]

    @pl.when(kb < j)
    def _():
        attend(False)

    @pl.when(kb == j)
    def _():
        attend(True)
        for c in range(FOX_HEADS // 2):
            a0 = acc_sc[2 * c]
            a1 = acc_sc[2 * c + 1]
            o0 = a0 / pltpu.roll(a0, FOX_HEAD_DIM, 1)
            o1 = pltpu.roll(a1, FOX_HEAD_DIM, 1) / a1
            cat_ref[0, :, c * LANES:(c + 1) * LANES] = (
                jnp.where(lane < FOX_HEAD_DIM, o0, o1).astype(cat_ref.dtype))


def _mixer_fox_prompt(layer, x, kt_b, vt_b, f_t, f_q, mem_k, mem_v, W, *, tl):
    B, L, _ = x.shape
    nkb = L // tl
    ft4 = jnp.swapaxes(f_t.reshape(B, FOX_HEADS, nkb, tl), 1, 2)
    jj = lambda t: _tri_unrank(t, nkb)[0]
    kk = lambda t: _tri_unrank(t, nkb)[1]
    kt_spec = pl.BlockSpec((1, FOX_HEADS, FOX_HEAD_DIM, tl), lambda b, t: (b, 0, 0, kk(t)))
    v_spec = pl.BlockSpec((1, FOX_HEADS, tl, LANES), lambda b, t: (b, 0, kk(t), 0))
    return pl.pallas_call(
        functools.partial(_mixer_fox_prompt_kernel, nkb=nkb),
        grid=(B, nkb * (nkb + 1) // 2),
        in_specs=[
            pl.BlockSpec((1, tl, D_MODEL), lambda b, t: (b, jj(t), 0)),
            _layer_spec(layer, (1, D_MODEL)),
            _layer_spec(0, (D_MODEL, MIX_WIDTH)),
            _layer_spec(layer, (1, MEM_HEAD_DIM)),
            _mem_spec(layer, 1),
            _mem_spec(layer, 1),
            _layer_spec(layer - N_A, (1, LANES)),
            kt_spec,
            v_spec,
            pl.BlockSpec((1, 1, FOX_HEADS, tl), lambda b, t: (b, kk(t), 0, 0)),
            pl.BlockSpec((1, tl, FOX_HEADS), lambda b, t: (b, jj(t), 0)),
        ],
        out_specs=pl.BlockSpec((1, tl, MIX_WIDTH), lambda b, t: (b, jj(t), 0)),
        out_shape=jax.ShapeDtypeStruct((B, L, MIX_WIDTH), BF16),
        scratch_shapes=[
            pltpu.VMEM((FOX_HEADS, tl, LANES), BF16),
            pltpu.VMEM((FOX_HEADS, tl, LANES), F32),
            pltpu.VMEM((FOX_HEADS, tl, LANES), F32),
            pltpu.VMEM((tl, tl), F32),
        ],
        compiler_params=_params(2),
        name="mixer_fox_prompt",
    )(x, W["g_mix"], W["w_in"][layer], W["q_norm_mem"], mem_k, mem_v, W["qnf_pair"], kt_b, vt_b, ft4, f_q)


def _block_diag(blocks):
    a, b = blocks
    z = jnp.zeros_like(a)
    return jnp.concatenate([jnp.concatenate([a, z], axis=1),
                            jnp.concatenate([z, b], axis=1)], axis=0)


def _fox_sample_pre_kernel(x_ref, g_ref, w_in_ref, qn_ref, mk_ref, mv_ref, qnf_ref, q_ref, cm_ref):
    nb, tl, _ = x_ref.shape
    z = _in_proj(x_ref, g_ref, w_in_ref)
    for i in range(nb):
        zi = z[i * tl:(i + 1) * tl]
        qs = _rms_head64(zi[:, :FOX_WIDTH], qnf_ref[0])
        for c in range(FOX_HEADS // 2):
            q_ref[i, :, c * LANES:(c + 1) * LANES] = (qs[c] * FOX_SCALE).astype(BF16)
        _mem_attend(zi[:, FOX_WIDTH:], qn_ref[0], mk_ref, mv_ref, i, cm_ref, 0)


def _fox_sample_attn_kernel(q_ref, ktp_ref, vtp_ref, ktn_ref, vtn_ref, ft_ref, fq_ref, cat_ref):
    nb, tl, _ = q_ref.shape
    past = ktp_ref.shape[-1]
    causal = (lax.broadcasted_iota(jnp.int32, (tl, 1), 0)
              >= lax.broadcasted_iota(jnp.int32, (1, tl), 1))
    lo = lax.broadcasted_iota(jnp.int32, (1, LANES), 1) < FOX_HEAD_DIM
    for i in range(nb):
        for c in range(FOX_HEADS // 2):
            hs = (2 * c, 2 * c + 1)
            q = q_ref[i, :, c * LANES:(c + 1) * LANES]
            s_p = _dot(q, _block_diag([ktp_ref[i, h].astype(BF16) for h in hs]))
            s_n = _dot(q, _block_diag([ktn_ref[i, h].astype(BF16) for h in hs]))
            pp, pn, inv_l = [], [], []
            for n, h in enumerate(hs):
                fq = fq_ref[i, :, h:h + 1]
                sp = (s_p[:, n * past:(n + 1) * past] + fq) - ft_ref[i, h:h + 1, 0:past]
                sn = (s_n[:, n * tl:(n + 1) * tl] + fq) - ft_ref[i, h:h + 1, past:past + tl]
                sn = jnp.where(causal, sn, -jnp.inf)
                m = jnp.maximum(jnp.max(sp, axis=-1, keepdims=True),
                                jnp.max(sn, axis=-1, keepdims=True))
                ep = jnp.exp(sp - m)
                en = jnp.exp(sn - m)
                inv_l.append(1.0 / (jnp.sum(ep, axis=-1, keepdims=True)
                                    + jnp.sum(en, axis=-1, keepdims=True)))
                pp.append(ep.astype(BF16))
                pn.append(en.astype(BF16))
            o = (_dot_nt(jnp.concatenate(pp, axis=1),
                         _block_diag([vtp_ref[i, h].astype(BF16) for h in hs]))
                 + _dot_nt(jnp.concatenate(pn, axis=1),
                           _block_diag([vtn_ref[i, h].astype(BF16) for h in hs])))
            scale = jnp.where(lo, inv_l[0], inv_l[1])
            cat_ref[i, :, c * LANES:(c + 1) * LANES] = (o * scale).astype(cat_ref.dtype)


def _mixer_fox_sample(layer, x, kt_past, vt_past, kt_new, vt_new, f_t, f_q, mem_k, mem_v, W, *,
                      nb_dense, nb_attn):
    B, L, _ = x.shape
    past = kt_past.shape[-1]
    lk_pad = f_t.shape[-1]
    blk = lambda nb, *tail: pl.BlockSpec((nb,) + tail, lambda b: (b,) + (0,) * len(tail))
    half = jax.ShapeDtypeStruct((B, L, FOX_WIDTH), BF16)
    q, cat_mem = pl.pallas_call(
        _fox_sample_pre_kernel,
        grid=(B // nb_dense,),
        in_specs=[
            blk(nb_dense, L, D_MODEL),
            _layer_spec(layer, (1, D_MODEL)),
            _layer_spec(0, (D_MODEL, MIX_WIDTH)),
            _layer_spec(layer, (1, MEM_HEAD_DIM)),
            _mem_spec(layer, nb_dense),
            _mem_spec(layer, nb_dense),
            _layer_spec(layer - N_A, (1, LANES)),
        ],
        out_specs=[blk(nb_dense, L, FOX_WIDTH), blk(nb_dense, L, MEM_WIDTH)],
        out_shape=[half, half],
        compiler_params=_params(1),
        name="fox_sample_pre",
    )(x, W["g_mix"], W["w_in"][layer], W["q_norm_mem"], mem_k, mem_v, W["qnf_pair"])
    cat_fox = pl.pallas_call(
        _fox_sample_attn_kernel,
        grid=(B // nb_attn,),
        in_specs=[
            blk(nb_attn, L, FOX_WIDTH),
            blk(nb_attn, FOX_HEADS, FOX_HEAD_DIM, past),
            blk(nb_attn, FOX_HEADS, FOX_HEAD_DIM, past),
            blk(nb_attn, FOX_HEADS, FOX_HEAD_DIM, L),
            blk(nb_attn, FOX_HEADS, FOX_HEAD_DIM, L),
            blk(nb_attn, FOX_HEADS, lk_pad),
            blk(nb_attn, L, FOX_HEADS),
        ],
        out_specs=blk(nb_attn, L, FOX_WIDTH),
        out_shape=half,
        compiler_params=_params(1),
        name="fox_sample_attn",
    )(q, kt_past, vt_past, kt_new, vt_new, f_t, f_q)
    return jnp.concatenate([cat_fox, cat_mem], axis=-1)


def _out_ffn_kernel(xa_ref, cata_ref, xb_ref, catb_ref, w_out_ref, g_ref, w_gu_ref, w_down_ref,
                    *rest, na, n_cast):
    if n_cast:
        src, (ya_ref, yb_ref), dst = rest[:4], rest[4:6], rest[6:]

        @pl.when(pl.program_id(0) < n_cast)
        def _():
            for s_ref, d_ref in zip(src, dst):
                d_ref[...] = s_ref[...].astype(BF16)
    else:
        ya_ref, yb_ref = rest

    def body(x_ref, cat_ref, y_ref):
        x1 = x_ref[...] + _dot(cat_ref[...], w_out_ref[0])
        xn = _rms(x1, g_ref[0]).astype(BF16)
        acc = x1
        for lo, hi in FFN_SPLITS:
            gate = _dot(xn, w_gu_ref[0, :, lo:hi])
            up = _dot(xn, w_gu_ref[0, :, D_FF + lo:D_FF + hi])
            h = (gate * (1.0 / (1.0 + jnp.exp(-gate)))) * up
            acc = acc + _dot(h.astype(BF16), w_down_ref[0, lo:hi, :])
        y_ref[...] = acc

    @pl.when(pl.program_id(0) < na)
    def _():
        body(xa_ref, cata_ref, ya_ref)

    @pl.when(pl.program_id(0) >= na)
    def _():
        body(xb_ref, catb_ref, yb_ref)


def _out_ffn(layer, xa, cata, xb, catb, W, next_f32=()):
    na = xa.shape[0] // ROW_BLOCK
    nb = xb.shape[0] // ROW_BLOCK
    a_map = lambda r: (jnp.minimum(r, na - 1), 0)
    b_map = lambda r: (jnp.maximum(r - na, 0), 0)
    n_cast = CAST_STEPS if next_f32 else 0
    assert n_cast <= na + nb
    chunk = lambda w: (1, w.shape[1] // CAST_STEPS, w.shape[2])
    src_specs = [pl.BlockSpec(chunk(w), lambda r: (layer + 1, jnp.minimum(r, CAST_STEPS - 1), 0))
                 for w in next_f32]
    dst_specs = [pl.BlockSpec(chunk(w), lambda r: (0, jnp.minimum(r, CAST_STEPS - 1), 0))
                 for w in next_f32]
    dst_shapes = [jax.ShapeDtypeStruct((1,) + w.shape[1:], BF16) for w in next_f32]
    outs = pl.pallas_call(
        functools.partial(_out_ffn_kernel, na=na, n_cast=n_cast),
        grid=(na + nb,),
        in_specs=[
            pl.BlockSpec((ROW_BLOCK, D_MODEL), a_map),
            pl.BlockSpec((ROW_BLOCK, MIX_WIDTH), a_map),
            pl.BlockSpec((ROW_BLOCK, D_MODEL), b_map),
            pl.BlockSpec((ROW_BLOCK, MIX_WIDTH), b_map),
            _layer_spec(0, (MIX_WIDTH, D_MODEL)),
            _layer_spec(layer, (1, D_MODEL)),
            _layer_spec(0, (D_MODEL, 2 * D_FF)),
            _layer_spec(0, (D_FF, D_MODEL)),
        ] + src_specs,
        out_specs=[pl.BlockSpec((ROW_BLOCK, D_MODEL), a_map),
                   pl.BlockSpec((ROW_BLOCK, D_MODEL), b_map)] + dst_specs,
        out_shape=[jax.ShapeDtypeStruct(xa.shape, F32), jax.ShapeDtypeStruct(xb.shape, F32)]
        + dst_shapes,
        compiler_params=_params(1),
        name="out_ffn",
    )(xa, cata, xb, catb, W["w_out"][layer], W["g_ffn"], W["w_gu"][layer], W["w_down"][layer],
      *next_f32)
    return outs[0], outs[1], outs[2:]


def _mixer(i, x, st, W):
    B, L, _ = x.shape
    if i < N_A:
        cat, state = _mixer_pool(i, x, st["hist"], st["mem_k"], st["mem_v"], W, pos0=st["pos0"],
                                 nb=st["nb"], tl=min(L, POOL_BLOCK))
        st["pool_states"].append(state[:, :, 1:, :])
        return cat
    past = st["past"]
    if i == N_A:
        kt_new, vt_new, lft_new, kt_b, v_b = _kv_proj(x, W, nb=st["nb"], tl=st["tl"])
        lft_all = lft_new if past is None else jnp.concatenate([past[2], lft_new], axis=2)
        lk = lft_all.shape[2]
        lk_pad = -(-lk // LANES) * LANES
        lf_t = jnp.pad(lft_all.reshape(B * FOX_HEADS, lk), ((0, 0), (0, lk_pad - lk)))
        f_t = _cumsum_lanes(lf_t).reshape(B, FOX_HEADS, lk_pad)
        st.update(kt_new=kt_new, vt_new=vt_new, lft_new=lft_new, kt_b=kt_b, v_b=v_b, f_t=f_t,
                  f_q=jnp.swapaxes(f_t[:, :, lk - L:lk], 1, 2))
    if past is None:
        return _mixer_fox_prompt(i, x, st["kt_b"], st["v_b"], st["f_t"], st["f_q"], st["mem_k"],
                                 st["mem_v"], W, tl=FOX_BLOCK)
    return _mixer_fox_sample(i, x, past[0], past[1], st["kt_new"], st["vt_new"], st["f_t"],
                             st["f_q"], st["mem_k"], st["mem_v"], W, nb_dense=st["nb"], nb_attn=4)


def _stream_outputs(st):
    return (jnp.concatenate(st["pool_states"], axis=0), jnp.transpose(st["kt_new"], (0, 3, 1, 2)),
            jnp.transpose(st["vt_new"], (0, 3, 1, 2)), jnp.swapaxes(st["lft_new"], 1, 2))


def kernel(x_prompt, x_sample, state_pool, cache_fox_k, cache_fox_v, cache_fox_logf, cache_mem_k,
           cache_mem_v, mem_prompt, g_mix, w_in, w_out, q_norm_mem, g_mem, w_mem_kv, k_norm_mem,
           w_pool, pool_scale, q_norm_fox, g_kv, w_kv, k_norm_fox, b_f, g_ffn, w_gu, w_down):
    B, L, _ = x_prompt.shape
    SB, SL, _ = x_sample.shape
    big = (w_in, w_out, w_gu, w_down)
    W = dict(
        g_mix=g_mix.reshape(DEPTH, 1, D_MODEL), w_in=[w_in[0:1].astype(BF16)],
        w_out=[w_out[0:1].astype(BF16)],
        q_norm_mem=q_norm_mem.reshape(DEPTH, 1, MEM_HEAD_DIM), w_pool=w_pool.astype(BF16),
        pool_scale=pool_scale.reshape(N_A, 1, POOL_WIDTH),
        qnf_pair=jnp.tile(q_norm_fox, (1, 2)).reshape(DEPTH - N_A, 1, LANES),
        g_kv=g_kv.reshape(1, D_MODEL),
        w_kv_t=jnp.pad(w_kv.T, ((0, KV_ROWS - w_kv.shape[1]), (0, 0))).astype(BF16),
        w_v=w_kv[:, FOX_WIDTH:2 * FOX_WIDTH].astype(BF16),
        kn_col=k_norm_fox.reshape(1, FOX_HEAD_DIM, 1), b_f=b_f.reshape(FOX_HEADS, 1),
        g_ffn=g_ffn.reshape(DEPTH, 1, D_MODEL), w_gu=[w_gu[0:1].astype(BF16)],
        w_down=[w_down[0:1].astype(BF16)])

    mem_k_p, mem_v_p = _mem_kv(mem_prompt, g_mem.reshape(DEPTH, 1, D_MODEL), w_mem_kv.astype(BF16),
                               k_norm_mem.reshape(DEPTH, 1, MEM_HEAD_DIM))
    prompt = dict(pos0=0, hist=jnp.zeros((N_A, B, HIST_ROWS, POOL_WIDTH), F32), past=None,
                  mem_k=mem_k_p, mem_v=mem_v_p, nb=1, tl=ROW_BLOCK, pool_states=[])
    sample = dict(
        pos0=PAST_LEN, hist=jnp.pad(state_pool, ((0, 0), (0, 0), (1, 0), (0, 0))),
        past=(jnp.transpose(cache_fox_k, (0, 2, 3, 1)), jnp.transpose(cache_fox_v, (0, 2, 3, 1)),
              jnp.swapaxes(cache_fox_logf, 1, 2)),
        mem_k=cache_mem_k.reshape(DEPTH, SB, MEM_ROWS, MEM_HEAD_DIM),
        mem_v=cache_mem_v.reshape(DEPTH, SB, MEM_ROWS, MEM_HEAD_DIM),
        nb=ROW_BLOCK // SL, tl=SL, pool_states=[])

    y_p, y_s = x_prompt, x_sample
    for i in range(DEPTH):
        cat_p = _mixer(i, y_p, prompt, W)
        cat_s = _mixer(i, y_s, sample, W)
        y_p, y_s, nxt = _out_ffn(i, y_p.reshape(B * L, D_MODEL), cat_p.reshape(B * L, MIX_WIDTH),
                                 y_s.reshape(SB * SL, D_MODEL), cat_s.reshape(SB * SL, MIX_WIDTH),
                                 W, next_f32=big if i + 1 < DEPTH else ())
        for name, w_next in zip(("w_in", "w_out", "w_gu", "w_down"), nxt):
            W[name].append(w_next)
        y_p = y_p.reshape(B, L, D_MODEL)
        y_s = y_s.reshape(SB, SL, D_MODEL)

    pool_p, fox_k_p, fox_v_p, fox_lf_p = _stream_outputs(prompt)
    pool_s, fox_k_s, fox_v_s, fox_lf_s = _stream_outputs(sample)
    mem_shape = (DEPTH, B, MEM_TOKENS, MEM_HEADS, MEM_HEAD_DIM)
    return (y_p, y_s, pool_p, fox_k_p, fox_v_p, fox_lf_p, mem_k_p.reshape(mem_shape),
            mem_v_p.reshape(mem_shape), pool_s, fox_k_s, fox_v_s, fox_lf_s)
```

```python
import functools

import jax
import jax.numpy as jnp
from jax import lax
from jax.experimental import pallas as pl
from jax.experimental.pallas import tpu as pltpu

F32 = jnp.float32
BF16 = jnp.bfloat16

D_MODEL = 1024
DEPTH = 4
N_A = DEPTH // 2
PAST_LEN = 1024
POOL_WINDOWS = (2, 4, 8, 16)
POOL_GROUPS = len(POOL_WINDOWS)
POOL_WIDTH = D_MODEL // 2
POOL_GROUP_DIM = POOL_WIDTH // POOL_GROUPS
POOL_HIST = max(POOL_WINDOWS) - 1
HIST_ROWS = POOL_HIST + 1
FOX_HEAD_DIM = 64
FOX_WIDTH = D_MODEL // 2
FOX_HEADS = FOX_WIDTH // FOX_HEAD_DIM
MEM_TOKENS = 256
MEM_HEADS = 4
MEM_WIDTH = D_MODEL // 2
MEM_HEAD_DIM = MEM_WIDTH // MEM_HEADS
MIX_WIDTH = POOL_WIDTH + MEM_WIDTH
D_FF = ((8 * D_MODEL // 3 + 255) // 256) * 256
EPS = 1e-6
FOX_SCALE = FOX_HEAD_DIM ** -0.5
MEM_SCALE = MEM_HEAD_DIM ** -0.5
LOG2E = 1.4426950408889634
AUG_ROWS = 16

LANES = 128
ROW_BLOCK = 512
FOX_BLOCK = 512
POOL_BLOCK = 1024
CAST_STEPS = 16
SAMPLE_HEAD_GROUP = 4
KV_ROWS = 2 * FOX_WIDTH + 16
MXU_DIM = 256
FFN_SPLITS = ((0, 6 * MXU_DIM), (6 * MXU_DIM, D_FF))
VMEM_LIMIT = 56 * 1024 * 1024


def _dot(a, b):
    return jnp.dot(a, b, preferred_element_type=F32)


def _dot_nt(a, b):
    return lax.dot_general(a, b, (((1,), (1,)), ((), ())), preferred_element_type=F32)


def _rms(x, g):
    ms = jnp.mean(x * x, axis=-1, keepdims=True)
    return (x * lax.rsqrt(ms + EPS)) * g


def _rms_head64(x, g_pair):
    lo = lax.broadcasted_iota(jnp.int32, (1, LANES), 1) < FOX_HEAD_DIM
    outs = []
    for c in range(x.shape[-1] // LANES):
        xc = x[:, c * LANES:(c + 1) * LANES]
        sq = xc * xc
        s_lo = jnp.sum(jnp.where(lo, sq, 0.0), axis=-1, keepdims=True)
        s_hi = jnp.sum(jnp.where(lo, 0.0, sq), axis=-1, keepdims=True)
        ms = jnp.where(lo, s_lo, s_hi) * (1.0 / FOX_HEAD_DIM)
        outs.append((xc * lax.rsqrt(ms + EPS)) * g_pair)
    return outs


def _const_spec(shape):
    return pl.BlockSpec(shape, lambda *_: (0,) * len(shape), pipeline_mode=pl.Buffered(1))


def _layer_spec(layer, shape):
    return pl.BlockSpec((1,) + shape, lambda *_: (layer,) + (0,) * len(shape),
                        pipeline_mode=pl.Buffered(1))


MEM_ROWS = MEM_TOKENS * MEM_HEADS


def _mem_spec(layer, nb):
    return pl.BlockSpec((1, nb, MEM_ROWS, MEM_HEAD_DIM), lambda b, *_: (layer, b, 0, 0))


def _head_rows(h):
    return pl.ds(h, MEM_TOKENS, stride=MEM_HEADS)


def _params(n_grid, flags=None):
    return pltpu.CompilerParams(
        dimension_semantics=("arbitrary",) * n_grid, vmem_limit_bytes=VMEM_LIMIT, flags=flags)


def _mem_kv_kernel(mem_ref, g_ref, w_ref, kn_ref, *rest):
    n = (len(rest) - 2) // 2
    src, (k_ref, v_ref), dst = rest[:n], rest[n:n + 2], rest[n + 2:]
    for s_ref, d_ref in zip(src, dst):
        d_ref[...] = s_ref[...].astype(BF16)
    nb = mem_ref.shape[0]
    x = mem_ref[...].reshape(nb * MEM_TOKENS, D_MODEL)
    kv = _dot(_rms(x, g_ref[0]).astype(BF16), w_ref[0])
    for h in range(MEM_HEADS):
        ks = slice(h * MEM_HEAD_DIM, (h + 1) * MEM_HEAD_DIM)
        vs = slice(MEM_WIDTH + h * MEM_HEAD_DIM, MEM_WIDTH + (h + 1) * MEM_HEAD_DIM)
        k_ref[0, :, _head_rows(h), :] = _rms(kv[:, ks], kn_ref[0]).reshape(nb, MEM_TOKENS, MEM_HEAD_DIM)
        v_ref[0, :, _head_rows(h), :] = kv[:, vs].reshape(nb, MEM_TOKENS, MEM_HEAD_DIM)


def _mem_kv(mem, g_mem, w_mem_kv, k_norm_mem, first_f32):
    B = mem.shape[0]
    nb = 4
    steps = DEPTH * (B // nb)
    out = jax.ShapeDtypeStruct((DEPTH, B, MEM_ROWS, MEM_HEAD_DIM), F32)
    out_spec = pl.BlockSpec((1, nb, MEM_ROWS, MEM_HEAD_DIM), lambda i, b: (i, b, 0, 0))
    chunk = lambda w: (1, w.shape[1] // steps, w.shape[2])
    chunk_spec = lambda w: pl.BlockSpec(chunk(w), lambda i, b: (0, i * (B // nb) + b, 0))
    outs = pl.pallas_call(
        _mem_kv_kernel,
        grid=(DEPTH, B // nb),
        in_specs=[
            pl.BlockSpec((nb, MEM_TOKENS, D_MODEL), lambda i, b: (b, 0, 0)),
            pl.BlockSpec((1, 1, D_MODEL), lambda i, b: (i, 0, 0)),
            pl.BlockSpec((1, D_MODEL, 2 * MEM_WIDTH), lambda i, b: (i, 0, 0)),
            pl.BlockSpec((1, 1, MEM_HEAD_DIM), lambda i, b: (i, 0, 0)),
        ] + [chunk_spec(w) for w in first_f32],
        out_specs=[out_spec, out_spec] + [chunk_spec(w) for w in first_f32],
        out_shape=[out, out] + [jax.ShapeDtypeStruct((1,) + w.shape[1:], BF16) for w in first_f32],
        compiler_params=_params(2),
        name="mem_kv",
    )(mem, g_mem, w_mem_kv, k_norm_mem, *first_f32)
    return outs[0], outs[1], outs[2:]


def _in_proj(x_ref, g_ref, w_ref):
    nb, tl, _ = x_ref.shape
    x = x_ref[...].reshape(nb * tl, D_MODEL)
    return _dot(_rms(x, g_ref[0]).astype(BF16), w_ref[0])


def _mem_attend(zq, qn, mk_ref, mv_ref, i, cat_ref, col0):
    for h in range(MEM_HEADS):
        sl = slice(h * MEM_HEAD_DIM, (h + 1) * MEM_HEAD_DIM)
        q = _rms(zq[:, sl], qn).astype(BF16)
        s = _dot_nt(q, mk_ref[0, i, _head_rows(h), :].astype(BF16)) * (MEM_SCALE * LOG2E)
        p = jnp.exp2(s - jnp.max(s, axis=-1, keepdims=True)).astype(BF16)
        v = mv_ref[0, i, _head_rows(h), :].astype(BF16)
        o = _dot(p, jnp.concatenate([v, jnp.ones_like(v)], axis=-1))
        o = o[:, :MEM_HEAD_DIM] / o[:, MEM_HEAD_DIM:]
        cat_ref[i, :, col0 + h * MEM_HEAD_DIM:col0 + (h + 1) * MEM_HEAD_DIM] = o.astype(cat_ref.dtype)


def _mem_attend_paired(zq, qn, mk_ref, mv_ref, i, cat_ref, col0):
    z = jnp.zeros((MEM_TOKENS, MEM_HEAD_DIM), BF16)
    one = jnp.ones((MEM_TOKENS, MEM_HEAD_DIM), BF16)
    for c in range(MEM_HEADS // 2):
        hs = (2 * c, 2 * c + 1)
        q = jnp.concatenate(
            [_rms(zq[:, h * MEM_HEAD_DIM:(h + 1) * MEM_HEAD_DIM], qn) for h in hs], axis=1)
        k0, k1 = (mk_ref[0, i, _head_rows(h), :].astype(BF16) for h in hs)
        v0, v1 = (mv_ref[0, i, _head_rows(h), :].astype(BF16) for h in hs)
        k_pair = jnp.concatenate([jnp.concatenate([k0, z], axis=1),
                                  jnp.concatenate([z, k1], axis=1)], axis=0)
        v_pair = jnp.concatenate([jnp.concatenate([v0, one, z, z], axis=1),
                                  jnp.concatenate([z, z, v1, one], axis=1)], axis=0)
        s = _dot_nt(q.astype(BF16), k_pair) * (MEM_SCALE * LOG2E)
        p = jnp.concatenate(
            [jnp.exp2(sh - jnp.max(sh, axis=-1, keepdims=True))
             for sh in (s[:, :MEM_TOKENS], s[:, MEM_TOKENS:])], axis=1).astype(BF16)
        o = _dot(p, v_pair)
        for n, h in enumerate(hs):
            oh = o[:, 2 * n * MEM_HEAD_DIM:(2 * n + 1) * MEM_HEAD_DIM]
            lh = o[:, (2 * n + 1) * MEM_HEAD_DIM:(2 * n + 2) * MEM_HEAD_DIM]
            cat_ref[i, :, col0 + h * MEM_HEAD_DIM:col0 + (h + 1) * MEM_HEAD_DIM] = (
                (oh / lh).astype(cat_ref.dtype))


def _mixer_pool_kernel(x_ref, g_ref, w_in_ref, qn_ref, mk_ref, mv_ref, hist_ref, wp_ref, ps_ref,
                       cat_ref, state_ref, ubuf, *, pos0):
    nb, tl, _ = x_ref.shape
    j = pl.program_id(1)
    z = _in_proj(x_ref, g_ref, w_in_ref)

    @pl.when(j == 0)
    def _():
        ubuf[:, 0:HIST_ROWS, :] = hist_ref[0]

    pos = pos0 + j * tl + lax.broadcasted_iota(jnp.int32, (tl, 1), 0)
    for i in range(nb):
        zi = z[i * tl:(i + 1) * tl]
        u = zi[:, :POOL_WIDTH]
        ubuf[i, HIST_ROWS:HIST_ROWS + tl, :] = u
        for g, w in enumerate(POOL_WINDOWS):
            sl = slice(g * POOL_GROUP_DIM, (g + 1) * POOL_GROUP_DIM)
            ug = u[:, sl]
            acc = ug
            for k in range(1, w):
                acc = acc + ubuf[i, HIST_ROWS - k:HIST_ROWS - k + tl, sl]
            cnt = jnp.minimum(pos + 1, w).astype(F32)
            d = acc / cnt - ug
            y = _dot(d.astype(BF16), wp_ref[0, g]) * ps_ref[0, :, sl]
            cat_ref[i, :, sl] = y.astype(cat_ref.dtype)
        _mem_attend(zi[:, POOL_WIDTH:], qn_ref[0], mk_ref, mv_ref, i, cat_ref, POOL_WIDTH)
        tail = ubuf[i, tl:tl + HIST_ROWS, :]
        state_ref[0, i] = tail
        ubuf[i, 0:HIST_ROWS, :] = tail


def _mixer_pool(layer, x, hist, mem_k, mem_v, W, *, pos0, nb, tl):
    B, L, _ = x.shape
    assert tl >= HIST_ROWS and L % tl == 0 and B % nb == 0
    hist_spec = pl.BlockSpec((1, nb, HIST_ROWS, POOL_WIDTH), lambda b, j: (layer, b, 0, 0))
    return pl.pallas_call(
        functools.partial(_mixer_pool_kernel, pos0=pos0),
        grid=(B // nb, L // tl),
        in_specs=[
            pl.BlockSpec((nb, tl, D_MODEL), lambda b, j: (b, j, 0)),
            _layer_spec(layer, (1, D_MODEL)),
            _layer_spec(0, (D_MODEL, MIX_WIDTH)),
            _layer_spec(layer, (1, MEM_HEAD_DIM)),
            _mem_spec(layer, nb),
            _mem_spec(layer, nb),
            hist_spec,
            _layer_spec(layer, (POOL_GROUPS, POOL_GROUP_DIM, POOL_GROUP_DIM)),
            _layer_spec(layer, (1, POOL_WIDTH)),
        ],
        out_specs=[
            pl.BlockSpec((nb, tl, MIX_WIDTH), lambda b, j: (b, j, 0)),
            pl.BlockSpec((1, nb, HIST_ROWS, POOL_WIDTH), lambda b, j: (0, b, 0, 0)),
        ],
        out_shape=[
            jax.ShapeDtypeStruct((B, L, MIX_WIDTH), BF16),
            jax.ShapeDtypeStruct((1, B, HIST_ROWS, POOL_WIDTH), F32),
        ],
        scratch_shapes=[pltpu.VMEM((nb, HIST_ROWS + tl, POOL_WIDTH), F32)],
        compiler_params=_params(2),
        name="mixer_pool",
    )(x, W["g_mix"], W["w_in"][layer], W["q_norm_mem"], mem_k, mem_v, hist, W["w_pool"], W["pool_scale"])


def _kv_proj_kernel(x_ref, g_ref, w_ref, wv_ref, kn_ref, bf_ref, k_ref, v_ref, lf_ref, kb_ref, vb_ref):
    nb, tl, _ = x_ref.shape
    rows = nb * tl
    x = x_ref[...].reshape(rows, D_MODEL)
    xn = _rms(x, g_ref[...]).astype(BF16)
    zt = _dot_nt(w_ref[...], xn)
    zv = _dot(xn, wv_ref[...])
    lo = lax.broadcasted_iota(jnp.int32, (1, LANES), 1) < FOX_HEAD_DIM
    for h in range(FOX_HEADS):
        pair = zv[:, (h // 2) * LANES:(h // 2 + 1) * LANES]
        base = pair if h % 2 == 0 else pltpu.roll(pair, FOX_HEAD_DIM, 1)
        vb_ref[:, h] = jnp.where(lo, base, 1.0).astype(BF16).reshape(nb, tl, LANES)
    k3 = zt[:FOX_WIDTH].reshape(FOX_HEADS, FOX_HEAD_DIM, rows)
    ms = jnp.mean(k3 * k3, axis=1, keepdims=True)
    k3 = (k3 * lax.rsqrt(ms + EPS)) * kn_ref[...]
    v3 = zt[FOX_WIDTH:2 * FOX_WIDTH].reshape(FOX_HEADS, FOX_HEAD_DIM, rows)
    t = -(zt[2 * FOX_WIDTH:2 * FOX_WIDTH + FOX_HEADS] + bf_ref[...])
    lf = -(jnp.maximum(t, 0.0) + jnp.log1p(jnp.exp(-jnp.abs(t))))
    for i in range(nb):
        cols = slice(i * tl, (i + 1) * tl)
        k_ref[i] = k3[:, :, cols]
        v_ref[i] = v3[:, :, cols]
        lf_ref[i] = lf[:, cols]
        kb_ref[i] = k3[:, :, cols].astype(BF16)


def _kv_proj(x, W, *, nb, tl):
    B, L, _ = x.shape
    hd = pl.BlockSpec((nb, FOX_HEADS, FOX_HEAD_DIM, tl), lambda b, j: (b, 0, 0, j))
    heads = jax.ShapeDtypeStruct((B, FOX_HEADS, FOX_HEAD_DIM, L), F32)
    return pl.pallas_call(
        _kv_proj_kernel,
        grid=(B // nb, L // tl),
        in_specs=[
            pl.BlockSpec((nb, tl, D_MODEL), lambda b, j: (b, j, 0)),
            _const_spec((1, D_MODEL)),
            _const_spec((KV_ROWS, D_MODEL)),
            _const_spec((D_MODEL, FOX_WIDTH)),
            _const_spec((1, FOX_HEAD_DIM, 1)),
            _const_spec((FOX_HEADS, 1)),
        ],
        out_specs=[hd, hd, pl.BlockSpec((nb, FOX_HEADS, tl), lambda b, j: (b, 0, j)), hd,
                   pl.BlockSpec((nb, FOX_HEADS, tl, LANES), lambda b, j: (b, 0, j, 0))],
        out_shape=[heads, heads, jax.ShapeDtypeStruct((B, FOX_HEADS, L), F32),
                   jax.ShapeDtypeStruct((B, FOX_HEADS, FOX_HEAD_DIM, L), BF16),
                   jax.ShapeDtypeStruct((B, FOX_HEADS, L, LANES), BF16)],
        compiler_params=_params(2),
        name="kv_proj",
    )(x, W["g_kv"], W["w_kv_t"], W["w_v"], W["kn_col"], W["b_f"])


def _cumsum_kernel(lf_ref, f_ref):
    rows, n = lf_ref.shape
    r = lax.broadcasted_iota(jnp.int32, (LANES, LANES), 0)
    c = lax.broadcasted_iota(jnp.int32, (LANES, LANES), 1)
    tri = jnp.where(r <= c, 1.0, 0.0).astype(BF16)
    carry = jnp.zeros((rows, 1), F32)
    for ch in range(n // LANES):
        x = lf_ref[:, ch * LANES:(ch + 1) * LANES]
        hi = x.astype(BF16)
        r1 = x - hi.astype(F32)
        mid = r1.astype(BF16)
        low = (r1 - mid.astype(F32)).astype(BF16)
        y = (_dot(hi, tri) + _dot(mid, tri)) + _dot(low, tri) + carry
        f_ref[:, ch * LANES:(ch + 1) * LANES] = y
        carry = y[:, LANES - 1:LANES]


def _cumsum_lanes(lf_t):
    return pl.pallas_call(
        _cumsum_kernel,
        out_shape=jax.ShapeDtypeStruct(lf_t.shape, F32),
        name="logf_cumsum",
    )(lf_t)


def _tri_unrank(t, n):
    row = sum((t >= k * (k + 1) // 2).astype(jnp.int32) for k in range(1, n))
    return row, t - row * (row + 1) // 2


def _split3(x):
    hi = x.astype(BF16).astype(F32)
    r1 = x - hi
    mid = r1.astype(BF16).astype(F32)
    low = (r1 - mid).astype(BF16).astype(F32)
    return hi, mid, low


def _mixer_fox_prompt_kernel(x_ref, g_ref, w_in_ref, qn_ref, mk_ref, mv_ref, qnf_ref,
                             kt_ref, v_ref, ft_ref, fq_ref, cat_ref, q_sc, m_sc, acc_sc, mask_sc,
                             *, nkb):
    _, tl, _ = x_ref.shape
    tk = kt_ref.shape[-1]
    j, kb = _tri_unrank(pl.program_id(1), nkb)
    lane = lax.broadcasted_iota(jnp.int32, (1, LANES), 1)
    row = lax.broadcasted_iota(jnp.int32, (AUG_ROWS, 1), 0)

    @pl.when(kb == 0)
    def _():
        z = _in_proj(x_ref, g_ref, w_in_ref)
        _mem_attend_paired(z[:, FOX_WIDTH:], qn_ref[0], mk_ref, mv_ref, 0, cat_ref, FOX_WIDTH)
        qs = _rms_head64(z[:, :FOX_WIDTH], qnf_ref[0])
        for h in range(FOX_HEADS):
            base = qs[h // 2] if h % 2 == 0 else pltpu.roll(qs[h // 2], FOX_HEAD_DIM, 1)
            hi, mid, low = _split3(fq_ref[0, :, h:h + 1] * LOG2E)
            tail = jnp.where(lane < FOX_HEAD_DIM + 3, 1.0,
                             jnp.where(lane == FOX_HEAD_DIM + 3, hi,
                                       jnp.where(lane == FOX_HEAD_DIM + 4, mid,
                                                 jnp.where(lane == FOX_HEAD_DIM + 5, low, 0.0))))
            q_sc[h] = jnp.where(lane < FOX_HEAD_DIM, base * (FOX_SCALE * LOG2E), tail).astype(BF16)
        m_sc[...] = jnp.full(m_sc.shape, -jnp.inf, F32)
        acc_sc[...] = jnp.zeros(acc_sc.shape, F32)
        causal = (lax.broadcasted_iota(jnp.int32, (tl, 1), 0)
                  >= lax.broadcasted_iota(jnp.int32, (1, tk), 1))
        mask_sc[...] = jnp.where(causal, 0.0, -jnp.inf)

    def k_aug(h):
        hi, mid, low = _split3(ft_ref[0, 0, h:h + 1, :] * (-LOG2E))
        aug = jnp.where(row == 0, hi, jnp.where(row == 1, mid, jnp.where(row == 2, low,
                        jnp.where(row < 6, 1.0, 0.0)))).astype(BF16)
        pad = jnp.zeros((FOX_HEAD_DIM - AUG_ROWS, tk), BF16)
        return jnp.concatenate([kt_ref[0, h], aug, pad], axis=0)

    def attend(masked):
        zk = jnp.zeros((LANES, tk), BF16)
        zv = jnp.zeros((tk, LANES), BF16)
        for c in range(FOX_HEADS // 2):
            h0, h1 = 2 * c, 2 * c + 1
            k_pair = jnp.concatenate([jnp.concatenate([k_aug(h0), zk], axis=1),
                                      jnp.concatenate([zk, k_aug(h1)], axis=1)], axis=0)
            v_pair = jnp.concatenate([jnp.concatenate([v_ref[0, h0], zv], axis=1),
                                      jnp.concatenate([zv, v_ref[0, h1]], axis=1)], axis=0)
            q_pair = jnp.concatenate([q_sc[h0], q_sc[h1]], axis=1)
            s = _dot(q_pair, k_pair)
            ps, alphas = [], []
            for h, sh in ((h0, s[:, :tk]), (h1, s[:, tk:])):
                if masked:
                    sh = sh + mask_sc[...]
                m_old = m_sc[h]
                m_new = jnp.maximum(m_old, jnp.max(sh, axis=-1, keepdims=True))
                ps.append(jnp.exp2(sh - m_new[:, 0:1]).astype(BF16))
                alphas.append(jnp.exp2(m_old - m_new))
                m_sc[h] = m_new
            o = _dot(jnp.concatenate(ps, axis=1), v_pair)
            acc_sc[h0] = alphas[0] * acc_sc[h0] + o[:, :LANES]
            acc_sc[h1] = alphas[1] * acc_sc[h1] + o[:, LANES:]

    @pl.when(kb < j)
    def _():
        attend(False)

    @pl.when(kb == j)
    def _():
        attend(True)
        for c in range(FOX_HEADS // 2):
            a0 = acc_sc[2 * c]
            a1 = acc_sc[2 * c + 1]
            o0 = a0 / pltpu.roll(a0, FOX_HEAD_DIM, 1)
            o1 = pltpu.roll(a1, FOX_HEAD_DIM, 1) / a1
            cat_ref[0, :, c * LANES:(c + 1) * LANES] = (
                jnp.where(lane < FOX_HEAD_DIM, o0, o1).astype(cat_ref.dtype))


def _mixer_fox_prompt(layer, x, kt_b, vt_b, f_t, f_q, mem_k, mem_v, W, *, tl):
    B, L, _ = x.shape
    nkb = L // tl
    ft4 = jnp.swapaxes(f_t.reshape(B, FOX_HEADS, nkb, tl), 1, 2)
    jj = lambda t: _tri_unrank(t, nkb)[0]
    kk = lambda t: _tri_unrank(t, nkb)[1]
    kt_spec = pl.BlockSpec((1, FOX_HEADS, FOX_HEAD_DIM, tl), lambda b, t: (b, 0, 0, kk(t)))
    v_spec = pl.BlockSpec((1, FOX_HEADS, tl, LANES), lambda b, t: (b, 0, kk(t), 0))
    return pl.pallas_call(
        functools.partial(_mixer_fox_prompt_kernel, nkb=nkb),
        grid=(B, nkb * (nkb + 1) // 2),
        in_specs=[
            pl.BlockSpec((1, tl, D_MODEL), lambda b, t: (b, jj(t), 0)),
            _layer_spec(layer, (1, D_MODEL)),
            _layer_spec(0, (D_MODEL, MIX_WIDTH)),
            _layer_spec(layer, (1, MEM_HEAD_DIM)),
            _mem_spec(layer, 1),
            _mem_spec(layer, 1),
            _layer_spec(layer - N_A, (1, LANES)),
            kt_spec,
            v_spec,
            pl.BlockSpec((1, 1, FOX_HEADS, tl), lambda b, t: (b, kk(t), 0, 0)),
            pl.BlockSpec((1, tl, FOX_HEADS), lambda b, t: (b, jj(t), 0)),
        ],
        out_specs=pl.BlockSpec((1, tl, MIX_WIDTH), lambda b, t: (b, jj(t), 0)),
        out_shape=jax.ShapeDtypeStruct((B, L, MIX_WIDTH), BF16),
        scratch_shapes=[
            pltpu.VMEM((FOX_HEADS, tl, LANES), BF16),
            pltpu.VMEM((FOX_HEADS, tl, LANES), F32),
            pltpu.VMEM((FOX_HEADS, tl, LANES), F32),
            pltpu.VMEM((tl, tl), F32),
        ],
        compiler_params=_params(2),
        name="mixer_fox_prompt",
    )(x, W["g_mix"], W["w_in"][layer], W["q_norm_mem"], mem_k, mem_v, W["qnf_pair"], kt_b, vt_b, ft4, f_q)


def _block_diag(blocks):
    z = jnp.zeros_like(blocks[0])
    n = len(blocks)
    return jnp.concatenate(
        [jnp.concatenate([b if j == i else z for j in range(n)], axis=1)
         for i, b in enumerate(blocks)], axis=0)


def _fox_sample_pre_kernel(x_ref, g_ref, w_in_ref, qn_ref, mk_ref, mv_ref, qnf_ref, q_ref, cm_ref):
    nb, tl, _ = x_ref.shape
    z = _in_proj(x_ref, g_ref, w_in_ref)
    for i in range(nb):
        zi = z[i * tl:(i + 1) * tl]
        qs = _rms_head64(zi[:, :FOX_WIDTH], qnf_ref[0])
        for c in range(FOX_HEADS // 2):
            q_ref[i, :, c * LANES:(c + 1) * LANES] = (qs[c] * FOX_SCALE).astype(BF16)
        _mem_attend(zi[:, FOX_WIDTH:], qn_ref[0], mk_ref, mv_ref, i, cm_ref, 0)


def _fox_sample_attn_kernel(q_ref, ktp_ref, vtp_ref, ktn_ref, vtn_ref, ft_ref, fq_ref, cat_ref):
    nb, tl, _ = q_ref.shape
    past = ktp_ref.shape[-1]
    causal = (lax.broadcasted_iota(jnp.int32, (tl, 1), 0)
              >= lax.broadcasted_iota(jnp.int32, (1, tl), 1))
    lo = lax.broadcasted_iota(jnp.int32, (1, LANES), 1) < FOX_HEAD_DIM
    gw = SAMPLE_HEAD_GROUP * FOX_HEAD_DIM
    for i in range(nb):
        for g in range(FOX_HEADS // SAMPLE_HEAD_GROUP):
            hs = tuple(range(g * SAMPLE_HEAD_GROUP, (g + 1) * SAMPLE_HEAD_GROUP))
            q = q_ref[i, :, g * gw:(g + 1) * gw]
            s_p = _dot(q, _block_diag([ktp_ref[i, h].astype(BF16) for h in hs]))
            s_n = _dot(q, _block_diag([ktn_ref[i, h].astype(BF16) for h in hs]))
            pp, pn, inv_l = [], [], []
            for n, h in enumerate(hs):
                fq = fq_ref[i, :, h:h + 1]
                sp = (s_p[:, n * past:(n + 1) * past] + fq) - ft_ref[i, h:h + 1, 0:past]
                sn = (s_n[:, n * tl:(n + 1) * tl] + fq) - ft_ref[i, h:h + 1, past:past + tl]
                sn = jnp.where(causal, sn, -jnp.inf)
                m = jnp.maximum(jnp.max(sp, axis=-1, keepdims=True),
                                jnp.max(sn, axis=-1, keepdims=True))
                ep = jnp.exp(sp - m)
                en = jnp.exp(sn - m)
                inv_l.append(1.0 / (jnp.sum(ep, axis=-1, keepdims=True)
                                    + jnp.sum(en, axis=-1, keepdims=True)))
                pp.append(ep.astype(BF16))
                pn.append(en.astype(BF16))
            o = (_dot_nt(jnp.concatenate(pp, axis=1),
                         _block_diag([vtp_ref[i, h].astype(BF16) for h in hs]))
                 + _dot_nt(jnp.concatenate(pn, axis=1),
                           _block_diag([vtn_ref[i, h].astype(BF16) for h in hs])))
            scale = jnp.concatenate([jnp.where(lo, inv_l[n], inv_l[n + 1])
                                     for n in range(0, SAMPLE_HEAD_GROUP, 2)], axis=1)
            cat_ref[i, :, g * gw:(g + 1) * gw] = (o * scale).astype(cat_ref.dtype)


def _mixer_fox_sample(layer, x, kt_past, vt_past, kt_new, vt_new, f_t, f_q, mem_k, mem_v, W, *,
                      nb_dense, nb_attn):
    B, L, _ = x.shape
    past = kt_past.shape[-1]
    lk_pad = f_t.shape[-1]
    blk = lambda nb, *tail: pl.BlockSpec((nb,) + tail, lambda b: (b,) + (0,) * len(tail))
    half = jax.ShapeDtypeStruct((B, L, FOX_WIDTH), BF16)
    q, cat_mem = pl.pallas_call(
        _fox_sample_pre_kernel,
        grid=(B // nb_dense,),
        in_specs=[
            blk(nb_dense, L, D_MODEL),
            _layer_spec(layer, (1, D_MODEL)),
            _layer_spec(0, (D_MODEL, MIX_WIDTH)),
            _layer_spec(layer, (1, MEM_HEAD_DIM)),
            _mem_spec(layer, nb_dense),
            _mem_spec(layer, nb_dense),
            _layer_spec(layer - N_A, (1, LANES)),
        ],
        out_specs=[blk(nb_dense, L, FOX_WIDTH), blk(nb_dense, L, MEM_WIDTH)],
        out_shape=[half, half],
        compiler_params=_params(1),
        name="fox_sample_pre",
    )(x, W["g_mix"], W["w_in"][layer], W["q_norm_mem"], mem_k, mem_v, W["qnf_pair"])
    cat_fox = pl.pallas_call(
        _fox_sample_attn_kernel,
        grid=(B // nb_attn,),
        in_specs=[
            blk(nb_attn, L, FOX_WIDTH),
            blk(nb_attn, FOX_HEADS, FOX_HEAD_DIM, past),
            blk(nb_attn, FOX_HEADS, FOX_HEAD_DIM, past),
            blk(nb_attn, FOX_HEADS, FOX_HEAD_DIM, L),
            blk(nb_attn, FOX_HEADS, FOX_HEAD_DIM, L),
            blk(nb_attn, FOX_HEADS, lk_pad),
            blk(nb_attn, L, FOX_HEADS),
        ],
        out_specs=blk(nb_attn, L, FOX_WIDTH),
        out_shape=half,
        compiler_params=_params(1),
        name="fox_sample_attn",
    )(q, kt_past, vt_past, kt_new, vt_new, f_t, f_q)
    return jnp.concatenate([cat_fox, cat_mem], axis=-1)


def _out_ffn_kernel(xa_ref, cata_ref, xb_ref, catb_ref, w_out_ref, g_ref, w_gu_ref, w_down_ref,
                    *rest, na, n_cast):
    if n_cast:
        src, (ya_ref, yb_ref), dst = rest[:4], rest[4:6], rest[6:]

        @pl.when(pl.program_id(0) < n_cast)
        def _():
            for s_ref, d_ref in zip(src, dst):
                d_ref[...] = s_ref[...].astype(BF16)
    else:
        ya_ref, yb_ref = rest

    def body(x_ref, cat_ref, y_ref):
        x1 = x_ref[...] + _dot(cat_ref[...], w_out_ref[0])
        xn = _rms(x1, g_ref[0]).astype(BF16)
        acc = x1
        for lo, hi in FFN_SPLITS:
            gate = _dot(xn, w_gu_ref[0, :, lo:hi])
            up = _dot(xn, w_gu_ref[0, :, D_FF + lo:D_FF + hi])
            h = (gate * (1.0 / (1.0 + jnp.exp(-gate)))) * up
            acc = acc + _dot(h.astype(BF16), w_down_ref[0, lo:hi, :])
        y_ref[...] = acc

    @pl.when(pl.program_id(0) < na)
    def _():
        body(xa_ref, cata_ref, ya_ref)

    @pl.when(pl.program_id(0) >= na)
    def _():
        body(xb_ref, catb_ref, yb_ref)


def _out_ffn(layer, xa, cata, xb, catb, W, next_f32=()):
    na = xa.shape[0] // ROW_BLOCK
    nb = xb.shape[0] // ROW_BLOCK
    a_map = lambda r: (jnp.minimum(r, na - 1), 0)
    b_map = lambda r: (jnp.maximum(r - na, 0), 0)
    n_cast = CAST_STEPS if next_f32 else 0
    assert n_cast <= na + nb
    chunk = lambda w: (1, w.shape[1] // CAST_STEPS, w.shape[2])
    src_specs = [pl.BlockSpec(chunk(w), lambda r: (layer + 1, jnp.minimum(r, CAST_STEPS - 1), 0))
                 for w in next_f32]
    dst_specs = [pl.BlockSpec(chunk(w), lambda r: (0, jnp.minimum(r, CAST_STEPS - 1), 0))
                 for w in next_f32]
    dst_shapes = [jax.ShapeDtypeStruct((1,) + w.shape[1:], BF16) for w in next_f32]
    outs = pl.pallas_call(
        functools.partial(_out_ffn_kernel, na=na, n_cast=n_cast),
        grid=(na + nb,),
        in_specs=[
            pl.BlockSpec((ROW_BLOCK, D_MODEL), a_map),
            pl.BlockSpec((ROW_BLOCK, MIX_WIDTH), a_map),
            pl.BlockSpec((ROW_BLOCK, D_MODEL), b_map),
            pl.BlockSpec((ROW_BLOCK, MIX_WIDTH), b_map),
            _layer_spec(0, (MIX_WIDTH, D_MODEL)),
            _layer_spec(layer, (1, D_MODEL)),
            _layer_spec(0, (D_MODEL, 2 * D_FF)),
            _layer_spec(0, (D_FF, D_MODEL)),
        ] + src_specs,
        out_specs=[pl.BlockSpec((ROW_BLOCK, D_MODEL), a_map),
                   pl.BlockSpec((ROW_BLOCK, D_MODEL), b_map)] + dst_specs,
        out_shape=[jax.ShapeDtypeStruct(xa.shape, F32), jax.ShapeDtypeStruct(xb.shape, F32)]
        + dst_shapes,
        compiler_params=_params(1),
        name="out_ffn",
    )(xa, cata, xb, catb, W["w_out"][layer], W["g_ffn"], W["w_gu"][layer], W["w_down"][layer],
      *next_f32)
    return outs[0], outs[1], outs[2:]


def _mixer(i, x, st, W):
    B, L, _ = x.shape
    if i < N_A:
        cat, state = _mixer_pool(i, x, st["hist"], st["mem_k"], st["mem_v"], W, pos0=st["pos0"],
                                 nb=st["nb"], tl=min(L, POOL_BLOCK))
        st["pool_states"].append(state[:, :, 1:, :])
        return cat
    past = st["past"]
    if i == N_A:
        kt_new, vt_new, lft_new, kt_b, v_b = _kv_proj(x, W, nb=st["nb"], tl=st["tl"])
        lft_all = lft_new if past is None else jnp.concatenate([past[2], lft_new], axis=2)
        lk = lft_all.shape[2]
        lk_pad = -(-lk // LANES) * LANES
        lf_t = jnp.pad(lft_all.reshape(B * FOX_HEADS, lk), ((0, 0), (0, lk_pad - lk)))
        f_t = _cumsum_lanes(lf_t).reshape(B, FOX_HEADS, lk_pad)
        st.update(kt_new=kt_new, vt_new=vt_new, lft_new=lft_new, kt_b=kt_b, v_b=v_b, f_t=f_t,
                  f_q=jnp.swapaxes(f_t[:, :, lk - L:lk], 1, 2))
    if past is None:
        return _mixer_fox_prompt(i, x, st["kt_b"], st["v_b"], st["f_t"], st["f_q"], st["mem_k"],
                                 st["mem_v"], W, tl=FOX_BLOCK)
    return _mixer_fox_sample(i, x, past[0], past[1], st["kt_new"], st["vt_new"], st["f_t"],
                             st["f_q"], st["mem_k"], st["mem_v"], W, nb_dense=st["nb"], nb_attn=4)


def _stream_outputs(st):
    return (jnp.concatenate(st["pool_states"], axis=0), jnp.transpose(st["kt_new"], (0, 3, 1, 2)),
            jnp.transpose(st["vt_new"], (0, 3, 1, 2)), jnp.swapaxes(st["lft_new"], 1, 2))


def kernel(x_prompt, x_sample, state_pool, cache_fox_k, cache_fox_v, cache_fox_logf, cache_mem_k,
           cache_mem_v, mem_prompt, g_mix, w_in, w_out, q_norm_mem, g_mem, w_mem_kv, k_norm_mem,
           w_pool, pool_scale, q_norm_fox, g_kv, w_kv, k_norm_fox, b_f, g_ffn, w_gu, w_down):
    B, L, _ = x_prompt.shape
    SB, SL, _ = x_sample.shape
    big = (w_in, w_out, w_gu, w_down)
    W = dict(
        g_mix=g_mix.reshape(DEPTH, 1, D_MODEL),
        q_norm_mem=q_norm_mem.reshape(DEPTH, 1, MEM_HEAD_DIM), w_pool=w_pool.astype(BF16),
        pool_scale=pool_scale.reshape(N_A, 1, POOL_WIDTH),
        qnf_pair=jnp.tile(q_norm_fox, (1, 2)).reshape(DEPTH - N_A, 1, LANES),
        g_kv=g_kv.reshape(1, D_MODEL),
        w_kv_t=jnp.pad(w_kv.T, ((0, KV_ROWS - w_kv.shape[1]), (0, 0))).astype(BF16),
        w_v=w_kv[:, FOX_WIDTH:2 * FOX_WIDTH].astype(BF16),
        kn_col=k_norm_fox.reshape(1, FOX_HEAD_DIM, 1), b_f=b_f.reshape(FOX_HEADS, 1),
        g_ffn=g_ffn.reshape(DEPTH, 1, D_MODEL))

    mem_k_p, mem_v_p, first = _mem_kv(
        mem_prompt, g_mem.reshape(DEPTH, 1, D_MODEL), w_mem_kv.astype(BF16),
        k_norm_mem.reshape(DEPTH, 1, MEM_HEAD_DIM), big)
    for name, w0 in zip(("w_in", "w_out", "w_gu", "w_down"), first):
        W[name] = [w0]
    prompt = dict(pos0=0, hist=jnp.zeros((N_A, B, HIST_ROWS, POOL_WIDTH), F32), past=None,
                  mem_k=mem_k_p, mem_v=mem_v_p, nb=1, tl=ROW_BLOCK, pool_states=[])
    sample = dict(
        pos0=PAST_LEN, hist=jnp.pad(state_pool, ((0, 0), (0, 0), (1, 0), (0, 0))),
        past=(jnp.transpose(cache_fox_k, (0, 2, 3, 1)), jnp.transpose(cache_fox_v, (0, 2, 3, 1)),
              jnp.swapaxes(cache_fox_logf, 1, 2)),
        mem_k=cache_mem_k.reshape(DEPTH, SB, MEM_ROWS, MEM_HEAD_DIM),
        mem_v=cache_mem_v.reshape(DEPTH, SB, MEM_ROWS, MEM_HEAD_DIM),
        nb=ROW_BLOCK // SL, tl=SL, pool_states=[])

    y_p, y_s = x_prompt, x_sample
    for i in range(DEPTH):
        cat_p = _mixer(i, y_p, prompt, W)
        cat_s = _mixer(i, y_s, sample, W)
        y_p, y_s, nxt = _out_ffn(i, y_p.reshape(B * L, D_MODEL), cat_p.reshape(B * L, MIX_WIDTH),
                                 y_s.reshape(SB * SL, D_MODEL), cat_s.reshape(SB * SL, MIX_WIDTH),
                                 W, next_f32=big if i + 1 < DEPTH else ())
        for name, w_next in zip(("w_in", "w_out", "w_gu", "w_down"), nxt):
            W[name].append(w_next)
        y_p = y_p.reshape(B, L, D_MODEL)
        y_s = y_s.reshape(SB, SL, D_MODEL)

    pool_p, fox_k_p, fox_v_p, fox_lf_p = _stream_outputs(prompt)
    pool_s, fox_k_s, fox_v_s, fox_lf_s = _stream_outputs(sample)
    mem_shape = (DEPTH, B, MEM_TOKENS, MEM_HEADS, MEM_HEAD_DIM)
    return (y_p, y_s, pool_p, fox_k_p, fox_v_p, fox_lf_p, mem_k_p.reshape(mem_shape),
            mem_v_p.reshape(mem_shape), pool_s, fox_k_s, fox_v_s, fox_lf_s)
```

```python
import functools

import jax
import jax.numpy as jnp
from jax import lax
from jax.experimental import pallas as pl
from jax.experimental.pallas import tpu as pltpu

F32 = jnp.float32
BF16 = jnp.bfloat16

D_MODEL = 1024
DEPTH = 4
N_A = DEPTH // 2
PAST_LEN = 1024
POOL_WINDOWS = (2, 4, 8, 16)
POOL_GROUPS = len(POOL_WINDOWS)
POOL_WIDTH = D_MODEL // 2
POOL_GROUP_DIM = POOL_WIDTH // POOL_GROUPS
POOL_HIST = max(POOL_WINDOWS) - 1
HIST_ROWS = POOL_HIST + 1
FOX_HEAD_DIM = 64
FOX_WIDTH = D_MODEL // 2
FOX_HEADS = FOX_WIDTH // FOX_HEAD_DIM
MEM_TOKENS = 256
MEM_HEADS = 4
MEM_WIDTH = D_MODEL // 2
MEM_HEAD_DIM = MEM_WIDTH // MEM_HEADS
MIX_WIDTH = POOL_WIDTH + MEM_WIDTH
D_FF = ((8 * D_MODEL // 3 + 255) // 256) * 256
EPS = 1e-6
FOX_SCALE = FOX_HEAD_DIM ** -0.5
MEM_SCALE = MEM_HEAD_DIM ** -0.5
LOG2E = 1.4426950408889634
AUG_ROWS = 16

LANES = 128
ROW_BLOCK = 512
FOX_BLOCK = 512
POOL_BLOCK = 1024
KV_BLOCK = 1024
CAST_STEPS = 16
SAMPLE_HEAD_GROUP = 4
KV_ROWS = 2 * FOX_WIDTH + 16
MXU_DIM = 256
FFN_SPLITS = ((0, 6 * MXU_DIM), (6 * MXU_DIM, D_FF))
VMEM_LIMIT = 56 * 1024 * 1024


def _dot(a, b):
    return jnp.dot(a, b, preferred_element_type=F32)


def _dot_nt(a, b):
    return lax.dot_general(a, b, (((1,), (1,)), ((), ())), preferred_element_type=F32)


def _rms(x, g):
    ms = jnp.mean(x * x, axis=-1, keepdims=True)
    return (x * lax.rsqrt(ms + EPS)) * g


def _rms_head64(x, g_pair):
    lo = lax.broadcasted_iota(jnp.int32, (1, LANES), 1) < FOX_HEAD_DIM
    outs = []
    for c in range(x.shape[-1] // LANES):
        xc = x[:, c * LANES:(c + 1) * LANES]
        sq = xc * xc
        s_lo = jnp.sum(jnp.where(lo, sq, 0.0), axis=-1, keepdims=True)
        s_hi = jnp.sum(jnp.where(lo, 0.0, sq), axis=-1, keepdims=True)
        ms = jnp.where(lo, s_lo, s_hi) * (1.0 / FOX_HEAD_DIM)
        outs.append((xc * lax.rsqrt(ms + EPS)) * g_pair)
    return outs


def _const_spec(shape):
    return pl.BlockSpec(shape, lambda *_: (0,) * len(shape), pipeline_mode=pl.Buffered(1))


def _layer_spec(layer, shape):
    return pl.BlockSpec((1,) + shape, lambda *_: (layer,) + (0,) * len(shape),
                        pipeline_mode=pl.Buffered(1))


MEM_ROWS = MEM_TOKENS * MEM_HEADS


def _mem_spec(layer, nb):
    return pl.BlockSpec((1, nb, MEM_ROWS, MEM_HEAD_DIM), lambda b, *_: (layer, b, 0, 0))


def _head_rows(h):
    return pl.ds(h, MEM_TOKENS, stride=MEM_HEADS)


def _params(n_grid, flags=None):
    return pltpu.CompilerParams(
        dimension_semantics=("arbitrary",) * n_grid, vmem_limit_bytes=VMEM_LIMIT, flags=flags)


def _mem_kv_kernel(mem_ref, g_ref, w_ref, kn_ref, *rest):
    n = (len(rest) - 2) // 2
    src, (k_ref, v_ref), dst = rest[:n], rest[n:n + 2], rest[n + 2:]
    for s_ref, d_ref in zip(src, dst):
        d_ref[...] = s_ref[...].astype(BF16)
    nb = mem_ref.shape[0]
    x = mem_ref[...].reshape(nb * MEM_TOKENS, D_MODEL)
    kv = _dot(_rms(x, g_ref[0]).astype(BF16), w_ref[0].astype(BF16))
    for h in range(MEM_HEADS):
        ks = slice(h * MEM_HEAD_DIM, (h + 1) * MEM_HEAD_DIM)
        vs = slice(MEM_WIDTH + h * MEM_HEAD_DIM, MEM_WIDTH + (h + 1) * MEM_HEAD_DIM)
        k_ref[0, :, _head_rows(h), :] = _rms(kv[:, ks], kn_ref[0]).reshape(nb, MEM_TOKENS, MEM_HEAD_DIM)
        v_ref[0, :, _head_rows(h), :] = kv[:, vs].reshape(nb, MEM_TOKENS, MEM_HEAD_DIM)


def _mem_kv(mem, g_mem, w_mem_kv, k_norm_mem, first_f32):
    B = mem.shape[0]
    nb = 4
    steps = DEPTH * (B // nb)
    out = jax.ShapeDtypeStruct((DEPTH, B, MEM_ROWS, MEM_HEAD_DIM), F32)
    out_spec = pl.BlockSpec((1, nb, MEM_ROWS, MEM_HEAD_DIM), lambda i, b: (i, b, 0, 0))
    chunk = lambda w: (1, w.shape[1] // steps, w.shape[2])
    chunk_spec = lambda w: pl.BlockSpec(chunk(w), lambda i, b: (0, i * (B // nb) + b, 0))
    outs = pl.pallas_call(
        _mem_kv_kernel,
        grid=(DEPTH, B // nb),
        in_specs=[
            pl.BlockSpec((nb, MEM_TOKENS, D_MODEL), lambda i, b: (b, 0, 0)),
            pl.BlockSpec((1, 1, D_MODEL), lambda i, b: (i, 0, 0)),
            pl.BlockSpec((1, D_MODEL, 2 * MEM_WIDTH), lambda i, b: (i, 0, 0)),
            pl.BlockSpec((1, 1, MEM_HEAD_DIM), lambda i, b: (i, 0, 0)),
        ] + [chunk_spec(w) for w in first_f32],
        out_specs=[out_spec, out_spec] + [chunk_spec(w) for w in first_f32],
        out_shape=[out, out] + [jax.ShapeDtypeStruct((1,) + w.shape[1:], BF16) for w in first_f32],
        compiler_params=_params(2),
        name="mem_kv",
    )(mem, g_mem, w_mem_kv, k_norm_mem, *first_f32)
    return outs[0], outs[1], outs[2:]


def _in_proj(x_ref, g_ref, w_ref):
    nb, tl, _ = x_ref.shape
    x = x_ref[...].reshape(nb * tl, D_MODEL)
    return _dot(_rms(x, g_ref[0]).astype(BF16), w_ref[0])


def _mem_attend(zq, qn, mk_ref, mv_ref, i, cat_ref, col0):
    for h in range(MEM_HEADS):
        sl = slice(h * MEM_HEAD_DIM, (h + 1) * MEM_HEAD_DIM)
        q = _rms(zq[:, sl], qn).astype(BF16)
        s = _dot_nt(q, mk_ref[0, i, _head_rows(h), :].astype(BF16)) * (MEM_SCALE * LOG2E)
        p = jnp.exp2(s - jnp.max(s, axis=-1, keepdims=True)).astype(BF16)
        v = mv_ref[0, i, _head_rows(h), :].astype(BF16)
        o = _dot(p, jnp.concatenate([v, jnp.ones_like(v)], axis=-1))
        o = o[:, :MEM_HEAD_DIM] / o[:, MEM_HEAD_DIM:]
        cat_ref[i, :, col0 + h * MEM_HEAD_DIM:col0 + (h + 1) * MEM_HEAD_DIM] = o.astype(cat_ref.dtype)


def _mem_attend_paired(zq, qn, mk_ref, mv_ref, i, cat_ref, col0):
    z = jnp.zeros((MEM_TOKENS, MEM_HEAD_DIM), BF16)
    one = jnp.ones((MEM_TOKENS, MEM_HEAD_DIM), BF16)
    for c in range(MEM_HEADS // 2):
        hs = (2 * c, 2 * c + 1)
        q = jnp.concatenate(
            [_rms(zq[:, h * MEM_HEAD_DIM:(h + 1) * MEM_HEAD_DIM], qn) for h in hs], axis=1)
        k0, k1 = (mk_ref[0, i, _head_rows(h), :].astype(BF16) for h in hs)
        v0, v1 = (mv_ref[0, i, _head_rows(h), :].astype(BF16) for h in hs)
        k_pair = jnp.concatenate([jnp.concatenate([k0, z], axis=1),
                                  jnp.concatenate([z, k1], axis=1)], axis=0)
        v_pair = jnp.concatenate([jnp.concatenate([v0, one, z, z], axis=1),
                                  jnp.concatenate([z, z, v1, one], axis=1)], axis=0)
        s = _dot_nt(q.astype(BF16), k_pair) * (MEM_SCALE * LOG2E)
        p = jnp.concatenate(
            [jnp.exp2(sh - jnp.max(sh, axis=-1, keepdims=True))
             for sh in (s[:, :MEM_TOKENS], s[:, MEM_TOKENS:])], axis=1).astype(BF16)
        o = _dot(p, v_pair)
        for n, h in enumerate(hs):
            oh = o[:, 2 * n * MEM_HEAD_DIM:(2 * n + 1) * MEM_HEAD_DIM]
            lh = o[:, (2 * n + 1) * MEM_HEAD_DIM:(2 * n + 2) * MEM_HEAD_DIM]
            cat_ref[i, :, col0 + h * MEM_HEAD_DIM:col0 + (h + 1) * MEM_HEAD_DIM] = (
                (oh / lh).astype(cat_ref.dtype))


def _mixer_pool_kernel(x_ref, g_ref, w_in_ref, qn_ref, mk_ref, mv_ref, hist_ref, wp_ref, ps_ref,
                       cat_ref, state_ref, ubuf, *, pos0):
    nb, tl, _ = x_ref.shape
    j = pl.program_id(1)
    z = _in_proj(x_ref, g_ref, w_in_ref)

    @pl.when(j == 0)
    def _():
        ubuf[:, 0:HIST_ROWS, :] = hist_ref[0]

    pos = pos0 + j * tl + lax.broadcasted_iota(jnp.int32, (tl, 1), 0)
    for i in range(nb):
        zi = z[i * tl:(i + 1) * tl]
        u = zi[:, :POOL_WIDTH]
        ubuf[i, HIST_ROWS:HIST_ROWS + tl, :] = u
        for g, w in enumerate(POOL_WINDOWS):
            sl = slice(g * POOL_GROUP_DIM, (g + 1) * POOL_GROUP_DIM)
            ug = u[:, sl]
            acc = ug
            for k in range(1, w):
                acc = acc + ubuf[i, HIST_ROWS - k:HIST_ROWS - k + tl, sl]
            cnt = jnp.minimum(pos + 1, w).astype(F32)
            d = acc / cnt - ug
            y = _dot(d.astype(BF16), wp_ref[0, g]) * ps_ref[0, :, sl]
            cat_ref[i, :, sl] = y.astype(cat_ref.dtype)
        _mem_attend(zi[:, POOL_WIDTH:], qn_ref[0], mk_ref, mv_ref, i, cat_ref, POOL_WIDTH)
        tail = ubuf[i, tl:tl + HIST_ROWS, :]
        state_ref[0, i] = tail
        ubuf[i, 0:HIST_ROWS, :] = tail


def _mixer_pool(layer, x, hist, mem_k, mem_v, W, *, pos0, nb, tl):
    B, L, _ = x.shape
    assert tl >= HIST_ROWS and L % tl == 0 and B % nb == 0
    hist_spec = pl.BlockSpec((1, nb, HIST_ROWS, POOL_WIDTH), lambda b, j: (layer, b, 0, 0))
    return pl.pallas_call(
        functools.partial(_mixer_pool_kernel, pos0=pos0),
        grid=(B // nb, L // tl),
        in_specs=[
            pl.BlockSpec((nb, tl, D_MODEL), lambda b, j: (b, j, 0)),
            _layer_spec(layer, (1, D_MODEL)),
            _layer_spec(0, (D_MODEL, MIX_WIDTH)),
            _layer_spec(layer, (1, MEM_HEAD_DIM)),
            _mem_spec(layer, nb),
            _mem_spec(layer, nb),
            hist_spec,
            _layer_spec(layer, (POOL_GROUPS, POOL_GROUP_DIM, POOL_GROUP_DIM)),
            _layer_spec(layer, (1, POOL_WIDTH)),
        ],
        out_specs=[
            pl.BlockSpec((nb, tl, MIX_WIDTH), lambda b, j: (b, j, 0)),
            pl.BlockSpec((1, nb, HIST_ROWS, POOL_WIDTH), lambda b, j: (0, b, 0, 0)),
        ],
        out_shape=[
            jax.ShapeDtypeStruct((B, L, MIX_WIDTH), BF16),
            jax.ShapeDtypeStruct((1, B, HIST_ROWS, POOL_WIDTH), F32),
        ],
        scratch_shapes=[pltpu.VMEM((nb, HIST_ROWS + tl, POOL_WIDTH), F32)],
        compiler_params=_params(2),
        name="mixer_pool",
    )(x, W["g_mix"], W["w_in"][layer], W["q_norm_mem"], mem_k, mem_v, hist, W["w_pool"], W["pool_scale"])


def _kv_proj_kernel(x_ref, g_ref, w_ref, wv_ref, kn_ref, bf_ref, k_ref, v_ref, lf_ref, kb_ref, vb_ref):
    nb, tl, _ = x_ref.shape
    rows = nb * tl
    x = x_ref[...].reshape(rows, D_MODEL)
    xn = _rms(x, g_ref[...]).astype(BF16)
    zt = _dot_nt(w_ref[...], xn)
    zv = _dot(xn, wv_ref[...])
    lo = lax.broadcasted_iota(jnp.int32, (1, LANES), 1) < FOX_HEAD_DIM
    for h in range(FOX_HEADS):
        pair = zv[:, (h // 2) * LANES:(h // 2 + 1) * LANES]
        base = pair if h % 2 == 0 else pltpu.roll(pair, FOX_HEAD_DIM, 1)
        vb_ref[:, h] = jnp.where(lo, base, 1.0).astype(BF16).reshape(nb, tl, LANES)
    k3 = zt[:FOX_WIDTH].reshape(FOX_HEADS, FOX_HEAD_DIM, rows)
    ms = jnp.mean(k3 * k3, axis=1, keepdims=True)
    k3 = (k3 * lax.rsqrt(ms + EPS)) * kn_ref[...]
    v3 = zt[FOX_WIDTH:2 * FOX_WIDTH].reshape(FOX_HEADS, FOX_HEAD_DIM, rows)
    t = -(zt[2 * FOX_WIDTH:2 * FOX_WIDTH + FOX_HEADS] + bf_ref[...])
    lf = -(jnp.maximum(t, 0.0) + jnp.log1p(jnp.exp(-jnp.abs(t))))
    for i in range(nb):
        cols = slice(i * tl, (i + 1) * tl)
        k_ref[i] = k3[:, :, cols]
        v_ref[i] = v3[:, :, cols]
        lf_ref[i] = lf[:, cols]
        kb_ref[i] = k3[:, :, cols].astype(BF16)


def _kv_proj(x, W, *, nb, tl):
    B, L, _ = x.shape
    hd = pl.BlockSpec((nb, FOX_HEADS, FOX_HEAD_DIM, tl), lambda b, j: (b, 0, 0, j))
    heads = jax.ShapeDtypeStruct((B, FOX_HEADS, FOX_HEAD_DIM, L), F32)
    return pl.pallas_call(
        _kv_proj_kernel,
        grid=(B // nb, L // tl),
        in_specs=[
            pl.BlockSpec((nb, tl, D_MODEL), lambda b, j: (b, j, 0)),
            _const_spec((1, D_MODEL)),
            _const_spec((KV_ROWS, D_MODEL)),
            _const_spec((D_MODEL, FOX_WIDTH)),
            _const_spec((1, FOX_HEAD_DIM, 1)),
            _const_spec((FOX_HEADS, 1)),
        ],
        out_specs=[hd, hd, pl.BlockSpec((nb, FOX_HEADS, tl), lambda b, j: (b, 0, j)), hd,
                   pl.BlockSpec((nb, FOX_HEADS, tl, LANES), lambda b, j: (b, 0, j, 0))],
        out_shape=[heads, heads, jax.ShapeDtypeStruct((B, FOX_HEADS, L), F32),
                   jax.ShapeDtypeStruct((B, FOX_HEADS, FOX_HEAD_DIM, L), BF16),
                   jax.ShapeDtypeStruct((B, FOX_HEADS, L, LANES), BF16)],
        compiler_params=_params(2),
        name="kv_proj",
    )(x, W["g_kv"], W["w_kv_t"], W["w_v"], W["kn_col"], W["b_f"])


def _cumsum_kernel(lf_ref, f_ref):
    rows, n = lf_ref.shape
    r = lax.broadcasted_iota(jnp.int32, (LANES, LANES), 0)
    c = lax.broadcasted_iota(jnp.int32, (LANES, LANES), 1)
    tri = jnp.where(r <= c, 1.0, 0.0).astype(BF16)
    carry = jnp.zeros((rows, 1), F32)
    for ch in range(n // LANES):
        x = lf_ref[:, ch * LANES:(ch + 1) * LANES]
        hi = x.astype(BF16)
        r1 = x - hi.astype(F32)
        mid = r1.astype(BF16)
        low = (r1 - mid.astype(F32)).astype(BF16)
        y = (_dot(hi, tri) + _dot(mid, tri)) + _dot(low, tri) + carry
        f_ref[:, ch * LANES:(ch + 1) * LANES] = y
        carry = y[:, LANES - 1:LANES]


def _cumsum_lanes(lf_t):
    return pl.pallas_call(
        _cumsum_kernel,
        out_shape=jax.ShapeDtypeStruct(lf_t.shape, F32),
        name="logf_cumsum",
    )(lf_t)


def _tri_unrank(t, n):
    row = sum((t >= k * (k + 1) // 2).astype(jnp.int32) for k in range(1, n))
    return row, t - row * (row + 1) // 2


def _split3(x):
    hi = x.astype(BF16).astype(F32)
    r1 = x - hi
    mid = r1.astype(BF16).astype(F32)
    low = (r1 - mid).astype(BF16).astype(F32)
    return hi, mid, low


def _mixer_fox_prompt_kernel(x_ref, g_ref, w_in_ref, qn_ref, mk_ref, mv_ref, qnf_ref,
                             kt_ref, v_ref, ft_ref, fq_ref, cat_ref, q_sc, m_sc, acc_sc, mask_sc,
                             *, nkb):
    _, tl, _ = x_ref.shape
    tk = kt_ref.shape[-1]
    j, kb = _tri_unrank(pl.program_id(1), nkb)
    lane = lax.broadcasted_iota(jnp.int32, (1, LANES), 1)
    row = lax.broadcasted_iota(jnp.int32, (AUG_ROWS, 1), 0)

    @pl.when(kb == 0)
    def _():
        z = _in_proj(x_ref, g_ref, w_in_ref)
        _mem_attend_paired(z[:, FOX_WIDTH:], qn_ref[0], mk_ref, mv_ref, 0, cat_ref, FOX_WIDTH)
        qs = _rms_head64(z[:, :FOX_WIDTH], qnf_ref[0])
        for h in range(FOX_HEADS):
            base = qs[h // 2] if h % 2 == 0 else pltpu.roll(qs[h // 2], FOX_HEAD_DIM, 1)
            hi, mid, low = _split3(fq_ref[0, :, h:h + 1] * LOG2E)
            tail = jnp.where(lane < FOX_HEAD_DIM + 3, 1.0,
                             jnp.where(lane == FOX_HEAD_DIM + 3, hi,
                                       jnp.where(lane == FOX_HEAD_DIM + 4, mid,
                                                 jnp.where(lane == FOX_HEAD_DIM + 5, low, 0.0))))
            q_sc[h] = jnp.where(lane < FOX_HEAD_DIM, base * (FOX_SCALE * LOG2E), tail).astype(BF16)
        m_sc[...] = jnp.full(m_sc.shape, -jnp.inf, F32)
        acc_sc[...] = jnp.zeros(acc_sc.shape, F32)
        causal = (lax.broadcasted_iota(jnp.int32, (tl, 1), 0)
                  >= lax.broadcasted_iota(jnp.int32, (1, tk), 1))
        mask_sc[...] = jnp.where(causal, 0.0, -jnp.inf)

    def k_aug(h):
        hi, mid, low = _split3(ft_ref[0, 0, h:h + 1, :] * (-LOG2E))
        aug = jnp.where(row == 0, hi, jnp.where(row == 1, mid, jnp.where(row == 2, low,
                        jnp.where(row < 6, 1.0, 0.0)))).astype(BF16)
        pad = jnp.zeros((FOX_HEAD_DIM - AUG_ROWS, tk), BF16)
        return jnp.concatenate([kt_ref[0, h], aug, pad], axis=0)

    def attend(masked):
        zk = jnp.zeros((LANES, tk), BF16)
        zv = jnp.zeros((tk, LANES), BF16)
        for c in range(FOX_HEADS // 2):
            h0, h1 = 2 * c, 2 * c + 1
            k_pair = jnp.concatenate([jnp.concatenate([k_aug(h0), zk], axis=1),
                                      jnp.concatenate([zk, k_aug(h1)], axis=1)], axis=0)
            v_pair = jnp.concatenate([jnp.concatenate([v_ref[0, h0], zv], axis=1),
                                      jnp.concatenate([zv, v_ref[0, h1]], axis=1)], axis=0)
            q_pair = jnp.concatenate([q_sc[h0], q_sc[h1]], axis=1)
            s = _dot(q_pair, k_pair)
            ps, alphas = [], []
            for h, sh in ((h0, s[:, :tk]), (h1, s[:, tk:])):
                if masked:
                    sh = sh + mask_sc[...]
                m_old = m_sc[h]
                m_new = jnp.maximum(m_old, jnp.max(sh, axis=-1, keepdims=True))
                ps.append(jnp.exp2(sh - m_new[:, 0:1]).astype(BF16))
                alphas.append(jnp.exp2(m_old - m_new))
                m_sc[h] = m_new
            o = _dot(jnp.concatenate(ps, axis=1), v_pair)
            acc_sc[h0] = alphas[0] * acc_sc[h0] + o[:, :LANES]
            acc_sc[h1] = alphas[1] * acc_sc[h1] + o[:, LANES:]

    @pl.when(kb < j)
    def _():
        attend(False)

    @pl.when(kb == j)
    def _():
        attend(True)
        for c in range(FOX_HEADS // 2):
            a0 = acc_sc[2 * c]
            a1 = acc_sc[2 * c + 1]
            o0 = a0 / pltpu.roll(a0, FOX_HEAD_DIM, 1)
            o1 = pltpu.roll(a1, FOX_HEAD_DIM, 1) / a1
            cat_ref[0, :, c * LANES:(c + 1) * LANES] = (
                jnp.where(lane < FOX_HEAD_DIM, o0, o1).astype(cat_ref.dtype))


def _mixer_fox_prompt(layer, x, kt_b, vt_b, f_t, f_q, mem_k, mem_v, W, *, tl):
    B, L, _ = x.shape
    nkb = L // tl
    ft4 = jnp.swapaxes(f_t.reshape(B, FOX_HEADS, nkb, tl), 1, 2)
    jj = lambda t: _tri_unrank(t, nkb)[0]
    kk = lambda t: _tri_unrank(t, nkb)[1]
    kt_spec = pl.BlockSpec((1, FOX_HEADS, FOX_HEAD_DIM, tl), lambda b, t: (b, 0, 0, kk(t)))
    v_spec = pl.BlockSpec((1, FOX_HEADS, tl, LANES), lambda b, t: (b, 0, kk(t), 0))
    return pl.pallas_call(
        functools.partial(_mixer_fox_prompt_kernel, nkb=nkb),
        grid=(B, nkb * (nkb + 1) // 2),
        in_specs=[
            pl.BlockSpec((1, tl, D_MODEL), lambda b, t: (b, jj(t), 0)),
            _layer_spec(layer, (1, D_MODEL)),
            _layer_spec(0, (D_MODEL, MIX_WIDTH)),
            _layer_spec(layer, (1, MEM_HEAD_DIM)),
            _mem_spec(layer, 1),
            _mem_spec(layer, 1),
            _layer_spec(layer - N_A, (1, LANES)),
            kt_spec,
            v_spec,
            pl.BlockSpec((1, 1, FOX_HEADS, tl), lambda b, t: (b, kk(t), 0, 0)),
            pl.BlockSpec((1, tl, FOX_HEADS), lambda b, t: (b, jj(t), 0)),
        ],
        out_specs=pl.BlockSpec((1, tl, MIX_WIDTH), lambda b, t: (b, jj(t), 0)),
        out_shape=jax.ShapeDtypeStruct((B, L, MIX_WIDTH), BF16),
        scratch_shapes=[
            pltpu.VMEM((FOX_HEADS, tl, LANES), BF16),
            pltpu.VMEM((FOX_HEADS, tl, LANES), F32),
            pltpu.VMEM((FOX_HEADS, tl, LANES), F32),
            pltpu.VMEM((tl, tl), F32),
        ],
        compiler_params=_params(2),
        name="mixer_fox_prompt",
    )(x, W["g_mix"], W["w_in"][layer], W["q_norm_mem"], mem_k, mem_v, W["qnf_pair"], kt_b, vt_b, ft4, f_q)


def _block_diag(blocks):
    z = jnp.zeros_like(blocks[0])
    n = len(blocks)
    return jnp.concatenate(
        [jnp.concatenate([b if j == i else z for j in range(n)], axis=1)
         for i, b in enumerate(blocks)], axis=0)


def _fox_sample_pre_kernel(x_ref, g_ref, w_in_ref, qn_ref, mk_ref, mv_ref, qnf_ref, q_ref, cm_ref):
    nb, tl, _ = x_ref.shape
    z = _in_proj(x_ref, g_ref, w_in_ref)
    for i in range(nb):
        zi = z[i * tl:(i + 1) * tl]
        qs = _rms_head64(zi[:, :FOX_WIDTH], qnf_ref[0])
        for c in range(FOX_HEADS // 2):
            q_ref[i, :, c * LANES:(c + 1) * LANES] = (qs[c] * FOX_SCALE).astype(BF16)
        _mem_attend(zi[:, FOX_WIDTH:], qn_ref[0], mk_ref, mv_ref, i, cm_ref, 0)


def _fox_sample_attn_kernel(q_ref, ktp_ref, vtp_ref, ktn_ref, vtn_ref, ft_ref, fq_ref, cat_ref):
    nb, tl, _ = q_ref.shape
    past = ktp_ref.shape[-1]
    causal = (lax.broadcasted_iota(jnp.int32, (tl, 1), 0)
              >= lax.broadcasted_iota(jnp.int32, (1, tl), 1))
    lo = lax.broadcasted_iota(jnp.int32, (1, LANES), 1) < FOX_HEAD_DIM
    gw = SAMPLE_HEAD_GROUP * FOX_HEAD_DIM
    for i in range(nb):
        for g in range(FOX_HEADS // SAMPLE_HEAD_GROUP):
            hs = tuple(range(g * SAMPLE_HEAD_GROUP, (g + 1) * SAMPLE_HEAD_GROUP))
            q = q_ref[i, :, g * gw:(g + 1) * gw]
            s_p = _dot(q, _block_diag([ktp_ref[i, h].astype(BF16) for h in hs]))
            s_n = _dot(q, _block_diag([ktn_ref[i, h].astype(BF16) for h in hs]))
            pp, pn, inv_l = [], [], []
            for n, h in enumerate(hs):
                fq = fq_ref[i, :, h:h + 1]
                sp = (s_p[:, n * past:(n + 1) * past] + fq) - ft_ref[i, h:h + 1, 0:past]
                sn = (s_n[:, n * tl:(n + 1) * tl] + fq) - ft_ref[i, h:h + 1, past:past + tl]
                sn = jnp.where(causal, sn, -jnp.inf)
                m = jnp.maximum(jnp.max(sp, axis=-1, keepdims=True),
                                jnp.max(sn, axis=-1, keepdims=True))
                ep = jnp.exp(sp - m)
                en = jnp.exp(sn - m)
                inv_l.append(1.0 / (jnp.sum(ep, axis=-1, keepdims=True)
                                    + jnp.sum(en, axis=-1, keepdims=True)))
                pp.append(ep.astype(BF16))
                pn.append(en.astype(BF16))
            o = (_dot_nt(jnp.concatenate(pp, axis=1),
                         _block_diag([vtp_ref[i, h].astype(BF16) for h in hs]))
                 + _dot_nt(jnp.concatenate(pn, axis=1),
                           _block_diag([vtn_ref[i, h].astype(BF16) for h in hs])))
            scale = jnp.concatenate([jnp.where(lo, inv_l[n], inv_l[n + 1])
                                     for n in range(0, SAMPLE_HEAD_GROUP, 2)], axis=1)
            cat_ref[i, :, g * gw:(g + 1) * gw] = (o * scale).astype(cat_ref.dtype)


def _mixer_fox_sample(layer, x, kt_past, vt_past, kt_new, vt_new, f_t, f_q, mem_k, mem_v, W, *,
                      nb_dense, nb_attn):
    B, L, _ = x.shape
    past = kt_past.shape[-1]
    lk_pad = f_t.shape[-1]
    blk = lambda nb, *tail: pl.BlockSpec((nb,) + tail, lambda b: (b,) + (0,) * len(tail))
    half = jax.ShapeDtypeStruct((B, L, FOX_WIDTH), BF16)
    q, cat_mem = pl.pallas_call(
        _fox_sample_pre_kernel,
        grid=(B // nb_dense,),
        in_specs=[
            blk(nb_dense, L, D_MODEL),
            _layer_spec(layer, (1, D_MODEL)),
            _layer_spec(0, (D_MODEL, MIX_WIDTH)),
            _layer_spec(layer, (1, MEM_HEAD_DIM)),
            _mem_spec(layer, nb_dense),
            _mem_spec(layer, nb_dense),
            _layer_spec(layer - N_A, (1, LANES)),
        ],
        out_specs=[blk(nb_dense, L, FOX_WIDTH), blk(nb_dense, L, MEM_WIDTH)],
        out_shape=[half, half],
        compiler_params=_params(1),
        name="fox_sample_pre",
    )(x, W["g_mix"], W["w_in"][layer], W["q_norm_mem"], mem_k, mem_v, W["qnf_pair"])
    cat_fox = pl.pallas_call(
        _fox_sample_attn_kernel,
        grid=(B // nb_attn,),
        in_specs=[
            blk(nb_attn, L, FOX_WIDTH),
            blk(nb_attn, FOX_HEADS, FOX_HEAD_DIM, past),
            blk(nb_attn, FOX_HEADS, FOX_HEAD_DIM, past),
            blk(nb_attn, FOX_HEADS, FOX_HEAD_DIM, L),
            blk(nb_attn, FOX_HEADS, FOX_HEAD_DIM, L),
            blk(nb_attn, FOX_HEADS, lk_pad),
            blk(nb_attn, L, FOX_HEADS),
        ],
        out_specs=blk(nb_attn, L, FOX_WIDTH),
        out_shape=half,
        compiler_params=_params(1),
        name="fox_sample_attn",
    )(q, kt_past, vt_past, kt_new, vt_new, f_t, f_q)
    return jnp.concatenate([cat_fox, cat_mem], axis=-1)


def _out_ffn_kernel(xa_ref, cata_ref, xb_ref, catb_ref, w_out_ref, g_ref, w_gu_ref, w_down_ref,
                    *rest, na, n_cast):
    if n_cast:
        src, (ya_ref, yb_ref), dst = rest[:4], rest[4:6], rest[6:]

        @pl.when(pl.program_id(0) < n_cast)
        def _():
            for s_ref, d_ref in zip(src, dst):
                d_ref[...] = s_ref[...].astype(BF16)
    else:
        ya_ref, yb_ref = rest

    def body(x_ref, cat_ref, y_ref):
        x1 = x_ref[...] + _dot(cat_ref[...], w_out_ref[0])
        xn = _rms(x1, g_ref[0]).astype(BF16)
        acc = x1
        for lo, hi in FFN_SPLITS:
            gate = _dot(xn, w_gu_ref[0, :, lo:hi])
            up = _dot(xn, w_gu_ref[0, :, D_FF + lo:D_FF + hi])
            h = (gate * (1.0 / (1.0 + jnp.exp(-gate)))) * up
            acc = acc + _dot(h.astype(BF16), w_down_ref[0, lo:hi, :])
        y_ref[...] = acc

    @pl.when(pl.program_id(0) < na)
    def _():
        body(xa_ref, cata_ref, ya_ref)

    @pl.when(pl.program_id(0) >= na)
    def _():
        body(xb_ref, catb_ref, yb_ref)


def _out_ffn(layer, xa, cata, xb, catb, W, next_f32=()):
    na = xa.shape[0] // ROW_BLOCK
    nb = xb.shape[0] // ROW_BLOCK
    a_map = lambda r: (jnp.minimum(r, na - 1), 0)
    b_map = lambda r: (jnp.maximum(r - na, 0), 0)
    n_cast = CAST_STEPS if next_f32 else 0
    assert n_cast <= na + nb
    chunk = lambda w: (1, w.shape[1] // CAST_STEPS, w.shape[2])
    src_specs = [pl.BlockSpec(chunk(w), lambda r: (layer + 1, jnp.minimum(r, CAST_STEPS - 1), 0))
                 for w in next_f32]
    dst_specs = [pl.BlockSpec(chunk(w), lambda r: (0, jnp.minimum(r, CAST_STEPS - 1), 0))
                 for w in next_f32]
    dst_shapes = [jax.ShapeDtypeStruct((1,) + w.shape[1:], BF16) for w in next_f32]
    outs = pl.pallas_call(
        functools.partial(_out_ffn_kernel, na=na, n_cast=n_cast),
        grid=(na + nb,),
        in_specs=[
            pl.BlockSpec((ROW_BLOCK, D_MODEL), a_map),
            pl.BlockSpec((ROW_BLOCK, MIX_WIDTH), a_map),
            pl.BlockSpec((ROW_BLOCK, D_MODEL), b_map),
            pl.BlockSpec((ROW_BLOCK, MIX_WIDTH), b_map),
            _layer_spec(0, (MIX_WIDTH, D_MODEL)),
            _layer_spec(layer, (1, D_MODEL)),
            _layer_spec(0, (D_MODEL, 2 * D_FF)),
            _layer_spec(0, (D_FF, D_MODEL)),
        ] + src_specs,
        out_specs=[pl.BlockSpec((ROW_BLOCK, D_MODEL), a_map),
                   pl.BlockSpec((ROW_BLOCK, D_MODEL), b_map)] + dst_specs,
        out_shape=[jax.ShapeDtypeStruct(xa.shape, F32), jax.ShapeDtypeStruct(xb.shape, F32)]
        + dst_shapes,
        compiler_params=_params(1),
        name="out_ffn",
    )(xa, cata, xb, catb, W["w_out"][layer], W["g_ffn"], W["w_gu"][layer], W["w_down"][layer],
      *next_f32)
    return outs[0], outs[1], outs[2:]


def _mixer(i, x, st, W):
    B, L, _ = x.shape
    if i < N_A:
        cat, state = _mixer_pool(i, x, st["hist"], st["mem_k"], st["mem_v"], W, pos0=st["pos0"],
                                 nb=st["nb"], tl=min(L, POOL_BLOCK))
        st["pool_states"].append(state[:, :, 1:, :])
        return cat
    past = st["past"]
    if i == N_A:
        kt_new, vt_new, lft_new, kt_b, v_b = _kv_proj(x, W, nb=st["nb"], tl=st["tl"])
        lft_all = lft_new if past is None else jnp.concatenate([past[2], lft_new], axis=2)
        lk = lft_all.shape[2]
        lk_pad = -(-lk // LANES) * LANES
        lf_t = jnp.pad(lft_all.reshape(B * FOX_HEADS, lk), ((0, 0), (0, lk_pad - lk)))
        f_t = _cumsum_lanes(lf_t).reshape(B, FOX_HEADS, lk_pad)
        st.update(kt_new=kt_new, vt_new=vt_new, lft_new=lft_new, kt_b=kt_b, v_b=v_b, f_t=f_t,
                  f_q=jnp.swapaxes(f_t[:, :, lk - L:lk], 1, 2))
    if past is None:
        return _mixer_fox_prompt(i, x, st["kt_b"], st["v_b"], st["f_t"], st["f_q"], st["mem_k"],
                                 st["mem_v"], W, tl=FOX_BLOCK)
    return _mixer_fox_sample(i, x, past[0], past[1], st["kt_new"], st["vt_new"], st["f_t"],
                             st["f_q"], st["mem_k"], st["mem_v"], W, nb_dense=st["nb"], nb_attn=4)


def _stream_outputs(st):
    return (jnp.concatenate(st["pool_states"], axis=0), jnp.transpose(st["kt_new"], (0, 3, 1, 2)),
            jnp.transpose(st["vt_new"], (0, 3, 1, 2)), jnp.swapaxes(st["lft_new"], 1, 2))


def kernel(x_prompt, x_sample, state_pool, cache_fox_k, cache_fox_v, cache_fox_logf, cache_mem_k,
           cache_mem_v, mem_prompt, g_mix, w_in, w_out, q_norm_mem, g_mem, w_mem_kv, k_norm_mem,
           w_pool, pool_scale, q_norm_fox, g_kv, w_kv, k_norm_fox, b_f, g_ffn, w_gu, w_down):
    B, L, _ = x_prompt.shape
    SB, SL, _ = x_sample.shape
    big = (w_in, w_out, w_gu, w_down)
    W = dict(
        g_mix=g_mix.reshape(DEPTH, 1, D_MODEL),
        q_norm_mem=q_norm_mem.reshape(DEPTH, 1, MEM_HEAD_DIM), w_pool=w_pool.astype(BF16),
        pool_scale=pool_scale.reshape(N_A, 1, POOL_WIDTH),
        qnf_pair=jnp.tile(q_norm_fox, (1, 2)).reshape(DEPTH - N_A, 1, LANES),
        g_kv=g_kv.reshape(1, D_MODEL),
        w_kv_t=jnp.pad(w_kv.T, ((0, KV_ROWS - w_kv.shape[1]), (0, 0))).astype(BF16),
        w_v=w_kv[:, FOX_WIDTH:2 * FOX_WIDTH].astype(BF16),
        kn_col=k_norm_fox.reshape(1, FOX_HEAD_DIM, 1), b_f=b_f.reshape(FOX_HEADS, 1),
        g_ffn=g_ffn.reshape(DEPTH, 1, D_MODEL))

    mem_k_p, mem_v_p, first = _mem_kv(
        mem_prompt, g_mem.reshape(DEPTH, 1, D_MODEL), w_mem_kv,
        k_norm_mem.reshape(DEPTH, 1, MEM_HEAD_DIM), big)
    for name, w0 in zip(("w_in", "w_out", "w_gu", "w_down"), first):
        W[name] = [w0]
    prompt = dict(pos0=0, hist=jnp.zeros((N_A, B, HIST_ROWS, POOL_WIDTH), F32), past=None,
                  mem_k=mem_k_p, mem_v=mem_v_p, nb=1, tl=KV_BLOCK, pool_states=[])
    sample = dict(
        pos0=PAST_LEN, hist=jnp.pad(state_pool, ((0, 0), (0, 0), (1, 0), (0, 0))),
        past=(jnp.transpose(cache_fox_k, (0, 2, 3, 1)), jnp.transpose(cache_fox_v, (0, 2, 3, 1)),
              jnp.swapaxes(cache_fox_logf, 1, 2)),
        mem_k=cache_mem_k.reshape(DEPTH, SB, MEM_ROWS, MEM_HEAD_DIM),
        mem_v=cache_mem_v.reshape(DEPTH, SB, MEM_ROWS, MEM_HEAD_DIM),
        nb=ROW_BLOCK // SL, tl=SL, pool_states=[])

    y_p, y_s = x_prompt, x_sample
    for i in range(DEPTH):
        cat_p = _mixer(i, y_p, prompt, W)
        cat_s = _mixer(i, y_s, sample, W)
        y_p, y_s, nxt = _out_ffn(i, y_p.reshape(B * L, D_MODEL), cat_p.reshape(B * L, MIX_WIDTH),
                                 y_s.reshape(SB * SL, D_MODEL), cat_s.reshape(SB * SL, MIX_WIDTH),
                                 W, next_f32=big if i + 1 < DEPTH else ())
        for name, w_next in zip(("w_in", "w_out", "w_gu", "w_down"), nxt):
            W[name].append(w_next)
        y_p = y_p.reshape(B, L, D_MODEL)
        y_s = y_s.reshape(SB, SL, D_MODEL)

    pool_p, fox_k_p, fox_v_p, fox_lf_p = _stream_outputs(prompt)
    pool_s, fox_k_s, fox_v_s, fox_lf_s = _stream_outputs(sample)
    mem_shape = (DEPTH, B, MEM_TOKENS, MEM_HEADS, MEM_HEAD_DIM)
    return (y_p, y_s, pool_p, fox_k_p, fox_v_p, fox_lf_p, mem_k_p.reshape(mem_shape),
            mem_v_p.reshape(mem_shape), pool_s, fox_k_s, fox_v_s, fox_lf_s)
```

```python
import functools

import jax
import jax.numpy as jnp
from jax import lax
from jax.experimental import pallas as pl
from jax.experimental.pallas import tpu as pltpu

F32 = jnp.float32
BF16 = jnp.bfloat16

D_MODEL = 1024
DEPTH = 4
N_A = DEPTH // 2
PAST_LEN = 1024
POOL_WINDOWS = (2, 4, 8, 16)
POOL_GROUPS = len(POOL_WINDOWS)
POOL_WIDTH = D_MODEL // 2
POOL_GROUP_DIM = POOL_WIDTH // POOL_GROUPS
POOL_HIST = max(POOL_WINDOWS) - 1
HIST_ROWS = POOL_HIST + 1
FOX_HEAD_DIM = 64
FOX_WIDTH = D_MODEL // 2
FOX_HEADS = FOX_WIDTH // FOX_HEAD_DIM
MEM_TOKENS = 256
MEM_HEADS = 4
MEM_WIDTH = D_MODEL // 2
MEM_HEAD_DIM = MEM_WIDTH // MEM_HEADS
MIX_WIDTH = POOL_WIDTH + MEM_WIDTH
D_FF = ((8 * D_MODEL // 3 + 255) // 256) * 256
EPS = 1e-6
FOX_SCALE = FOX_HEAD_DIM ** -0.5
MEM_SCALE = MEM_HEAD_DIM ** -0.5
LOG2E = 1.4426950408889634
AUG_ROWS = 16

LANES = 128
ROW_BLOCK = 512
FOX_BLOCK = 512
POOL_BLOCK = 1024
KV_BLOCK = 1024
CAST_STEPS = 16
SAMPLE_HEAD_GROUP = 4
KV_ROWS = 2 * FOX_WIDTH + 16
MXU_DIM = 256
FFN_SPLITS = ((0, 6 * MXU_DIM), (6 * MXU_DIM, D_FF))
VMEM_LIMIT = 56 * 1024 * 1024


def _dot(a, b):
    return jnp.dot(a, b, preferred_element_type=F32)


def _dot_nt(a, b):
    return lax.dot_general(a, b, (((1,), (1,)), ((), ())), preferred_element_type=F32)


def _rms(x, g):
    ms = jnp.mean(x * x, axis=-1, keepdims=True)
    return (x * lax.rsqrt(ms + EPS)) * g


def _rms_head64(x, g_pair):
    lo = lax.broadcasted_iota(jnp.int32, (1, LANES), 1) < FOX_HEAD_DIM
    outs = []
    for c in range(x.shape[-1] // LANES):
        xc = x[:, c * LANES:(c + 1) * LANES]
        sq = xc * xc
        s_lo = jnp.sum(jnp.where(lo, sq, 0.0), axis=-1, keepdims=True)
        s_hi = jnp.sum(jnp.where(lo, 0.0, sq), axis=-1, keepdims=True)
        ms = jnp.where(lo, s_lo, s_hi) * (1.0 / FOX_HEAD_DIM)
        outs.append((xc * lax.rsqrt(ms + EPS)) * g_pair)
    return outs


def _const_spec(shape):
    return pl.BlockSpec(shape, lambda *_: (0,) * len(shape), pipeline_mode=pl.Buffered(1))


def _layer_spec(layer, shape):
    return pl.BlockSpec((1,) + shape, lambda *_: (layer,) + (0,) * len(shape),
                        pipeline_mode=pl.Buffered(1))


MEM_ROWS = MEM_TOKENS * MEM_HEADS


def _mem_spec(layer, nb):
    return pl.BlockSpec((1, nb, MEM_ROWS, MEM_HEAD_DIM), lambda b, *_: (layer, b, 0, 0))


def _head_rows(h):
    return pl.ds(h, MEM_TOKENS, stride=MEM_HEADS)


def _params(n_grid, flags=None):
    return pltpu.CompilerParams(
        dimension_semantics=("arbitrary",) * n_grid, vmem_limit_bytes=VMEM_LIMIT, flags=flags)


def _mem_kv_kernel(mem_ref, g_ref, w_ref, kn_ref, *rest):
    n = (len(rest) - 2) // 2
    src, (k_ref, v_ref), dst = rest[:n], rest[n:n + 2], rest[n + 2:]
    for s_ref, d_ref in zip(src, dst):
        d_ref[...] = s_ref[...].astype(BF16)
    nb = mem_ref.shape[0]
    x = mem_ref[...].reshape(nb * MEM_TOKENS, D_MODEL)
    kv = _dot(_rms(x, g_ref[0]).astype(BF16), w_ref[0].astype(BF16))
    for h in range(MEM_HEADS):
        ks = slice(h * MEM_HEAD_DIM, (h + 1) * MEM_HEAD_DIM)
        vs = slice(MEM_WIDTH + h * MEM_HEAD_DIM, MEM_WIDTH + (h + 1) * MEM_HEAD_DIM)
        k_ref[0, :, _head_rows(h), :] = _rms(kv[:, ks], kn_ref[0]).reshape(nb, MEM_TOKENS, MEM_HEAD_DIM)
        v_ref[0, :, _head_rows(h), :] = kv[:, vs].reshape(nb, MEM_TOKENS, MEM_HEAD_DIM)


def _mem_kv(mem, g_mem, w_mem_kv, k_norm_mem, first_f32):
    B = mem.shape[0]
    nb = 4
    steps = DEPTH * (B // nb)
    out = jax.ShapeDtypeStruct((DEPTH, B, MEM_ROWS, MEM_HEAD_DIM), F32)
    out_spec = pl.BlockSpec((1, nb, MEM_ROWS, MEM_HEAD_DIM), lambda i, b: (i, b, 0, 0))
    chunk = lambda w: (1, w.shape[1] // steps, w.shape[2])
    chunk_spec = lambda w: pl.BlockSpec(chunk(w), lambda i, b: (0, i * (B // nb) + b, 0))
    outs = pl.pallas_call(
        _mem_kv_kernel,
        grid=(DEPTH, B // nb),
        in_specs=[
            pl.BlockSpec((nb, MEM_TOKENS, D_MODEL), lambda i, b: (b, 0, 0)),
            pl.BlockSpec((1, 1, D_MODEL), lambda i, b: (i, 0, 0)),
            pl.BlockSpec((1, D_MODEL, 2 * MEM_WIDTH), lambda i, b: (i, 0, 0)),
            pl.BlockSpec((1, 1, MEM_HEAD_DIM), lambda i, b: (i, 0, 0)),
        ] + [chunk_spec(w) for w in first_f32],
        out_specs=[out_spec, out_spec] + [chunk_spec(w) for w in first_f32],
        out_shape=[out, out] + [jax.ShapeDtypeStruct((1,) + w.shape[1:], BF16) for w in first_f32],
        compiler_params=_params(2),
        name="mem_kv",
    )(mem, g_mem, w_mem_kv, k_norm_mem, *first_f32)
    return outs[0], outs[1], outs[2:]


def _in_proj(x_ref, g_ref, w_ref):
    nb, tl, _ = x_ref.shape
    x = x_ref[...].reshape(nb * tl, D_MODEL)
    return _dot(_rms(x, g_ref[0]).astype(BF16), w_ref[0])


def _mem_attend(zq, qn, mk_ref, mv_ref, i, cat_ref, col0):
    for h in range(MEM_HEADS):
        sl = slice(h * MEM_HEAD_DIM, (h + 1) * MEM_HEAD_DIM)
        q = _rms(zq[:, sl], qn).astype(BF16)
        s = _dot_nt(q, mk_ref[0, i, _head_rows(h), :].astype(BF16)) * (MEM_SCALE * LOG2E)
        p = jnp.exp2(s - jnp.max(s, axis=-1, keepdims=True)).astype(BF16)
        v = mv_ref[0, i, _head_rows(h), :].astype(BF16)
        o = _dot(p, jnp.concatenate([v, jnp.ones_like(v)], axis=-1))
        o = o[:, :MEM_HEAD_DIM] / o[:, MEM_HEAD_DIM:]
        cat_ref[i, :, col0 + h * MEM_HEAD_DIM:col0 + (h + 1) * MEM_HEAD_DIM] = o.astype(cat_ref.dtype)


def _mem_attend_paired(zq, qn, mk_ref, mv_ref, i, cat_ref, col0):
    z = jnp.zeros((MEM_TOKENS, MEM_HEAD_DIM), BF16)
    one = jnp.ones((MEM_TOKENS, MEM_HEAD_DIM), BF16)
    for c in range(MEM_HEADS // 2):
        hs = (2 * c, 2 * c + 1)
        q = jnp.concatenate(
            [_rms(zq[:, h * MEM_HEAD_DIM:(h + 1) * MEM_HEAD_DIM], qn) for h in hs], axis=1)
        k0, k1 = (mk_ref[0, i, _head_rows(h), :].astype(BF16) for h in hs)
        v0, v1 = (mv_ref[0, i, _head_rows(h), :].astype(BF16) for h in hs)
        k_pair = jnp.concatenate([jnp.concatenate([k0, z], axis=1),
                                  jnp.concatenate([z, k1], axis=1)], axis=0)
        v_pair = jnp.concatenate([jnp.concatenate([v0, one, z, z], axis=1),
                                  jnp.concatenate([z, z, v1, one], axis=1)], axis=0)
        s = _dot_nt(q.astype(BF16), k_pair) * (MEM_SCALE * LOG2E)
        p = jnp.concatenate(
            [jnp.exp2(sh - jnp.max(sh, axis=-1, keepdims=True))
             for sh in (s[:, :MEM_TOKENS], s[:, MEM_TOKENS:])], axis=1).astype(BF16)
        o = _dot(p, v_pair)
        for n, h in enumerate(hs):
            oh = o[:, 2 * n * MEM_HEAD_DIM:(2 * n + 1) * MEM_HEAD_DIM]
            lh = o[:, (2 * n + 1) * MEM_HEAD_DIM:(2 * n + 2) * MEM_HEAD_DIM]
            cat_ref[i, :, col0 + h * MEM_HEAD_DIM:col0 + (h + 1) * MEM_HEAD_DIM] = (
                (oh / lh).astype(cat_ref.dtype))


def _mixer_pool_kernel(x_ref, g_ref, w_in_ref, qn_ref, mk_ref, mv_ref, hist_ref, wp_ref, ps_ref,
                       cat_ref, state_ref, ubuf, *, pos0):
    nb, tl, _ = x_ref.shape
    j = pl.program_id(1)
    z = _in_proj(x_ref, g_ref, w_in_ref)

    @pl.when(j == 0)
    def _():
        ubuf[:, 0:HIST_ROWS, :] = hist_ref[0]

    pos = pos0 + j * tl + lax.broadcasted_iota(jnp.int32, (tl, 1), 0)
    for i in range(nb):
        zi = z[i * tl:(i + 1) * tl]
        u = zi[:, :POOL_WIDTH]
        ubuf[i, HIST_ROWS:HIST_ROWS + tl, :] = u
        for g, w in enumerate(POOL_WINDOWS):
            sl = slice(g * POOL_GROUP_DIM, (g + 1) * POOL_GROUP_DIM)
            ug = u[:, sl]
            acc = ug
            for k in range(1, w):
                acc = acc + ubuf[i, HIST_ROWS - k:HIST_ROWS - k + tl, sl]
            cnt = jnp.minimum(pos + 1, w).astype(F32)
            d = acc / cnt - ug
            y = _dot(d.astype(BF16), wp_ref[0, g]) * ps_ref[0, :, sl]
            cat_ref[i, :, sl] = y.astype(cat_ref.dtype)
        _mem_attend(zi[:, POOL_WIDTH:], qn_ref[0], mk_ref, mv_ref, i, cat_ref, POOL_WIDTH)
        tail = ubuf[i, tl:tl + HIST_ROWS, :]
        state_ref[0, i] = tail
        ubuf[i, 0:HIST_ROWS, :] = tail


def _mixer_pool(layer, x, hist, mem_k, mem_v, W, *, pos0, nb, tl):
    B, L, _ = x.shape
    assert tl >= HIST_ROWS and L % tl == 0 and B % nb == 0
    hist_spec = pl.BlockSpec((1, nb, HIST_ROWS, POOL_WIDTH), lambda b, j: (layer, b, 0, 0))
    return pl.pallas_call(
        functools.partial(_mixer_pool_kernel, pos0=pos0),
        grid=(B // nb, L // tl),
        in_specs=[
            pl.BlockSpec((nb, tl, D_MODEL), lambda b, j: (b, j, 0)),
            _layer_spec(layer, (1, D_MODEL)),
            _layer_spec(0, (D_MODEL, MIX_WIDTH)),
            _layer_spec(layer, (1, MEM_HEAD_DIM)),
            _mem_spec(layer, nb),
            _mem_spec(layer, nb),
            hist_spec,
            _layer_spec(layer, (POOL_GROUPS, POOL_GROUP_DIM, POOL_GROUP_DIM)),
            _layer_spec(layer, (1, POOL_WIDTH)),
        ],
        out_specs=[
            pl.BlockSpec((nb, tl, MIX_WIDTH), lambda b, j: (b, j, 0)),
            pl.BlockSpec((1, nb, HIST_ROWS, POOL_WIDTH), lambda b, j: (0, b, 0, 0)),
        ],
        out_shape=[
            jax.ShapeDtypeStruct((B, L, MIX_WIDTH), BF16),
            jax.ShapeDtypeStruct((1, B, HIST_ROWS, POOL_WIDTH), F32),
        ],
        scratch_shapes=[pltpu.VMEM((nb, HIST_ROWS + tl, POOL_WIDTH), F32)],
        compiler_params=_params(2),
        name="mixer_pool",
    )(x, W["g_mix"], W["w_in"][layer], W["q_norm_mem"], mem_k, mem_v, hist, W["w_pool"], W["pool_scale"])


def _kv_proj_kernel(x_ref, g_ref, w_ref, wv_ref, kn_ref, bf_ref, k_ref, v_ref, lf_ref, kb_ref, vb_ref):
    nb, tl, _ = x_ref.shape
    rows = nb * tl
    x = x_ref[...].reshape(rows, D_MODEL)
    xn = _rms(x, g_ref[...]).astype(BF16)
    zt = _dot_nt(w_ref[...], xn)
    zv = _dot(xn, wv_ref[...])
    lo = lax.broadcasted_iota(jnp.int32, (1, LANES), 1) < FOX_HEAD_DIM
    for h in range(FOX_HEADS):
        pair = zv[:, (h // 2) * LANES:(h // 2 + 1) * LANES]
        base = pair if h % 2 == 0 else pltpu.roll(pair, FOX_HEAD_DIM, 1)
        vb_ref[:, h] = jnp.where(lo, base, 1.0).astype(BF16).reshape(nb, tl, LANES)
    k3 = zt[:FOX_WIDTH].reshape(FOX_HEADS, FOX_HEAD_DIM, rows)
    ms = jnp.mean(k3 * k3, axis=1, keepdims=True)
    k3 = (k3 * lax.rsqrt(ms + EPS)) * kn_ref[...]
    v3 = zt[FOX_WIDTH:2 * FOX_WIDTH].reshape(FOX_HEADS, FOX_HEAD_DIM, rows)
    t = -(zt[2 * FOX_WIDTH:2 * FOX_WIDTH + FOX_HEADS] + bf_ref[...])
    lf = -(jnp.maximum(t, 0.0) + jnp.log1p(jnp.exp(-jnp.abs(t))))
    for i in range(nb):
        cols = slice(i * tl, (i + 1) * tl)
        k_ref[i] = k3[:, :, cols]
        v_ref[i] = v3[:, :, cols]
        lf_ref[i] = lf[:, cols]
        kb_ref[i] = k3[:, :, cols].astype(BF16)


def _kv_proj(x, W, *, nb, tl):
    B, L, _ = x.shape
    hd = pl.BlockSpec((nb, FOX_HEADS, FOX_HEAD_DIM, tl), lambda b, j: (b, 0, 0, j))
    heads = jax.ShapeDtypeStruct((B, FOX_HEADS, FOX_HEAD_DIM, L), F32)
    return pl.pallas_call(
        _kv_proj_kernel,
        grid=(B // nb, L // tl),
        in_specs=[
            pl.BlockSpec((nb, tl, D_MODEL), lambda b, j: (b, j, 0)),
            _const_spec((1, D_MODEL)),
            _const_spec((KV_ROWS, D_MODEL)),
            _const_spec((D_MODEL, FOX_WIDTH)),
            _const_spec((1, FOX_HEAD_DIM, 1)),
            _const_spec((FOX_HEADS, 1)),
        ],
        out_specs=[hd, hd, pl.BlockSpec((nb, FOX_HEADS, tl), lambda b, j: (b, 0, j)), hd,
                   pl.BlockSpec((nb, FOX_HEADS, tl, LANES), lambda b, j: (b, 0, j, 0))],
        out_shape=[heads, heads, jax.ShapeDtypeStruct((B, FOX_HEADS, L), F32),
                   jax.ShapeDtypeStruct((B, FOX_HEADS, FOX_HEAD_DIM, L), BF16),
                   jax.ShapeDtypeStruct((B, FOX_HEADS, L, LANES), BF16)],
        compiler_params=_params(2),
        name="kv_proj",
    )(x, W["g_kv"], W["w_kv_t"], W["w_v"], W["kn_col"], W["b_f"])


def _cumsum_kernel(lf_ref, f_ref):
    rows, n = lf_ref.shape
    r = lax.broadcasted_iota(jnp.int32, (LANES, LANES), 0)
    c = lax.broadcasted_iota(jnp.int32, (LANES, LANES), 1)
    tri = jnp.where(r <= c, 1.0, 0.0).astype(BF16)
    carry = jnp.zeros((rows, 1), F32)
    for ch in range(n // LANES):
        x = lf_ref[:, ch * LANES:(ch + 1) * LANES]
        hi = x.astype(BF16)
        r1 = x - hi.astype(F32)
        mid = r1.astype(BF16)
        low = (r1 - mid.astype(F32)).astype(BF16)
        y = (_dot(hi, tri) + _dot(mid, tri)) + _dot(low, tri) + carry
        f_ref[:, ch * LANES:(ch + 1) * LANES] = y
        carry = y[:, LANES - 1:LANES]


def _cumsum_lanes(lf_t):
    return pl.pallas_call(
        _cumsum_kernel,
        out_shape=jax.ShapeDtypeStruct(lf_t.shape, F32),
        name="logf_cumsum",
    )(lf_t)


def _tri_unrank(t, n):
    row = sum((t >= k * (k + 1) // 2).astype(jnp.int32) for k in range(1, n))
    return row, t - row * (row + 1) // 2


def _split3(x):
    hi = x.astype(BF16).astype(F32)
    r1 = x - hi
    mid = r1.astype(BF16).astype(F32)
    low = (r1 - mid).astype(BF16).astype(F32)
    return hi, mid, low


def _mixer_fox_prompt_kernel(x_ref, g_ref, w_in_ref, qn_ref, mk_ref, mv_ref, qnf_ref,
                             kt_ref, v_ref, ft_ref, fq_ref, cat_ref, q_sc, m_sc, acc_sc, mask_sc,
                             *, nkb):
    _, tl, _ = x_ref.shape
    tk = kt_ref.shape[-1]
    j, kb = _tri_unrank(pl.program_id(1), nkb)
    lane = lax.broadcasted_iota(jnp.int32, (1, LANES), 1)
    row = lax.broadcasted_iota(jnp.int32, (AUG_ROWS, 1), 0)

    @pl.when(kb == 0)
    def _():
        z = _in_proj(x_ref, g_ref, w_in_ref)
        _mem_attend_paired(z[:, FOX_WIDTH:], qn_ref[0], mk_ref, mv_ref, 0, cat_ref, FOX_WIDTH)
        qs = _rms_head64(z[:, :FOX_WIDTH], qnf_ref[0])
        for h in range(FOX_HEADS):
            base = qs[h // 2] if h % 2 == 0 else pltpu.roll(qs[h // 2], FOX_HEAD_DIM, 1)
            hi, mid, low = _split3(fq_ref[0, :, h:h + 1] * LOG2E)
            tail = jnp.where(lane < FOX_HEAD_DIM + 3, 1.0,
                             jnp.where(lane == FOX_HEAD_DIM + 3, hi,
                                       jnp.where(lane == FOX_HEAD_DIM + 4, mid,
                                                 jnp.where(lane == FOX_HEAD_DIM + 5, low, 0.0))))
            q_sc[h] = jnp.where(lane < FOX_HEAD_DIM, base * (FOX_SCALE * LOG2E), tail).astype(BF16)
        m_sc[...] = jnp.full(m_sc.shape, -jnp.inf, F32)
        acc_sc[...] = jnp.zeros(acc_sc.shape, F32)
        causal = (lax.broadcasted_iota(jnp.int32, (tl, 1), 0)
                  >= lax.broadcasted_iota(jnp.int32, (1, tk), 1))
        mask_sc[...] = jnp.where(causal, 0.0, -jnp.inf)

    def k_aug(h):
        hi, mid, low = _split3(ft_ref[0, h:h + 1, :] * (-LOG2E))
        aug = jnp.where(row == 0, hi, jnp.where(row == 1, mid, jnp.where(row == 2, low,
                        jnp.where(row < 6, 1.0, 0.0)))).astype(BF16)
        pad = jnp.zeros((FOX_HEAD_DIM - AUG_ROWS, tk), BF16)
        return jnp.concatenate([kt_ref[0, h], aug, pad], axis=0)

    def attend(masked):
        zk = jnp.zeros((LANES, tk), BF16)
        zv = jnp.zeros((tk, LANES), BF16)
        for c in range(FOX_HEADS // 2):
            h0, h1 = 2 * c, 2 * c + 1
            k_pair = jnp.concatenate([jnp.concatenate([k_aug(h0), zk], axis=1),
                                      jnp.concatenate([zk, k_aug(h1)], axis=1)], axis=0)
            v_pair = jnp.concatenate([jnp.concatenate([v_ref[0, h0], zv], axis=1),
                                      jnp.concatenate([zv, v_ref[0, h1]], axis=1)], axis=0)
            q_pair = jnp.concatenate([q_sc[h0], q_sc[h1]], axis=1)
            s = _dot(q_pair, k_pair)
            ps, alphas = [], []
            for h, sh in ((h0, s[:, :tk]), (h1, s[:, tk:])):
                if masked:
                    sh = sh + mask_sc[...]
                m_old = m_sc[h]
                m_new = jnp.maximum(m_old, jnp.max(sh, axis=-1, keepdims=True))
                ps.append(jnp.exp2(sh - m_new[:, 0:1]).astype(BF16))
                alphas.append(jnp.exp2(m_old - m_new))
                m_sc[h] = m_new
            o = _dot(jnp.concatenate(ps, axis=1), v_pair)
            acc_sc[h0] = alphas[0] * acc_sc[h0] + o[:, :LANES]
            acc_sc[h1] = alphas[1] * acc_sc[h1] + o[:, LANES:]

    @pl.when(kb < j)
    def _():
        attend(False)

    @pl.when(kb == j)
    def _():
        attend(True)
        for c in range(FOX_HEADS // 2):
            a0 = acc_sc[2 * c]
            a1 = acc_sc[2 * c + 1]
            o0 = a0 / pltpu.roll(a0, FOX_HEAD_DIM, 1)
            o1 = pltpu.roll(a1, FOX_HEAD_DIM, 1) / a1
            cat_ref[0, :, c * LANES:(c + 1) * LANES] = (
                jnp.where(lane < FOX_HEAD_DIM, o0, o1).astype(cat_ref.dtype))


def _mixer_fox_prompt(layer, x, kt_b, vt_b, f_t, f_q, mem_k, mem_v, W, *, tl):
    B, L, _ = x.shape
    nkb = L // tl
    jj = lambda t: _tri_unrank(t, nkb)[0]
    kk = lambda t: _tri_unrank(t, nkb)[1]
    kt_spec = pl.BlockSpec((1, FOX_HEADS, FOX_HEAD_DIM, tl), lambda b, t: (b, 0, 0, kk(t)))
    v_spec = pl.BlockSpec((1, FOX_HEADS, tl, LANES), lambda b, t: (b, 0, kk(t), 0))
    return pl.pallas_call(
        functools.partial(_mixer_fox_prompt_kernel, nkb=nkb),
        grid=(B, nkb * (nkb + 1) // 2),
        in_specs=[
            pl.BlockSpec((1, tl, D_MODEL), lambda b, t: (b, jj(t), 0)),
            _layer_spec(layer, (1, D_MODEL)),
            _layer_spec(0, (D_MODEL, MIX_WIDTH)),
            _layer_spec(layer, (1, MEM_HEAD_DIM)),
            _mem_spec(layer, 1),
            _mem_spec(layer, 1),
            _layer_spec(layer - N_A, (1, LANES)),
            kt_spec,
            v_spec,
            pl.BlockSpec((1, FOX_HEADS, tl), lambda b, t: (b, 0, kk(t))),
            pl.BlockSpec((1, tl, FOX_HEADS), lambda b, t: (b, jj(t), 0)),
        ],
        out_specs=pl.BlockSpec((1, tl, MIX_WIDTH), lambda b, t: (b, jj(t), 0)),
        out_shape=jax.ShapeDtypeStruct((B, L, MIX_WIDTH), BF16),
        scratch_shapes=[
            pltpu.VMEM((FOX_HEADS, tl, LANES), BF16),
            pltpu.VMEM((FOX_HEADS, tl, LANES), F32),
            pltpu.VMEM((FOX_HEADS, tl, LANES), F32),
            pltpu.VMEM((tl, tl), F32),
        ],
        compiler_params=_params(2),
        name="mixer_fox_prompt",
    )(x, W["g_mix"], W["w_in"][layer], W["q_norm_mem"], mem_k, mem_v, W["qnf_pair"], kt_b, vt_b, f_t, f_q)


def _block_diag(blocks):
    z = jnp.zeros_like(blocks[0])
    n = len(blocks)
    return jnp.concatenate(
        [jnp.concatenate([b if j == i else z for j in range(n)], axis=1)
         for i, b in enumerate(blocks)], axis=0)


def _fox_sample_pre_kernel(x_ref, g_ref, w_in_ref, qn_ref, mk_ref, mv_ref, qnf_ref, q_ref, cm_ref):
    nb, tl, _ = x_ref.shape
    z = _in_proj(x_ref, g_ref, w_in_ref)
    for i in range(nb):
        zi = z[i * tl:(i + 1) * tl]
        qs = _rms_head64(zi[:, :FOX_WIDTH], qnf_ref[0])
        for c in range(FOX_HEADS // 2):
            q_ref[i, :, c * LANES:(c + 1) * LANES] = (qs[c] * FOX_SCALE).astype(BF16)
        _mem_attend(zi[:, FOX_WIDTH:], qn_ref[0], mk_ref, mv_ref, i, cm_ref, 0)


def _fox_sample_attn_kernel(q_ref, ktp_ref, vtp_ref, ktn_ref, vtn_ref, ft_ref, fq_ref, cat_ref):
    nb, tl, _ = q_ref.shape
    past = ktp_ref.shape[-1]
    causal = (lax.broadcasted_iota(jnp.int32, (tl, 1), 0)
              >= lax.broadcasted_iota(jnp.int32, (1, tl), 1))
    lo = lax.broadcasted_iota(jnp.int32, (1, LANES), 1) < FOX_HEAD_DIM
    gw = SAMPLE_HEAD_GROUP * FOX_HEAD_DIM
    for i in range(nb):
        for g in range(FOX_HEADS // SAMPLE_HEAD_GROUP):
            hs = tuple(range(g * SAMPLE_HEAD_GROUP, (g + 1) * SAMPLE_HEAD_GROUP))
            q = q_ref[i, :, g * gw:(g + 1) * gw]
            s_p = _dot(q, _block_diag([ktp_ref[i, h].astype(BF16) for h in hs]))
            s_n = _dot(q, _block_diag([ktn_ref[i, h].astype(BF16) for h in hs]))
            pp, pn, inv_l = [], [], []
            for n, h in enumerate(hs):
                fq = fq_ref[i, :, h:h + 1]
                sp = (s_p[:, n * past:(n + 1) * past] + fq) - ft_ref[i, h:h + 1, 0:past]
                sn = (s_n[:, n * tl:(n + 1) * tl] + fq) - ft_ref[i, h:h + 1, past:past + tl]
                sn = jnp.where(causal, sn, -jnp.inf)
                m = jnp.maximum(jnp.max(sp, axis=-1, keepdims=True),
                                jnp.max(sn, axis=-1, keepdims=True))
                ep = jnp.exp(sp - m)
                en = jnp.exp(sn - m)
                inv_l.append(1.0 / (jnp.sum(ep, axis=-1, keepdims=True)
                                    + jnp.sum(en, axis=-1, keepdims=True)))
                pp.append(ep.astype(BF16))
                pn.append(en.astype(BF16))
            o = (_dot_nt(jnp.concatenate(pp, axis=1),
                         _block_diag([vtp_ref[i, h].astype(BF16) for h in hs]))
                 + _dot_nt(jnp.concatenate(pn, axis=1),
                           _block_diag([vtn_ref[i, h].astype(BF16) for h in hs])))
            scale = jnp.concatenate([jnp.where(lo, inv_l[n], inv_l[n + 1])
                                     for n in range(0, SAMPLE_HEAD_GROUP, 2)], axis=1)
            cat_ref[i, :, g * gw:(g + 1) * gw] = (o * scale).astype(cat_ref.dtype)


def _mixer_fox_sample(layer, x, kt_past, vt_past, kt_new, vt_new, f_t, f_q, mem_k, mem_v, W, *,
                      nb_dense, nb_attn):
    B, L, _ = x.shape
    past = kt_past.shape[-1]
    lk_pad = f_t.shape[-1]
    blk = lambda nb, *tail: pl.BlockSpec((nb,) + tail, lambda b: (b,) + (0,) * len(tail))
    half = jax.ShapeDtypeStruct((B, L, FOX_WIDTH), BF16)
    q, cat_mem = pl.pallas_call(
        _fox_sample_pre_kernel,
        grid=(B // nb_dense,),
        in_specs=[
            blk(nb_dense, L, D_MODEL),
            _layer_spec(layer, (1, D_MODEL)),
            _layer_spec(0, (D_MODEL, MIX_WIDTH)),
            _layer_spec(layer, (1, MEM_HEAD_DIM)),
            _mem_spec(layer, nb_dense),
            _mem_spec(layer, nb_dense),
            _layer_spec(layer - N_A, (1, LANES)),
        ],
        out_specs=[blk(nb_dense, L, FOX_WIDTH), blk(nb_dense, L, MEM_WIDTH)],
        out_shape=[half, half],
        compiler_params=_params(1),
        name="fox_sample_pre",
    )(x, W["g_mix"], W["w_in"][layer], W["q_norm_mem"], mem_k, mem_v, W["qnf_pair"])
    cat_fox = pl.pallas_call(
        _fox_sample_attn_kernel,
        grid=(B // nb_attn,),
        in_specs=[
            blk(nb_attn, L, FOX_WIDTH),
            blk(nb_attn, FOX_HEADS, FOX_HEAD_DIM, past),
            blk(nb_attn, FOX_HEADS, FOX_HEAD_DIM, past),
            blk(nb_attn, FOX_HEADS, FOX_HEAD_DIM, L),
            blk(nb_attn, FOX_HEADS, FOX_HEAD_DIM, L),
            blk(nb_attn, FOX_HEADS, lk_pad),
            blk(nb_attn, L, FOX_HEADS),
        ],
        out_specs=blk(nb_attn, L, FOX_WIDTH),
        out_shape=half,
        compiler_params=_params(1),
        name="fox_sample_attn",
    )(q, kt_past, vt_past, kt_new, vt_new, f_t, f_q)
    return jnp.concatenate([cat_fox, cat_mem], axis=-1)


def _out_ffn_kernel(xa_ref, cata_ref, xb_ref, catb_ref, w_out_ref, g_ref, w_gu_ref, w_down_ref,
                    *rest, na, n_cast):
    if n_cast:
        src, (ya_ref, yb_ref), dst = rest[:4], rest[4:6], rest[6:]

        @pl.when(pl.program_id(0) < n_cast)
        def _():
            for s_ref, d_ref in zip(src, dst):
                d_ref[...] = s_ref[...].astype(BF16)
    else:
        ya_ref, yb_ref = rest

    def body(x_ref, cat_ref, y_ref):
        x1 = x_ref[...] + _dot(cat_ref[...], w_out_ref[0])
        xn = _rms(x1, g_ref[0]).astype(BF16)
        acc = x1
        for lo, hi in FFN_SPLITS:
            gate = _dot(xn, w_gu_ref[0, :, lo:hi])
            up = _dot(xn, w_gu_ref[0, :, D_FF + lo:D_FF + hi])
            h = (gate * (1.0 / (1.0 + jnp.exp(-gate)))) * up
            acc = acc + _dot(h.astype(BF16), w_down_ref[0, lo:hi, :])
        y_ref[...] = acc

    @pl.when(pl.program_id(0) < na)
    def _():
        body(xa_ref, cata_ref, ya_ref)

    @pl.when(pl.program_id(0) >= na)
    def _():
        body(xb_ref, catb_ref, yb_ref)


def _out_ffn(layer, xa, cata, xb, catb, W, next_f32=()):
    na = xa.shape[0] // ROW_BLOCK
    nb = xb.shape[0] // ROW_BLOCK
    a_map = lambda r: (jnp.minimum(r, na - 1), 0)
    b_map = lambda r: (jnp.maximum(r - na, 0), 0)
    n_cast = CAST_STEPS if next_f32 else 0
    assert n_cast <= na + nb
    chunk = lambda w: (1, w.shape[1] // CAST_STEPS, w.shape[2])
    src_specs = [pl.BlockSpec(chunk(w), lambda r: (layer + 1, jnp.minimum(r, CAST_STEPS - 1), 0))
                 for w in next_f32]
    dst_specs = [pl.BlockSpec(chunk(w), lambda r: (0, jnp.minimum(r, CAST_STEPS - 1), 0))
                 for w in next_f32]
    dst_shapes = [jax.ShapeDtypeStruct((1,) + w.shape[1:], BF16) for w in next_f32]
    outs = pl.pallas_call(
        functools.partial(_out_ffn_kernel, na=na, n_cast=n_cast),
        grid=(na + nb,),
        in_specs=[
            pl.BlockSpec((ROW_BLOCK, D_MODEL), a_map),
            pl.BlockSpec((ROW_BLOCK, MIX_WIDTH), a_map),
            pl.BlockSpec((ROW_BLOCK, D_MODEL), b_map),
            pl.BlockSpec((ROW_BLOCK, MIX_WIDTH), b_map),
            _layer_spec(0, (MIX_WIDTH, D_MODEL)),
            _layer_spec(layer, (1, D_MODEL)),
            _layer_spec(0, (D_MODEL, 2 * D_FF)),
            _layer_spec(0, (D_FF, D_MODEL)),
        ] + src_specs,
        out_specs=[pl.BlockSpec((ROW_BLOCK, D_MODEL), a_map),
                   pl.BlockSpec((ROW_BLOCK, D_MODEL), b_map)] + dst_specs,
        out_shape=[jax.ShapeDtypeStruct(xa.shape, F32), jax.ShapeDtypeStruct(xb.shape, F32)]
        + dst_shapes,
        compiler_params=_params(1),
        name="out_ffn",
    )(xa, cata, xb, catb, W["w_out"][layer], W["g_ffn"], W["w_gu"][layer], W["w_down"][layer],
      *next_f32)
    return outs[0], outs[1], outs[2:]


def _mixer(i, x, st, W):
    B, L, _ = x.shape
    if i < N_A:
        cat, state = _mixer_pool(i, x, st["hist"], st["mem_k"], st["mem_v"], W, pos0=st["pos0"],
                                 nb=st["nb"], tl=min(L, POOL_BLOCK))
        st["pool_states"].append(state[:, :, 1:, :])
        return cat
    past = st["past"]
    if i == N_A:
        kt_new, vt_new, lft_new, kt_b, v_b = _kv_proj(x, W, nb=st["nb"], tl=st["tl"])
        lft_all = lft_new if past is None else jnp.concatenate([past[2], lft_new], axis=2)
        lk = lft_all.shape[2]
        lk_pad = -(-lk // LANES) * LANES
        lf_t = jnp.pad(lft_all.reshape(B * FOX_HEADS, lk), ((0, 0), (0, lk_pad - lk)))
        f_t = _cumsum_lanes(lf_t).reshape(B, FOX_HEADS, lk_pad)
        st.update(kt_new=kt_new, vt_new=vt_new, lft_new=lft_new, kt_b=kt_b, v_b=v_b, f_t=f_t,
                  f_q=jnp.swapaxes(f_t[:, :, lk - L:lk], 1, 2))
    if past is None:
        return _mixer_fox_prompt(i, x, st["kt_b"], st["v_b"], st["f_t"], st["f_q"], st["mem_k"],
                                 st["mem_v"], W, tl=FOX_BLOCK)
    return _mixer_fox_sample(i, x, past[0], past[1], st["kt_new"], st["vt_new"], st["f_t"],
                             st["f_q"], st["mem_k"], st["mem_v"], W, nb_dense=st["nb"], nb_attn=4)


def _stream_outputs(st):
    return (jnp.concatenate(st["pool_states"], axis=0), jnp.transpose(st["kt_new"], (0, 3, 1, 2)),
            jnp.transpose(st["vt_new"], (0, 3, 1, 2)), jnp.swapaxes(st["lft_new"], 1, 2))


def kernel(x_prompt, x_sample, state_pool, cache_fox_k, cache_fox_v, cache_fox_logf, cache_mem_k,
           cache_mem_v, mem_prompt, g_mix, w_in, w_out, q_norm_mem, g_mem, w_mem_kv, k_norm_mem,
           w_pool, pool_scale, q_norm_fox, g_kv, w_kv, k_norm_fox, b_f, g_ffn, w_gu, w_down):
    B, L, _ = x_prompt.shape
    SB, SL, _ = x_sample.shape
    big = (w_in, w_out, w_gu, w_down)
    W = dict(
        g_mix=g_mix.reshape(DEPTH, 1, D_MODEL),
        q_norm_mem=q_norm_mem.reshape(DEPTH, 1, MEM_HEAD_DIM), w_pool=w_pool.astype(BF16),
        pool_scale=pool_scale.reshape(N_A, 1, POOL_WIDTH),
        qnf_pair=jnp.tile(q_norm_fox, (1, 2)).reshape(DEPTH - N_A, 1, LANES),
        g_kv=g_kv.reshape(1, D_MODEL),
        w_kv_t=jnp.pad(w_kv.T, ((0, KV_ROWS - w_kv.shape[1]), (0, 0))).astype(BF16),
        w_v=w_kv[:, FOX_WIDTH:2 * FOX_WIDTH].astype(BF16),
        kn_col=k_norm_fox.reshape(1, FOX_HEAD_DIM, 1), b_f=b_f.reshape(FOX_HEADS, 1),
        g_ffn=g_ffn.reshape(DEPTH, 1, D_MODEL))

    mem_k_p, mem_v_p, first = _mem_kv(
        mem_prompt, g_mem.reshape(DEPTH, 1, D_MODEL), w_mem_kv,
        k_norm_mem.reshape(DEPTH, 1, MEM_HEAD_DIM), big)
    for name, w0 in zip(("w_in", "w_out", "w_gu", "w_down"), first):
        W[name] = [w0]
    prompt = dict(pos0=0, hist=jnp.zeros((N_A, B, HIST_ROWS, POOL_WIDTH), F32), past=None,
                  mem_k=mem_k_p, mem_v=mem_v_p, nb=1, tl=KV_BLOCK, pool_states=[])
    sample = dict(
        pos0=PAST_LEN, hist=jnp.pad(state_pool, ((0, 0), (0, 0), (1, 0), (0, 0))),
        past=(jnp.transpose(cache_fox_k, (0, 2, 3, 1)), jnp.transpose(cache_fox_v, (0, 2, 3, 1)),
              jnp.swapaxes(cache_fox_logf, 1, 2)),
        mem_k=cache_mem_k.reshape(DEPTH, SB, MEM_ROWS, MEM_HEAD_DIM),
        mem_v=cache_mem_v.reshape(DEPTH, SB, MEM_ROWS, MEM_HEAD_DIM),
        nb=ROW_BLOCK // SL, tl=SL, pool_states=[])

    y_p, y_s = x_prompt, x_sample
    for i in range(DEPTH):
        cat_p = _mixer(i, y_p, prompt, W)
        cat_s = _mixer(i, y_s, sample, W)
        y_p, y_s, nxt = _out_ffn(i, y_p.reshape(B * L, D_MODEL), cat_p.reshape(B * L, MIX_WIDTH),
                                 y_s.reshape(SB * SL, D_MODEL), cat_s.reshape(SB * SL, MIX_WIDTH),
                                 W, next_f32=big if i + 1 < DEPTH else ())
        for name, w_next in zip(("w_in", "w_out", "w_gu", "w_down"), nxt):
            W[name].append(w_next)
        y_p = y_p.reshape(B, L, D_MODEL)
        y_s = y_s.reshape(SB, SL, D_MODEL)

    pool_p, fox_k_p, fox_v_p, fox_lf_p = _stream_outputs(prompt)
    pool_s, fox_k_s, fox_v_s, fox_lf_s = _stream_outputs(sample)
    mem_shape = (DEPTH, B, MEM_TOKENS, MEM_HEADS, MEM_HEAD_DIM)
    return (y_p, y_s, pool_p, fox_k_p, fox_v_p, fox_lf_p, mem_k_p.reshape(mem_shape),
            mem_v_p.reshape(mem_shape), pool_s, fox_k_s, fox_v_s, fox_lf_s)
```

```python
import functools

import jax
import jax.numpy as jnp
from jax import lax
from jax.experimental import pallas as pl
from jax.experimental.pallas import tpu as pltpu

F32 = jnp.float32
BF16 = jnp.bfloat16

D_MODEL = 1024
DEPTH = 4
N_A = DEPTH // 2
PAST_LEN = 1024
POOL_WINDOWS = (2, 4, 8, 16)
POOL_GROUPS = len(POOL_WINDOWS)
POOL_WIDTH = D_MODEL // 2
POOL_GROUP_DIM = POOL_WIDTH // POOL_GROUPS
POOL_HIST = max(POOL_WINDOWS) - 1
HIST_ROWS = POOL_HIST + 1
FOX_HEAD_DIM = 64
FOX_WIDTH = D_MODEL // 2
FOX_HEADS = FOX_WIDTH // FOX_HEAD_DIM
MEM_TOKENS = 256
MEM_HEADS = 4
MEM_WIDTH = D_MODEL // 2
MEM_HEAD_DIM = MEM_WIDTH // MEM_HEADS
MIX_WIDTH = POOL_WIDTH + MEM_WIDTH
D_FF = ((8 * D_MODEL // 3 + 255) // 256) * 256
EPS = 1e-6
FOX_SCALE = FOX_HEAD_DIM ** -0.5
MEM_SCALE = MEM_HEAD_DIM ** -0.5
LOG2E = 1.4426950408889634
AUG_ROWS = 16

LANES = 128
ROW_BLOCK = 512
FOX_BLOCK = 512
POOL_BLOCK = 2048
KV_BLOCK = 1024
CAST_STEPS = 16
SAMPLE_HEAD_GROUP = 4
KV_ROWS = 2 * FOX_WIDTH + 16
MXU_DIM = 256
FFN_SPLITS = ((0, 6 * MXU_DIM), (6 * MXU_DIM, D_FF))
VMEM_LIMIT = 56 * 1024 * 1024


def _dot(a, b):
    return jnp.dot(a, b, preferred_element_type=F32)


def _dot_nt(a, b):
    return lax.dot_general(a, b, (((1,), (1,)), ((), ())), preferred_element_type=F32)


def _rms(x, g):
    ms = jnp.mean(x * x, axis=-1, keepdims=True)
    return (x * lax.rsqrt(ms + EPS)) * g


def _rms_head64(x, g_pair):
    lo = lax.broadcasted_iota(jnp.int32, (1, LANES), 1) < FOX_HEAD_DIM
    outs = []
    for c in range(x.shape[-1] // LANES):
        xc = x[:, c * LANES:(c + 1) * LANES]
        sq = xc * xc
        s_lo = jnp.sum(jnp.where(lo, sq, 0.0), axis=-1, keepdims=True)
        s_hi = jnp.sum(jnp.where(lo, 0.0, sq), axis=-1, keepdims=True)
        ms = jnp.where(lo, s_lo, s_hi) * (1.0 / FOX_HEAD_DIM)
        outs.append((xc * lax.rsqrt(ms + EPS)) * g_pair)
    return outs


def _const_spec(shape):
    return pl.BlockSpec(shape, lambda *_: (0,) * len(shape), pipeline_mode=pl.Buffered(1))


def _layer_spec(layer, shape):
    return pl.BlockSpec((1,) + shape, lambda *_: (layer,) + (0,) * len(shape),
                        pipeline_mode=pl.Buffered(1))


MEM_ROWS = MEM_TOKENS * MEM_HEADS


def _mem_spec(layer, nb):
    return pl.BlockSpec((1, nb, MEM_ROWS, MEM_HEAD_DIM), lambda b, *_: (layer, b, 0, 0))


def _head_rows(h):
    return pl.ds(h, MEM_TOKENS, stride=MEM_HEADS)


def _params(n_grid, flags=None):
    return pltpu.CompilerParams(
        dimension_semantics=("arbitrary",) * n_grid, vmem_limit_bytes=VMEM_LIMIT, flags=flags)


def _mem_kv_kernel(mem_ref, g_ref, w_ref, kn_ref, *rest):
    n = (len(rest) - 2) // 2
    src, (k_ref, v_ref), dst = rest[:n], rest[n:n + 2], rest[n + 2:]
    for s_ref, d_ref in zip(src, dst):
        d_ref[...] = s_ref[...].astype(BF16)
    nb = mem_ref.shape[0]
    x = mem_ref[...].reshape(nb * MEM_TOKENS, D_MODEL)
    kv = _dot(_rms(x, g_ref[0]).astype(BF16), w_ref[0].astype(BF16))
    for h in range(MEM_HEADS):
        ks = slice(h * MEM_HEAD_DIM, (h + 1) * MEM_HEAD_DIM)
        vs = slice(MEM_WIDTH + h * MEM_HEAD_DIM, MEM_WIDTH + (h + 1) * MEM_HEAD_DIM)
        k_ref[0, :, _head_rows(h), :] = _rms(kv[:, ks], kn_ref[0]).reshape(nb, MEM_TOKENS, MEM_HEAD_DIM)
        v_ref[0, :, _head_rows(h), :] = kv[:, vs].reshape(nb, MEM_TOKENS, MEM_HEAD_DIM)


def _mem_kv(mem, g_mem, w_mem_kv, k_norm_mem, first_f32):
    B = mem.shape[0]
    nb = 4
    steps = DEPTH * (B // nb)
    out = jax.ShapeDtypeStruct((DEPTH, B, MEM_ROWS, MEM_HEAD_DIM), F32)
    out_spec = pl.BlockSpec((1, nb, MEM_ROWS, MEM_HEAD_DIM), lambda i, b: (i, b, 0, 0))
    chunk = lambda w: (1, w.shape[1] // steps, w.shape[2])
    chunk_spec = lambda w: pl.BlockSpec(chunk(w), lambda i, b: (0, i * (B // nb) + b, 0))
    outs = pl.pallas_call(
        _mem_kv_kernel,
        grid=(DEPTH, B // nb),
        in_specs=[
            pl.BlockSpec((nb, MEM_TOKENS, D_MODEL), lambda i, b: (b, 0, 0)),
            pl.BlockSpec((1, 1, D_MODEL), lambda i, b: (i, 0, 0)),
            pl.BlockSpec((1, D_MODEL, 2 * MEM_WIDTH), lambda i, b: (i, 0, 0)),
            pl.BlockSpec((1, 1, MEM_HEAD_DIM), lambda i, b: (i, 0, 0)),
        ] + [chunk_spec(w) for w in first_f32],
        out_specs=[out_spec, out_spec] + [chunk_spec(w) for w in first_f32],
        out_shape=[out, out] + [jax.ShapeDtypeStruct((1,) + w.shape[1:], BF16) for w in first_f32],
        compiler_params=_params(2),
        name="mem_kv",
    )(mem, g_mem, w_mem_kv, k_norm_mem, *first_f32)
    return outs[0], outs[1], outs[2:]


def _in_proj(x_ref, g_ref, w_ref):
    nb, tl, _ = x_ref.shape
    x = x_ref[...].reshape(nb * tl, D_MODEL)
    return _dot(_rms(x, g_ref[0]).astype(BF16), w_ref[0])


def _mem_attend(zq, qn, mk_ref, mv_ref, i, cat_ref, col0):
    for h in range(MEM_HEADS):
        sl = slice(h * MEM_HEAD_DIM, (h + 1) * MEM_HEAD_DIM)
        q = _rms(zq[:, sl], qn).astype(BF16)
        s = _dot_nt(q, mk_ref[0, i, _head_rows(h), :].astype(BF16)) * (MEM_SCALE * LOG2E)
        p = jnp.exp2(s - jnp.max(s, axis=-1, keepdims=True)).astype(BF16)
        v = mv_ref[0, i, _head_rows(h), :].astype(BF16)
        o = _dot(p, jnp.concatenate([v, jnp.ones_like(v)], axis=-1))
        o = o[:, :MEM_HEAD_DIM] / o[:, MEM_HEAD_DIM:]
        cat_ref[i, :, col0 + h * MEM_HEAD_DIM:col0 + (h + 1) * MEM_HEAD_DIM] = o.astype(cat_ref.dtype)


def _mem_attend_paired(zq, qn, mk_ref, mv_ref, i, cat_ref, col0):
    z = jnp.zeros((MEM_TOKENS, MEM_HEAD_DIM), BF16)
    one = jnp.ones((MEM_TOKENS, MEM_HEAD_DIM), BF16)
    for c in range(MEM_HEADS // 2):
        hs = (2 * c, 2 * c + 1)
        q = jnp.concatenate(
            [_rms(zq[:, h * MEM_HEAD_DIM:(h + 1) * MEM_HEAD_DIM], qn) for h in hs], axis=1)
        k0, k1 = (mk_ref[0, i, _head_rows(h), :].astype(BF16) for h in hs)
        v0, v1 = (mv_ref[0, i, _head_rows(h), :].astype(BF16) for h in hs)
        k_pair = jnp.concatenate([jnp.concatenate([k0, z], axis=1),
                                  jnp.concatenate([z, k1], axis=1)], axis=0)
        v_pair = jnp.concatenate([jnp.concatenate([v0, one, z, z], axis=1),
                                  jnp.concatenate([z, z, v1, one], axis=1)], axis=0)
        s = _dot_nt(q.astype(BF16), k_pair) * (MEM_SCALE * LOG2E)
        p = jnp.concatenate(
            [jnp.exp2(sh - jnp.max(sh, axis=-1, keepdims=True))
             for sh in (s[:, :MEM_TOKENS], s[:, MEM_TOKENS:])], axis=1).astype(BF16)
        o = _dot(p, v_pair)
        for n, h in enumerate(hs):
            oh = o[:, 2 * n * MEM_HEAD_DIM:(2 * n + 1) * MEM_HEAD_DIM]
            lh = o[:, (2 * n + 1) * MEM_HEAD_DIM:(2 * n + 2) * MEM_HEAD_DIM]
            cat_ref[i, :, col0 + h * MEM_HEAD_DIM:col0 + (h + 1) * MEM_HEAD_DIM] = (
                (oh / lh).astype(cat_ref.dtype))


def _mixer_pool_kernel(x_ref, g_ref, w_in_ref, qn_ref, mk_ref, mv_ref, hist_ref, wp_ref, ps_ref,
                       cat_ref, state_ref, ubuf, *, pos0):
    nb, tl, _ = x_ref.shape
    j = pl.program_id(1)
    z = _in_proj(x_ref, g_ref, w_in_ref)

    @pl.when(j == 0)
    def _():
        ubuf[:, 0:HIST_ROWS, :] = hist_ref[0]

    pos = pos0 + j * tl + lax.broadcasted_iota(jnp.int32, (tl, 1), 0)
    for i in range(nb):
        zi = z[i * tl:(i + 1) * tl]
        u = zi[:, :POOL_WIDTH]
        ubuf[i, HIST_ROWS:HIST_ROWS + tl, :] = u
        for g, w in enumerate(POOL_WINDOWS):
            sl = slice(g * POOL_GROUP_DIM, (g + 1) * POOL_GROUP_DIM)
            ug = u[:, sl]
            acc = ug
            for k in range(1, w):
                acc = acc + ubuf[i, HIST_ROWS - k:HIST_ROWS - k + tl, sl]
            cnt = jnp.minimum(pos + 1, w).astype(F32)
            d = acc / cnt - ug
            y = _dot(d.astype(BF16), wp_ref[0, g]) * ps_ref[0, :, sl]
            cat_ref[i, :, sl] = y.astype(cat_ref.dtype)
        _mem_attend(zi[:, POOL_WIDTH:], qn_ref[0], mk_ref, mv_ref, i, cat_ref, POOL_WIDTH)
        tail = ubuf[i, tl:tl + HIST_ROWS, :]
        state_ref[0, i] = tail
        ubuf[i, 0:HIST_ROWS, :] = tail


def _mixer_pool(layer, x, hist, mem_k, mem_v, W, *, pos0, nb, tl):
    B, L, _ = x.shape
    assert tl >= HIST_ROWS and L % tl == 0 and B % nb == 0
    hist_spec = pl.BlockSpec((1, nb, HIST_ROWS, POOL_WIDTH), lambda b, j: (layer, b, 0, 0))
    return pl.pallas_call(
        functools.partial(_mixer_pool_kernel, pos0=pos0),
        grid=(B // nb, L // tl),
        in_specs=[
            pl.BlockSpec((nb, tl, D_MODEL), lambda b, j: (b, j, 0)),
            _layer_spec(layer, (1, D_MODEL)),
            _layer_spec(0, (D_MODEL, MIX_WIDTH)),
            _layer_spec(layer, (1, MEM_HEAD_DIM)),
            _mem_spec(layer, nb),
            _mem_spec(layer, nb),
            hist_spec,
            _layer_spec(layer, (POOL_GROUPS, POOL_GROUP_DIM, POOL_GROUP_DIM)),
            _layer_spec(layer, (1, POOL_WIDTH)),
        ],
        out_specs=[
            pl.BlockSpec((nb, tl, MIX_WIDTH), lambda b, j: (b, j, 0)),
            pl.BlockSpec((1, nb, HIST_ROWS, POOL_WIDTH), lambda b, j: (0, b, 0, 0)),
        ],
        out_shape=[
            jax.ShapeDtypeStruct((B, L, MIX_WIDTH), BF16),
            jax.ShapeDtypeStruct((1, B, HIST_ROWS, POOL_WIDTH), F32),
        ],
        scratch_shapes=[pltpu.VMEM((nb, HIST_ROWS + tl, POOL_WIDTH), F32)],
        compiler_params=_params(2),
        name="mixer_pool",
    )(x, W["g_mix"], W["w_in"][layer], W["q_norm_mem"], mem_k, mem_v, hist, W["w_pool"], W["pool_scale"])


def _kv_proj_kernel(x_ref, g_ref, w_ref, wv_ref, kn_ref, bf_ref, k_ref, v_ref, lf_ref, kb_ref, vb_ref):
    nb, tl, _ = x_ref.shape
    rows = nb * tl
    x = x_ref[...].reshape(rows, D_MODEL)
    xn = _rms(x, g_ref[...]).astype(BF16)
    zt = _dot_nt(w_ref[...], xn)
    zv = _dot(xn, wv_ref[...])
    lo = lax.broadcasted_iota(jnp.int32, (1, LANES), 1) < FOX_HEAD_DIM
    for h in range(FOX_HEADS):
        pair = zv[:, (h // 2) * LANES:(h // 2 + 1) * LANES]
        base = pair if h % 2 == 0 else pltpu.roll(pair, FOX_HEAD_DIM, 1)
        vb_ref[:, h] = jnp.where(lo, base, 1.0).astype(BF16).reshape(nb, tl, LANES)
    k3 = zt[:FOX_WIDTH].reshape(FOX_HEADS, FOX_HEAD_DIM, rows)
    ms = jnp.mean(k3 * k3, axis=1, keepdims=True)
    k3 = (k3 * lax.rsqrt(ms + EPS)) * kn_ref[...]
    v3 = zt[FOX_WIDTH:2 * FOX_WIDTH].reshape(FOX_HEADS, FOX_HEAD_DIM, rows)
    t = -(zt[2 * FOX_WIDTH:2 * FOX_WIDTH + FOX_HEADS] + bf_ref[...])
    lf = -(jnp.maximum(t, 0.0) + jnp.log1p(jnp.exp(-jnp.abs(t))))
    for i in range(nb):
        cols = slice(i * tl, (i + 1) * tl)
        k_ref[i] = k3[:, :, cols]
        v_ref[i] = v3[:, :, cols]
        lf_ref[i] = lf[:, cols]
        kb_ref[i] = k3[:, :, cols].astype(BF16)


def _kv_proj(x, W, *, nb, tl):
    B, L, _ = x.shape
    hd = pl.BlockSpec((nb, FOX_HEADS, FOX_HEAD_DIM, tl), lambda b, j: (b, 0, 0, j))
    heads = jax.ShapeDtypeStruct((B, FOX_HEADS, FOX_HEAD_DIM, L), F32)
    return pl.pallas_call(
        _kv_proj_kernel,
        grid=(B // nb, L // tl),
        in_specs=[
            pl.BlockSpec((nb, tl, D_MODEL), lambda b, j: (b, j, 0)),
            _const_spec((1, D_MODEL)),
            _const_spec((KV_ROWS, D_MODEL)),
            _const_spec((D_MODEL, FOX_WIDTH)),
            _const_spec((1, FOX_HEAD_DIM, 1)),
            _const_spec((FOX_HEADS, 1)),
        ],
        out_specs=[hd, hd, pl.BlockSpec((nb, FOX_HEADS, tl), lambda b, j: (b, 0, j)), hd,
                   pl.BlockSpec((nb, FOX_HEADS, tl, LANES), lambda b, j: (b, 0, j, 0))],
        out_shape=[heads, heads, jax.ShapeDtypeStruct((B, FOX_HEADS, L), F32),
                   jax.ShapeDtypeStruct((B, FOX_HEADS, FOX_HEAD_DIM, L), BF16),
                   jax.ShapeDtypeStruct((B, FOX_HEADS, L, LANES), BF16)],
        compiler_params=_params(2),
        name="kv_proj",
    )(x, W["g_kv"], W["w_kv_t"], W["w_v"], W["kn_col"], W["b_f"])


def _cumsum_kernel(lf_ref, f_ref):
    rows, n = lf_ref.shape
    r = lax.broadcasted_iota(jnp.int32, (LANES, LANES), 0)
    c = lax.broadcasted_iota(jnp.int32, (LANES, LANES), 1)
    tri = jnp.where(r <= c, 1.0, 0.0).astype(BF16)
    carry = jnp.zeros((rows, 1), F32)
    for ch in range(n // LANES):
        x = lf_ref[:, ch * LANES:(ch + 1) * LANES]
        hi = x.astype(BF16)
        r1 = x - hi.astype(F32)
        mid = r1.astype(BF16)
        low = (r1 - mid.astype(F32)).astype(BF16)
        y = (_dot(hi, tri) + _dot(mid, tri)) + _dot(low, tri) + carry
        f_ref[:, ch * LANES:(ch + 1) * LANES] = y
        carry = y[:, LANES - 1:LANES]


def _cumsum_lanes(lf_t):
    return pl.pallas_call(
        _cumsum_kernel,
        out_shape=jax.ShapeDtypeStruct(lf_t.shape, F32),
        name="logf_cumsum",
    )(lf_t)


def _tri_unrank(t, n):
    row = sum((t >= k * (k + 1) // 2).astype(jnp.int32) for k in range(1, n))
    return row, t - row * (row + 1) // 2


def _split3(x):
    hi = x.astype(BF16).astype(F32)
    r1 = x - hi
    mid = r1.astype(BF16).astype(F32)
    low = (r1 - mid).astype(BF16).astype(F32)
    return hi, mid, low


def _mixer_fox_prompt_kernel(x_ref, g_ref, w_in_ref, qn_ref, mk_ref, mv_ref, qnf_ref,
                             kt_ref, v_ref, ft_ref, fq_ref, cat_ref, q_sc, m_sc, acc_sc, mask_sc,
                             *, nkb):
    _, tl, _ = x_ref.shape
    tk = kt_ref.shape[-1]
    j, kb = _tri_unrank(pl.program_id(1), nkb)
    lane = lax.broadcasted_iota(jnp.int32, (1, LANES), 1)
    row = lax.broadcasted_iota(jnp.int32, (AUG_ROWS, 1), 0)

    @pl.when(kb == 0)
    def _():
        z = _in_proj(x_ref, g_ref, w_in_ref)
        _mem_attend_paired(z[:, FOX_WIDTH:], qn_ref[0], mk_ref, mv_ref, 0, cat_ref, FOX_WIDTH)
        qs = _rms_head64(z[:, :FOX_WIDTH], qnf_ref[0])
        for h in range(FOX_HEADS):
            base = qs[h // 2] if h % 2 == 0 else pltpu.roll(qs[h // 2], FOX_HEAD_DIM, 1)
            hi, mid, low = _split3(fq_ref[0, :, h:h + 1] * LOG2E)
            tail = jnp.where(lane < FOX_HEAD_DIM + 3, 1.0,
                             jnp.where(lane == FOX_HEAD_DIM + 3, hi,
                                       jnp.where(lane == FOX_HEAD_DIM + 4, mid,
                                                 jnp.where(lane == FOX_HEAD_DIM + 5, low, 0.0))))
            q_sc[h] = jnp.where(lane < FOX_HEAD_DIM, base * (FOX_SCALE * LOG2E), tail).astype(BF16)
        m_sc[...] = jnp.full(m_sc.shape, -jnp.inf, F32)
        acc_sc[...] = jnp.zeros(acc_sc.shape, F32)
        causal = (lax.broadcasted_iota(jnp.int32, (tl, 1), 0)
                  >= lax.broadcasted_iota(jnp.int32, (1, tk), 1))
        mask_sc[...] = jnp.where(causal, 0.0, -jnp.inf)

    def k_aug(h):
        hi, mid, low = _split3(ft_ref[0, h:h + 1, :] * (-LOG2E))
        aug = jnp.where(row == 0, hi, jnp.where(row == 1, mid, jnp.where(row == 2, low,
                        jnp.where(row < 6, 1.0, 0.0)))).astype(BF16)
        pad = jnp.zeros((FOX_HEAD_DIM - AUG_ROWS, tk), BF16)
        return jnp.concatenate([kt_ref[0, h], aug, pad], axis=0)

    def attend(masked):
        zk = jnp.zeros((LANES, tk), BF16)
        zv = jnp.zeros((tk, LANES), BF16)
        for c in range(FOX_HEADS // 2):
            h0, h1 = 2 * c, 2 * c + 1
            k_pair = jnp.concatenate([jnp.concatenate([k_aug(h0), zk], axis=1),
                                      jnp.concatenate([zk, k_aug(h1)], axis=1)], axis=0)
            v_pair = jnp.concatenate([jnp.concatenate([v_ref[0, h0], zv], axis=1),
                                      jnp.concatenate([zv, v_ref[0, h1]], axis=1)], axis=0)
            q_pair = jnp.concatenate([q_sc[h0], q_sc[h1]], axis=1)
            s = _dot(q_pair, k_pair)
            ps, alphas = [], []
            for h, sh in ((h0, s[:, :tk]), (h1, s[:, tk:])):
                if masked:
                    sh = sh + mask_sc[...]
                m_old = m_sc[h]
                m_new = jnp.maximum(m_old, jnp.max(sh, axis=-1, keepdims=True))
                ps.append(jnp.exp2(sh - m_new[:, 0:1]).astype(BF16))
                alphas.append(jnp.exp2(m_old - m_new))
                m_sc[h] = m_new
            o = _dot(jnp.concatenate(ps, axis=1), v_pair)
            acc_sc[h0] = alphas[0] * acc_sc[h0] + o[:, :LANES]
            acc_sc[h1] = alphas[1] * acc_sc[h1] + o[:, LANES:]

    @pl.when(kb < j)
    def _():
        attend(False)

    @pl.when(kb == j)
    def _():
        attend(True)
        for c in range(FOX_HEADS // 2):
            a0 = acc_sc[2 * c]
            a1 = acc_sc[2 * c + 1]
            o0 = a0 / pltpu.roll(a0, FOX_HEAD_DIM, 1)
            o1 = pltpu.roll(a1, FOX_HEAD_DIM, 1) / a1
            cat_ref[0, :, c * LANES:(c + 1) * LANES] = (
                jnp.where(lane < FOX_HEAD_DIM, o0, o1).astype(cat_ref.dtype))


def _mixer_fox_prompt(layer, x, kt_b, vt_b, f_t, f_q, mem_k, mem_v, W, *, tl):
    B, L, _ = x.shape
    nkb = L // tl
    jj = lambda t: _tri_unrank(t, nkb)[0]
    kk = lambda t: _tri_unrank(t, nkb)[1]
    kt_spec = pl.BlockSpec((1, FOX_HEADS, FOX_HEAD_DIM, tl), lambda b, t: (b, 0, 0, kk(t)))
    v_spec = pl.BlockSpec((1, FOX_HEADS, tl, LANES), lambda b, t: (b, 0, kk(t), 0))
    return pl.pallas_call(
        functools.partial(_mixer_fox_prompt_kernel, nkb=nkb),
        grid=(B, nkb * (nkb + 1) // 2),
        in_specs=[
            pl.BlockSpec((1, tl, D_MODEL), lambda b, t: (b, jj(t), 0)),
            _layer_spec(layer, (1, D_MODEL)),
            _layer_spec(0, (D_MODEL, MIX_WIDTH)),
            _layer_spec(layer, (1, MEM_HEAD_DIM)),
            _mem_spec(layer, 1),
            _mem_spec(layer, 1),
            _layer_spec(layer - N_A, (1, LANES)),
            kt_spec,
            v_spec,
            pl.BlockSpec((1, FOX_HEADS, tl), lambda b, t: (b, 0, kk(t))),
            pl.BlockSpec((1, tl, FOX_HEADS), lambda b, t: (b, jj(t), 0)),
        ],
        out_specs=pl.BlockSpec((1, tl, MIX_WIDTH), lambda b, t: (b, jj(t), 0)),
        out_shape=jax.ShapeDtypeStruct((B, L, MIX_WIDTH), BF16),
        scratch_shapes=[
            pltpu.VMEM((FOX_HEADS, tl, LANES), BF16),
            pltpu.VMEM((FOX_HEADS, tl, LANES), F32),
            pltpu.VMEM((FOX_HEADS, tl, LANES), F32),
            pltpu.VMEM((tl, tl), F32),
        ],
        compiler_params=_params(2),
        name="mixer_fox_prompt",
    )(x, W["g_mix"], W["w_in"][layer], W["q_norm_mem"], mem_k, mem_v, W["qnf_pair"], kt_b, vt_b, f_t, f_q)


def _block_diag(blocks):
    z = jnp.zeros_like(blocks[0])
    n = len(blocks)
    return jnp.concatenate(
        [jnp.concatenate([b if j == i else z for j in range(n)], axis=1)
         for i, b in enumerate(blocks)], axis=0)


def _fox_sample_pre_kernel(x_ref, g_ref, w_in_ref, qn_ref, mk_ref, mv_ref, qnf_ref, q_ref, cm_ref):
    nb, tl, _ = x_ref.shape
    z = _in_proj(x_ref, g_ref, w_in_ref)
    for i in range(nb):
        zi = z[i * tl:(i + 1) * tl]
        qs = _rms_head64(zi[:, :FOX_WIDTH], qnf_ref[0])
        for c in range(FOX_HEADS // 2):
            q_ref[i, :, c * LANES:(c + 1) * LANES] = (qs[c] * FOX_SCALE).astype(BF16)
        _mem_attend(zi[:, FOX_WIDTH:], qn_ref[0], mk_ref, mv_ref, i, cm_ref, 0)


def _fox_sample_attn_kernel(q_ref, ktp_ref, vtp_ref, ktn_ref, vtn_ref, ft_ref, fq_ref, cat_ref):
    nb, tl, _ = q_ref.shape
    past = ktp_ref.shape[-1]
    causal = (lax.broadcasted_iota(jnp.int32, (tl, 1), 0)
              >= lax.broadcasted_iota(jnp.int32, (1, tl), 1))
    lo = lax.broadcasted_iota(jnp.int32, (1, LANES), 1) < FOX_HEAD_DIM
    gw = SAMPLE_HEAD_GROUP * FOX_HEAD_DIM
    for i in range(nb):
        for g in range(FOX_HEADS // SAMPLE_HEAD_GROUP):
            hs = tuple(range(g * SAMPLE_HEAD_GROUP, (g + 1) * SAMPLE_HEAD_GROUP))
            q = q_ref[i, :, g * gw:(g + 1) * gw]
            s_p = _dot(q, _block_diag([ktp_ref[i, h].astype(BF16) for h in hs]))
            s_n = _dot(q, _block_diag([ktn_ref[i, h].astype(BF16) for h in hs]))
            pp, pn, inv_l = [], [], []
            for n, h in enumerate(hs):
                fq = fq_ref[i, :, h:h + 1]
                sp = (s_p[:, n * past:(n + 1) * past] + fq) - ft_ref[i, h:h + 1, 0:past]
                sn = (s_n[:, n * tl:(n + 1) * tl] + fq) - ft_ref[i, h:h + 1, past:past + tl]
                sn = jnp.where(causal, sn, -jnp.inf)
                m = jnp.maximum(jnp.max(sp, axis=-1, keepdims=True),
                                jnp.max(sn, axis=-1, keepdims=True))
                ep = jnp.exp(sp - m)
                en = jnp.exp(sn - m)
                inv_l.append(1.0 / (jnp.sum(ep, axis=-1, keepdims=True)
                                    + jnp.sum(en, axis=-1, keepdims=True)))
                pp.append(ep.astype(BF16))
                pn.append(en.astype(BF16))
            o = (_dot_nt(jnp.concatenate(pp, axis=1),
                         _block_diag([vtp_ref[i, h].astype(BF16) for h in hs]))
                 + _dot_nt(jnp.concatenate(pn, axis=1),
                           _block_diag([vtn_ref[i, h].astype(BF16) for h in hs])))
            scale = jnp.concatenate([jnp.where(lo, inv_l[n], inv_l[n + 1])
                                     for n in range(0, SAMPLE_HEAD_GROUP, 2)], axis=1)
            cat_ref[i, :, g * gw:(g + 1) * gw] = (o * scale).astype(cat_ref.dtype)


def _mixer_fox_sample(layer, x, kt_past, vt_past, kt_new, vt_new, f_t, f_q, mem_k, mem_v, W, *,
                      nb_dense, nb_attn):
    B, L, _ = x.shape
    past = kt_past.shape[-1]
    lk_pad = f_t.shape[-1]
    blk = lambda nb, *tail: pl.BlockSpec((nb,) + tail, lambda b: (b,) + (0,) * len(tail))
    half = jax.ShapeDtypeStruct((B, L, FOX_WIDTH), BF16)
    q, cat_mem = pl.pallas_call(
        _fox_sample_pre_kernel,
        grid=(B // nb_dense,),
        in_specs=[
            blk(nb_dense, L, D_MODEL),
            _layer_spec(layer, (1, D_MODEL)),
            _layer_spec(0, (D_MODEL, MIX_WIDTH)),
            _layer_spec(layer, (1, MEM_HEAD_DIM)),
            _mem_spec(layer, nb_dense),
            _mem_spec(layer, nb_dense),
            _layer_spec(layer - N_A, (1, LANES)),
        ],
        out_specs=[blk(nb_dense, L, FOX_WIDTH), blk(nb_dense, L, MEM_WIDTH)],
        out_shape=[half, half],
        compiler_params=_params(1),
        name="fox_sample_pre",
    )(x, W["g_mix"], W["w_in"][layer], W["q_norm_mem"], mem_k, mem_v, W["qnf_pair"])
    cat_fox = pl.pallas_call(
        _fox_sample_attn_kernel,
        grid=(B // nb_attn,),
        in_specs=[
            blk(nb_attn, L, FOX_WIDTH),
            blk(nb_attn, FOX_HEADS, FOX_HEAD_DIM, past),
            blk(nb_attn, FOX_HEADS, FOX_HEAD_DIM, past),
            blk(nb_attn, FOX_HEADS, FOX_HEAD_DIM, L),
            blk(nb_attn, FOX_HEADS, FOX_HEAD_DIM, L),
            blk(nb_attn, FOX_HEADS, lk_pad),
            blk(nb_attn, L, FOX_HEADS),
        ],
        out_specs=blk(nb_attn, L, FOX_WIDTH),
        out_shape=half,
        compiler_params=_params(1),
        name="fox_sample_attn",
    )(q, kt_past, vt_past, kt_new, vt_new, f_t, f_q)
    return jnp.concatenate([cat_fox, cat_mem], axis=-1)


def _out_ffn_kernel(xa_ref, cata_ref, xb_ref, catb_ref, w_out_ref, g_ref, w_gu_ref, w_down_ref,
                    *rest, na, n_cast):
    if n_cast:
        src, (ya_ref, yb_ref), dst = rest[:4], rest[4:6], rest[6:]

        @pl.when(pl.program_id(0) < n_cast)
        def _():
            for s_ref, d_ref in zip(src, dst):
                d_ref[...] = s_ref[...].astype(BF16)
    else:
        ya_ref, yb_ref = rest

    def body(x_ref, cat_ref, y_ref):
        x1 = x_ref[...] + _dot(cat_ref[...], w_out_ref[0])
        xn = _rms(x1, g_ref[0]).astype(BF16)
        acc = x1
        for lo, hi in FFN_SPLITS:
            gate = _dot(xn, w_gu_ref[0, :, lo:hi])
            up = _dot(xn, w_gu_ref[0, :, D_FF + lo:D_FF + hi])
            h = (gate * (1.0 / (1.0 + jnp.exp(-gate)))) * up
            acc = acc + _dot(h.astype(BF16), w_down_ref[0, lo:hi, :])
        y_ref[...] = acc

    @pl.when(pl.program_id(0) < na)
    def _():
        body(xa_ref, cata_ref, ya_ref)

    @pl.when(pl.program_id(0) >= na)
    def _():
        body(xb_ref, catb_ref, yb_ref)


def _out_ffn(layer, xa, cata, xb, catb, W, next_f32=()):
    na = xa.shape[0] // ROW_BLOCK
    nb = xb.shape[0] // ROW_BLOCK
    a_map = lambda r: (jnp.minimum(r, na - 1), 0)
    b_map = lambda r: (jnp.maximum(r - na, 0), 0)
    n_cast = CAST_STEPS if next_f32 else 0
    assert n_cast <= na + nb
    chunk = lambda w: (1, w.shape[1] // CAST_STEPS, w.shape[2])
    src_specs = [pl.BlockSpec(chunk(w), lambda r: (layer + 1, jnp.minimum(r, CAST_STEPS - 1), 0))
                 for w in next_f32]
    dst_specs = [pl.BlockSpec(chunk(w), lambda r: (0, jnp.minimum(r, CAST_STEPS - 1), 0))
                 for w in next_f32]
    dst_shapes = [jax.ShapeDtypeStruct((1,) + w.shape[1:], BF16) for w in next_f32]
    outs = pl.pallas_call(
        functools.partial(_out_ffn_kernel, na=na, n_cast=n_cast),
        grid=(na + nb,),
        in_specs=[
            pl.BlockSpec((ROW_BLOCK, D_MODEL), a_map),
            pl.BlockSpec((ROW_BLOCK, MIX_WIDTH), a_map),
            pl.BlockSpec((ROW_BLOCK, D_MODEL), b_map),
            pl.BlockSpec((ROW_BLOCK, MIX_WIDTH), b_map),
            _layer_spec(0, (MIX_WIDTH, D_MODEL)),
            _layer_spec(layer, (1, D_MODEL)),
            _layer_spec(0, (D_MODEL, 2 * D_FF)),
            _layer_spec(0, (D_FF, D_MODEL)),
        ] + src_specs,
        out_specs=[pl.BlockSpec((ROW_BLOCK, D_MODEL), a_map),
                   pl.BlockSpec((ROW_BLOCK, D_MODEL), b_map)] + dst_specs,
        out_shape=[jax.ShapeDtypeStruct(xa.shape, F32), jax.ShapeDtypeStruct(xb.shape, F32)]
        + dst_shapes,
        compiler_params=_params(1),
        name="out_ffn",
    )(xa, cata, xb, catb, W["w_out"][layer], W["g_ffn"], W["w_gu"][layer], W["w_down"][layer],
      *next_f32)
    return outs[0], outs[1], outs[2:]


def _mixer(i, x, st, W):
    B, L, _ = x.shape
    if i < N_A:
        cat, state = _mixer_pool(i, x, st["hist"], st["mem_k"], st["mem_v"], W, pos0=st["pos0"],
                                 nb=st["nb"], tl=min(L, POOL_BLOCK))
        st["pool_states"].append(state[:, :, 1:, :])
        return cat
    past = st["past"]
    if i == N_A:
        kt_new, vt_new, lft_new, kt_b, v_b = _kv_proj(x, W, nb=st["nb"], tl=st["tl"])
        lft_all = lft_new if past is None else jnp.concatenate([past[2], lft_new], axis=2)
        lk = lft_all.shape[2]
        lk_pad = -(-lk // LANES) * LANES
        lf_t = jnp.pad(lft_all.reshape(B * FOX_HEADS, lk), ((0, 0), (0, lk_pad - lk)))
        f_t = _cumsum_lanes(lf_t).reshape(B, FOX_HEADS, lk_pad)
        st.update(kt_new=kt_new, vt_new=vt_new, lft_new=lft_new, kt_b=kt_b, v_b=v_b, f_t=f_t,
                  f_q=jnp.swapaxes(f_t[:, :, lk - L:lk], 1, 2))
    if past is None:
        return _mixer_fox_prompt(i, x, st["kt_b"], st["v_b"], st["f_t"], st["f_q"], st["mem_k"],
                                 st["mem_v"], W, tl=FOX_BLOCK)
    return _mixer_fox_sample(i, x, past[0], past[1], st["kt_new"], st["vt_new"], st["f_t"],
                             st["f_q"], st["mem_k"], st["mem_v"], W, nb_dense=st["nb"], nb_attn=4)


def _stream_outputs(st):
    return (jnp.concatenate(st["pool_states"], axis=0), jnp.transpose(st["kt_new"], (0, 3, 1, 2)),
            jnp.transpose(st["vt_new"], (0, 3, 1, 2)), jnp.swapaxes(st["lft_new"], 1, 2))


def kernel(x_prompt, x_sample, state_pool, cache_fox_k, cache_fox_v, cache_fox_logf, cache_mem_k,
           cache_mem_v, mem_prompt, g_mix, w_in, w_out, q_norm_mem, g_mem, w_mem_kv, k_norm_mem,
           w_pool, pool_scale, q_norm_fox, g_kv, w_kv, k_norm_fox, b_f, g_ffn, w_gu, w_down):
    B, L, _ = x_prompt.shape
    SB, SL, _ = x_sample.shape
    big = (w_in, w_out, w_gu, w_down)
    W = dict(
        g_mix=g_mix.reshape(DEPTH, 1, D_MODEL),
        q_norm_mem=q_norm_mem.reshape(DEPTH, 1, MEM_HEAD_DIM), w_pool=w_pool.astype(BF16),
        pool_scale=pool_scale.reshape(N_A, 1, POOL_WIDTH),
        qnf_pair=jnp.tile(q_norm_fox, (1, 2)).reshape(DEPTH - N_A, 1, LANES),
        g_kv=g_kv.reshape(1, D_MODEL),
        w_kv_t=jnp.pad(w_kv.T, ((0, KV_ROWS - w_kv.shape[1]), (0, 0))).astype(BF16),
        w_v=w_kv[:, FOX_WIDTH:2 * FOX_WIDTH].astype(BF16),
        kn_col=k_norm_fox.reshape(1, FOX_HEAD_DIM, 1), b_f=b_f.reshape(FOX_HEADS, 1),
        g_ffn=g_ffn.reshape(DEPTH, 1, D_MODEL))

    mem_k_p, mem_v_p, first = _mem_kv(
        mem_prompt, g_mem.reshape(DEPTH, 1, D_MODEL), w_mem_kv,
        k_norm_mem.reshape(DEPTH, 1, MEM_HEAD_DIM), big)
    for name, w0 in zip(("w_in", "w_out", "w_gu", "w_down"), first):
        W[name] = [w0]
    prompt = dict(pos0=0, hist=jnp.zeros((N_A, B, HIST_ROWS, POOL_WIDTH), F32), past=None,
                  mem_k=mem_k_p, mem_v=mem_v_p, nb=1, tl=KV_BLOCK, pool_states=[])
    sample = dict(
        pos0=PAST_LEN, hist=jnp.pad(state_pool, ((0, 0), (0, 0), (1, 0), (0, 0))),
        past=(jnp.transpose(cache_fox_k, (0, 2, 3, 1)), jnp.transpose(cache_fox_v, (0, 2, 3, 1)),
              jnp.swapaxes(cache_fox_logf, 1, 2)),
        mem_k=cache_mem_k.reshape(DEPTH, SB, MEM_ROWS, MEM_HEAD_DIM),
        mem_v=cache_mem_v.reshape(DEPTH, SB, MEM_ROWS, MEM_HEAD_DIM),
        nb=ROW_BLOCK // SL, tl=SL, pool_states=[])

    y_p, y_s = x_prompt, x_sample
    for i in range(DEPTH):
        cat_p = _mixer(i, y_p, prompt, W)
        cat_s = _mixer(i, y_s, sample, W)
        y_p, y_s, nxt = _out_ffn(i, y_p.reshape(B * L, D_MODEL), cat_p.reshape(B * L, MIX_WIDTH),
                                 y_s.reshape(SB * SL, D_MODEL), cat_s.reshape(SB * SL, MIX_WIDTH),
                                 W, next_f32=big if i + 1 < DEPTH else ())
        for name, w_next in zip(("w_in", "w_out", "w_gu", "w_down"), nxt):
            W[name].append(w_next)
        y_p = y_p.reshape(B, L, D_MODEL)
        y_s = y_s.reshape(SB, SL, D_MODEL)

    pool_p, fox_k_p, fox_v_p, fox_lf_p = _stream_outputs(prompt)
    pool_s, fox_k_s, fox_v_s, fox_lf_s = _stream_outputs(sample)
    mem_shape = (DEPTH, B, MEM_TOKENS, MEM_HEADS, MEM_HEAD_DIM)
    return (y_p, y_s, pool_p, fox_k_p, fox_v_p, fox_lf_p, mem_k_p.reshape(mem_shape),
            mem_v_p.reshape(mem_shape), pool_s, fox_k_s, fox_v_s, fox_lf_s)
```
